```python
import math
import numpy as np
import jax
import jax.numpy as jnp
from jax import lax

D_MODEL = 2048
BATCH = 4
SEQ = 4096
DEPTH = 2

GRID_W = 64
CTX_LEN = 256
EPS = 1e-6

HY_WIDTH = D_MODEL // 2
HY_ORDER = 2
HY_BANDS = 16
HY_EMB = 2 * HY_BANDS + 1
HY_FFN = 64
HY_MIN_DECAY = -3.0701134573253943
HY_MAX_DECAY = -15.35056728662697

ATT_HD = 128
ATT_HEADS = (D_MODEL // 2) // ATT_HD
ATT_KV_HEADS = ATT_HEADS // 4
ATT_GROUP = ATT_HEADS // ATT_KV_HEADS
ATT_WINDOW = 128
ATT_BLOCK = 128
ROPE_BASE = 10000.0

ML_HEADS = 4
ML_V = (D_MODEL // 2) // ML_HEADS
ML_QK = ML_V // 2
ML_CHUNK = 64

N_BRANCH = 3
BRANCH_W = D_MODEL // 2

FFN_DENSE = 5632
N_EXPERTS = 8
TOP_K = 2
FFN_EXPERT = 7168
MOE_BLOCK = 256

IN_SIZES = (3 * HY_WIDTH,
            ATT_HEADS * ATT_HD,
            ATT_KV_HEADS * ATT_HD,
            ATT_KV_HEADS * ATT_HD,
            ML_HEADS * ML_QK,
            ML_HEADS * ML_QK,
            ML_HEADS * ML_V,
            ML_HEADS * ML_V,
            2 * 2 * ML_HEADS,
            N_BRANCH * D_MODEL)
IN_COLS = sum(IN_SIZES)

kernel_name = 'hybrid_hyena_swa_mlstm_moe_dit'


def rms_norm(x, g):
    xf = x.astype(jnp.float32)
    y = xf * lax.rsqrt(jnp.mean(xf * xf, axis=-1, keepdims=True) + EPS)
    return (y * g.astype(jnp.float32)).astype(x.dtype)


def modulate(x, shift, scale):
    return x * (1 + scale) + shift


def split_cols(z, sizes):
    return jnp.split(z, np.cumsum(sizes)[:-1].tolist(), axis=-1)


def swiglu(x, wg, wu, wd):
    return (jax.nn.silu(x @ wg) * (x @ wu)) @ wd


def short_conv3(x, w, b):
    xp = jnp.pad(x, ((0, 0), (1, 1), (0, 0)))
    return w[0] * xp[:, :-2] + w[1] * xp[:, 1:-1] + w[2] * xp[:, 2:] + b


def hyena_kernel(L, w1, b1, w2, b2, w3, freq):
    f32 = jnp.float32
    t = jnp.arange(L, dtype=f32) / L
    bands = jnp.arange(1, HY_BANDS + 1, dtype=f32)
    ang = (2.0 * math.pi) * t[:, None] * bands[None, :]
    feats = jnp.concatenate([t[:, None], jnp.sin(ang), jnp.cos(ang)], axis=-1)
    h = jnp.sin(freq[0].astype(f32) * (feats @ w1.astype(f32) + b1.astype(f32)))
    h = jnp.sin(freq[1].astype(f32) * (h @ w2.astype(f32) + b2.astype(f32)))
    h = (h @ w3.astype(f32)).reshape(L, 2, HY_ORDER, HY_WIDTH)
    deltas = jnp.abs(jnp.linspace(HY_MIN_DECAY, HY_MAX_DECAY, HY_WIDTH, dtype=f32))
    h = h * jnp.exp(-t[:, None] * deltas[None, :])[:, None, None, :]
    fwd, bwd = h[:, 0], h[:, 1]
    return jnp.concatenate([fwd, jnp.zeros_like(fwd[:1]), jnp.flip(bwd[1:], axis=0)], axis=0)


def fft_long_conv(u, k):
    L = u.shape[1]
    n = 2 * L
    uf = jnp.fft.rfft(u, n=n, axis=1)
    kf = jnp.fft.rfft(k, n=n, axis=0)
    return jnp.fft.irfft(uf * kf[None], n=n, axis=1)[:, :L]


def hyena_branch(z, short_w, short_b, w1, b1, w2, b2, w3, freq, bias):
    L = z.shape[1]
    z = short_conv3(z, short_w, short_b)
    v, x1, x2 = jnp.split(z.astype(jnp.float32), 3, axis=-1)
    kern = hyena_kernel(L, w1, b1, w2, b2, w3, freq)
    y = v
    for o, gate in enumerate((x1, x2)):
        y = gate * (fft_long_conv(y, kern[:, o]) + bias[o].astype(jnp.float32) * y)
    return y.astype(z.dtype)


def grid_positions(L):
    rows_n = L // GRID_W
    rows = jnp.repeat(jnp.arange(rows_n), GRID_W)
    cols = jnp.tile(jnp.arange(GRID_W), rows_n)
    return rows, cols


def axial_rope(x, rows, cols):
    half = ATT_HD // 2
    nf = half // 2
    inv = ROPE_BASE ** (-jnp.arange(nf, dtype=jnp.float32) / nf)

    def rot(xs, pos):
        ang = pos.astype(jnp.float32)[:, None] * inv[None, :]
        cos = jnp.cos(ang)[None, :, None, :].astype(xs.dtype)
        sin = jnp.sin(ang)[None, :, None, :].astype(xs.dtype)
        x1, x2 = xs[..., :nf], xs[..., nf:]
        return jnp.concatenate([x1 * cos - x2 * sin, x2 * cos + x1 * sin], axis=-1)

    return jnp.concatenate([rot(x[..., :half], rows), rot(x[..., half:], cols)], axis=-1)


def window_attention(q, k, v, kc, vc, sink):
    B, L, Hkv, G, hd = q.shape
    nb = L // ATT_BLOCK
    qb = q.reshape(B, nb, ATT_BLOCK, Hkv, G, hd)

    def windows(a):
        ap = jnp.pad(a, ((0, 0), (ATT_BLOCK, ATT_BLOCK), (0, 0), (0, 0))).reshape(B, nb + 2, ATT_BLOCK, Hkv, hd)
        return jnp.concatenate([ap[:, :-2], ap[:, 1:-1], ap[:, 2:]], axis=2)

    kw, vw = windows(k), windows(v)
    blk = jnp.arange(nb)[:, None, None] * ATT_BLOCK
    qi = blk + jnp.arange(ATT_BLOCK)[None, :, None]
    kj = blk - ATT_BLOCK + jnp.arange(3 * ATT_BLOCK)[None, None, :]
    mask = (jnp.abs(qi - kj) <= ATT_WINDOW) & (kj >= 0) & (kj < L)
    scale = ATT_HD ** -0.5
    s_win = jnp.einsum('bnqhgd,bnkhd->bhgnqk', qb, kw).astype(jnp.float32) * scale
    s_win = jnp.where(mask, s_win, jnp.finfo(jnp.float32).min)
    s_ctx = jnp.einsum('bnqhgd,bchd->bhgnqc', qb, kc).astype(jnp.float32) * scale
    s_sink = jnp.broadcast_to(sink[None, :, :, None, None, None], s_win.shape[:-1] + (1,))
    p = jax.nn.softmax(jnp.concatenate([s_win, s_ctx, s_sink], axis=-1), axis=-1).astype(v.dtype)
    nw = 3 * ATT_BLOCK
    nc = kc.shape[1]
    o = (jnp.einsum('bhgnqk,bnkhd->bnqhgd', p[..., :nw], vw)
         + jnp.einsum('bhgnqc,bchd->bnqhgd', p[..., nw:nw + nc], vc))
    return o.reshape(B, L, Hkv * G * hd)


def context_attention(q, k, v, sink):
    B, Lc = q.shape[:2]
    s = jnp.einsum('bqhgd,bkhd->bhgqk', q, k).astype(jnp.float32) * ATT_HD ** -0.5
    s_sink = jnp.broadcast_to(sink[None, :, :, None, None], s.shape[:-1] + (1,))
    p = jax.nn.softmax(jnp.concatenate([s, s_sink], axis=-1), axis=-1)[..., :-1].astype(v.dtype)
    return jnp.einsum('bhgqk,bkhd->bqhgd', p, v).reshape(B, Lc, ATT_HEADS * ATT_HD)


def mlstm_inputs(zq, zk, zv, zg, gate_b):
    B, L, _ = zq.shape

    def heads(z, d):
        return z.reshape(B, L, ML_HEADS, d).transpose(0, 2, 1, 3).astype(jnp.float32)

    q = heads(zq, ML_QK) * (ML_QK ** -0.5)
    k = heads(zk, ML_QK)
    v = heads(zv, ML_V)
    g = zg.reshape(B, L, 2, 2, ML_HEADS).astype(jnp.float32) + gate_b.astype(jnp.float32)
    g = g.transpose(2, 3, 0, 4, 1)
    return q, k, v, g[:, 0], jax.nn.log_sigmoid(g[:, 1])


def mlstm_chunkwise(q, k, v, log_i, log_f, state):
    B, H, L, _ = q.shape
    dv = v.shape[-1]
    nc = L // ML_CHUNK

    def chunks(a):
        return jnp.moveaxis(a.reshape(B, H, nc, ML_CHUNK, *a.shape[3:]), 2, 0)

    causal = jnp.tril(jnp.ones((ML_CHUNK, ML_CHUNK), dtype=bool))

    def step(carry, xs):
        C, n, m = carry
        qc, kc, vc, li, lf = xs
        b = jnp.cumsum(lf, axis=-1)
        dmat = jnp.where(causal, b[..., :, None] - b[..., None, :] + li[..., None, :], -jnp.inf)
        inter = b + m[..., None]
        m_t = jnp.maximum(inter, jnp.max(dmat, axis=-1))
        w_intra = jnp.exp(dmat - m_t[..., None])
        w_inter = jnp.exp(inter - m_t)
        s = jnp.einsum('bhtd,bhsd->bhts', qc, kc) * w_intra
        num = (jnp.einsum('bhts,bhsv->bhtv', s, vc)
               + w_inter[..., None] * jnp.einsum('bhvd,bhtd->bhtv', C, qc))
        den = jnp.sum(s, axis=-1) + w_inter * jnp.einsum('bhd,bhtd->bht', n, qc)
        h = num / jnp.maximum(jnp.abs(den), jnp.exp(-m_t))[..., None]
        b_end = b[..., -1]
        g = b_end[..., None] - b + li
        m_new = jnp.maximum(b_end + m, jnp.max(g, axis=-1))
        w_s = jnp.exp(g - m_new[..., None])
        w_c = jnp.exp(b_end + m - m_new)
        C = w_c[..., None, None] * C + jnp.einsum('bhsv,bhsd->bhvd', vc * w_s[..., None], kc)
        n = w_c[..., None] * n + jnp.einsum('bhs,bhsd->bhd', w_s, kc)
        return (C, n, m_new), h

    xs = (chunks(q), chunks(k), chunks(v), chunks(log_i), chunks(log_f))
    state, hs = lax.scan(step, state, xs)
    return state, jnp.moveaxis(hs, 0, 2).reshape(B, H, L, dv)


def directed(a, d):
    return a if d == 0 else jnp.flip(a, axis=2)


def mlstm_bidirectional(lat, ctx):
    ql, kl, vl, il, fl = lat
    qc, kc, vc, ic, fc = ctx
    B = ql.shape[0]
    h_lat = jnp.zeros(vl.shape, jnp.float32)
    h_ctx = jnp.zeros(vc.shape, jnp.float32)
    for d in range(2):
        state0 = (jnp.zeros((B, ML_HEADS, ML_V, ML_QK), jnp.float32),
                  jnp.zeros((B, ML_HEADS, ML_QK), jnp.float32),
                  jnp.zeros((B, ML_HEADS), jnp.float32))
        state_c, hc = mlstm_chunkwise(directed(qc, d), directed(kc, d), directed(vc, d),
                                      directed(ic[d], d), directed(fc[d], d), state0)
        _, hl = mlstm_chunkwise(directed(ql, d), directed(kl, d), directed(vl, d),
                                directed(il[d], d), directed(fl[d], d), state_c)
        h_ctx = h_ctx + directed(hc, d)
        h_lat = h_lat + directed(hl, d)
    return h_lat, h_ctx


def mlstm_output(h, zo, g):
    B, H, L, dv = h.shape
    hn = h * lax.rsqrt(jnp.mean(h * h, axis=-1, keepdims=True) + EPS)
    hn = hn.transpose(0, 2, 1, 3).reshape(B, L, H * dv) * g.astype(jnp.float32)
    return (hn * jax.nn.sigmoid(zo.astype(jnp.float32))).astype(zo.dtype)


def merge_branches(a, b, c, z_gate, w_branch, w_out):
    g = jnp.split(jax.nn.sigmoid(z_gate), N_BRANCH, axis=-1)
    y = g[0] * (a @ w_branch[0]) + g[1] * (b @ w_branch[1]) + g[2] * (c @ w_branch[2])
    return y @ w_out


def token_mixer(u_lat, u_ctx, w_in, hy_short_w, hy_short_b, hy_w1, hy_b1, hy_w2, hy_b2, hy_w3,
                hy_freq, hy_bias, att_sink, ml_gate_b, ml_norm_g, w_branch, w_out, need_ctx):
    B, L, _ = u_lat.shape
    Lc = u_ctx.shape[1]
    zl = split_cols(u_lat @ w_in, IN_SIZES)
    zc = split_cols(u_ctx @ w_in, IN_SIZES)
    hy = (hy_short_w, hy_short_b, hy_w1, hy_b1, hy_w2, hy_b2, hy_w3, hy_freq, hy_bias)

    a_lat = hyena_branch(zl[0], *hy)

    rows, cols = grid_positions(L)
    q = axial_rope(zl[1].reshape(B, L, ATT_HEADS, ATT_HD), rows, cols)
    q = q.reshape(B, L, ATT_KV_HEADS, ATT_GROUP, ATT_HD)
    k = axial_rope(zl[2].reshape(B, L, ATT_KV_HEADS, ATT_HD), rows, cols)
    v = zl[3].reshape(B, L, ATT_KV_HEADS, ATT_HD)
    kc = zc[2].reshape(B, Lc, ATT_KV_HEADS, ATT_HD)
    vc = zc[3].reshape(B, Lc, ATT_KV_HEADS, ATT_HD)
    sink = att_sink.astype(jnp.float32).reshape(ATT_KV_HEADS, ATT_GROUP)
    b_lat = window_attention(q, k, v, kc, vc, sink)

    ml_lat = mlstm_inputs(zl[4], zl[5], zl[6], zl[8], ml_gate_b)
    ml_ctx = mlstm_inputs(zc[4], zc[5], zc[6], zc[8], ml_gate_b)
    h_lat, h_ctx = mlstm_bidirectional(ml_lat, ml_ctx)
    c_lat = mlstm_output(h_lat, zl[7], ml_norm_g)

    y_lat = merge_branches(a_lat, b_lat, c_lat, zl[9], w_branch, w_out)
    if not need_ctx:
        return y_lat, None
    a_ctx = hyena_branch(zc[0], *hy)
    qc = zc[1].reshape(B, Lc, ATT_KV_HEADS, ATT_GROUP, ATT_HD)
    b_ctx = context_attention(qc, kc, vc, sink)
    c_ctx_out = mlstm_output(h_ctx, zc[7], ml_norm_g)
    y_ctx = merge_branches(a_ctx, b_ctx, c_ctx_out, zc[9], w_branch, w_out)
    return y_lat, y_ctx


def moe_swiglu(x, w_router, wg, wu, wd):
    B, L, Dm = x.shape
    T = B * L
    xt = x.reshape(T, Dm)
    logits = (xt @ w_router).astype(jnp.float32)
    top_v, top_i = lax.top_k(logits, TOP_K)
    probs = jax.nn.softmax(top_v, axis=-1)
    A = T * TOP_K
    e_flat = top_i.reshape(A)
    order = jnp.argsort(e_flat)
    e_sorted = e_flat[order]
    tok_sorted = (order // TOP_K).astype(jnp.int32)
    p_sorted = probs.reshape(A)[order]
    counts = jnp.bincount(e_flat, length=N_EXPERTS)
    padded = (counts + MOE_BLOCK - 1) // MOE_BLOCK * MOE_BLOCK
    pad_end = jnp.cumsum(padded)
    pad_start = pad_end - padded
    grp_start = jnp.cumsum(counts) - counts
    slot = pad_start[e_sorted] + jnp.arange(A) - grp_start[e_sorted]
    n_blocks = (A + MOE_BLOCK - 1) // MOE_BLOCK + N_EXPERTS
    P = n_blocks * MOE_BLOCK
    slot_tok = jnp.full((P,), T, jnp.int32).at[slot].set(tok_sorted)
    slot_p = jnp.zeros((P,), jnp.float32).at[slot].set(p_sorted)
    blk_expert = jnp.minimum(jnp.searchsorted(pad_end, jnp.arange(n_blocks) * MOE_BLOCK, side='right'),
                             N_EXPERTS - 1)
    x_pad = jnp.concatenate([xt, jnp.zeros((1, Dm), xt.dtype)], axis=0)

    def expert_block(args):
        tok, p, e = args
        xb = x_pad[tok]
        hb = jax.nn.silu(xb @ wg[e]) * (xb @ wu[e])
        return (hb @ wd[e]) * p[:, None].astype(xb.dtype)

    yb = lax.map(expert_block, (slot_tok.reshape(n_blocks, MOE_BLOCK),
                                slot_p.reshape(n_blocks, MOE_BLOCK), blk_expert))
    y = jnp.zeros((T + 1, Dm), x.dtype).at[slot_tok].add(yb.reshape(P, Dm))
    return y[:T].reshape(B, L, Dm)


def setup_inputs(seed: int = 0) -> dict:
    key = jax.random.key(seed)
    keys = iter(jax.random.split(key, 32))

    def nrm(shape, scale):
        return jax.random.normal(next(keys), shape, jnp.float32) * scale

    D = D_MODEL
    n_dense = (DEPTH + 1) // 2
    n_moe = DEPTH // 2
    ml_gate_b = nrm((DEPTH, 2, 2, ML_HEADS), 0.1)
    ml_gate_b = ml_gate_b.at[:, :, 1].add(jnp.linspace(3.0, 6.0, ML_HEADS))
    return {
        'x': nrm((BATCH, SEQ, D), 1.0),
        'c': nrm((BATCH, D), 1.0),
        'ctx': nrm((BATCH, CTX_LEN, D), 1.0),
        'c_ctx': nrm((D,), 1.0),
        'w_mod': nrm((DEPTH, D, 6 * D), 0.5 * D ** -0.5),
        'b_mod': nrm((DEPTH, 6 * D), 0.02),
        'norm_mix_g': 1.0 + nrm((DEPTH, D), 0.02),
        'norm_ffn_g': 1.0 + nrm((DEPTH, D), 0.02),
        'w_in': nrm((DEPTH, D, IN_COLS), D ** -0.5),
        'hy_short_w': nrm((DEPTH, 3, 3 * HY_WIDTH), 3 ** -0.5),
        'hy_short_b': nrm((DEPTH, 3 * HY_WIDTH), 0.02),
        'hy_w1': nrm((DEPTH, HY_EMB, HY_FFN), HY_EMB ** -0.5),
        'hy_b1': nrm((DEPTH, HY_FFN), 0.1),
        'hy_w2': nrm((DEPTH, HY_FFN, HY_FFN), HY_FFN ** -0.5),
        'hy_b2': nrm((DEPTH, HY_FFN), 0.1),
        'hy_w3': nrm((DEPTH, HY_FFN, 2 * HY_ORDER * HY_WIDTH), 0.005),
        'hy_freq': 1.0 + nrm((DEPTH, 2, HY_FFN), 0.02),
        'hy_bias': nrm((DEPTH, HY_ORDER, HY_WIDTH), 0.5),
        'att_sink': nrm((DEPTH, ATT_HEADS), 0.5),
        'ml_gate_b': ml_gate_b,
        'ml_norm_g': 1.0 + nrm((DEPTH, ML_HEADS * ML_V), 0.02),
        'w_branch': nrm((DEPTH, N_BRANCH, BRANCH_W, D), BRANCH_W ** -0.5),
        'w_out': nrm((DEPTH, D, D), D ** -0.5),
        'ffn_wg': nrm((n_dense, D, FFN_DENSE), D ** -0.5),
        'ffn_wu': nrm((n_dense, D, FFN_DENSE), D ** -0.5),
        'ffn_wd': nrm((n_dense, FFN_DENSE, D), FFN_DENSE ** -0.5),
        'moe_router': nrm((n_moe, D, N_EXPERTS), D ** -0.5),
        'moe_wg': nrm((n_moe, N_EXPERTS, D, FFN_EXPERT), D ** -0.5),
        'moe_wu': nrm((n_moe, N_EXPERTS, D, FFN_EXPERT), D ** -0.5),
        'moe_wd': nrm((n_moe, N_EXPERTS, FFN_EXPERT, D), FFN_EXPERT ** -0.5),
        'final_g': 1.0 + nrm((D,), 0.02),
    }


def reference(x, c, ctx, c_ctx, w_mod, b_mod, norm_mix_g, norm_ffn_g, w_in, hy_short_w, hy_short_b,
              hy_w1, hy_b1, hy_w2, hy_b2, hy_w3, hy_freq, hy_bias, att_sink, ml_gate_b, ml_norm_g,
              w_branch, w_out, ffn_wg, ffn_wu, ffn_wd, moe_router, moe_wg, moe_wu, moe_wd, final_g):
    h_ctx = ctx
    for layer in range(DEPTH):
        last = layer == DEPTH - 1
        mod = jax.nn.silu(c) @ w_mod[layer] + b_mod[layer]
        mod_c = jax.nn.silu(c_ctx) @ w_mod[layer] + b_mod[layer]
        sh1, sc1, g1, sh2, sc2, g2 = jnp.split(mod[:, None, :], 6, axis=-1)
        csh1, csc1, cg1, csh2, csc2, cg2 = jnp.split(mod_c, 6, axis=-1)

        u_lat = modulate(rms_norm(x, norm_mix_g[layer]), sh1, sc1)
        u_ctx = modulate(rms_norm(h_ctx, norm_mix_g[layer]), csh1, csc1)
        y_lat, y_ctx = token_mixer(u_lat, u_ctx, w_in[layer], hy_short_w[layer], hy_short_b[layer],
                                   hy_w1[layer], hy_b1[layer], hy_w2[layer], hy_b2[layer], hy_w3[layer],
                                   hy_freq[layer], hy_bias[layer], att_sink[layer], ml_gate_b[layer],
                                   ml_norm_g[layer], w_branch[layer], w_out[layer], not last)
        x = x + g1 * y_lat

        f_in = modulate(rms_norm(x, norm_ffn_g[layer]), sh2, sc2)
        n_ctx = 0
        if not last:
            h_ctx = h_ctx + cg1 * y_ctx
            f_ctx = modulate(rms_norm(h_ctx, norm_ffn_g[layer]), csh2, csc2)
            f_in = jnp.concatenate([f_ctx, f_in], axis=1)
            n_ctx = f_ctx.shape[1]
        if layer % 2 == 0:
            f_out = swiglu(f_in, ffn_wg[layer // 2], ffn_wu[layer // 2], ffn_wd[layer // 2])
        else:
            f_out = moe_swiglu(f_in, moe_router[layer // 2], moe_wg[layer // 2],
                               moe_wu[layer // 2], moe_wd[layer // 2])
        x = x + g2 * f_out[:, n_ctx:]
        if not last:
            h_ctx = h_ctx + cg2 * f_out[:, :n_ctx]
    return rms_norm(x, final_g)
```

```python
import functools
import math

import numpy as np
import jax
import jax.numpy as jnp
from jax import lax
from jax.experimental import pallas as pl
from jax.experimental.pallas import tpu as pltpu

f32 = jnp.float32
bf16 = jnp.bfloat16

D_MODEL = 2048
BATCH = 4
SEQ = 4096
DEPTH = 2
GRID_W = 64
CTX_LEN = 256
EPS = 1e-6

HY_WIDTH = D_MODEL // 2
HY_ORDER = 2
HY_BANDS = 16
HY_EMB = 2 * HY_BANDS + 1
HY_FFN = 64
HY_MIN_DECAY = -3.0701134573253943
HY_MAX_DECAY = -15.35056728662697

ATT_HD = 128
ATT_HEADS = 8
ATT_KV_HEADS = 2
ATT_GROUP = 4
ATT_WINDOW = 128
ATT_BLOCK = 128
ROPE_BASE = 10000.0

ML_HEADS = 4
ML_V = 256
ML_QK = 128
ML_CHUNK = 256

N_BRANCH = 3
BRANCH_W = D_MODEL // 2
FFN_DENSE = 5632
N_EXPERTS = 8
TOP_K = 2
FFN_EXPERT = 7168
MOE_TILE = 512

NCTX = BATCH * CTX_LEN
NLAT = BATCH * SEQ
R = NCTX + NLAT

Z_HY = 0
Z_Q = 3072
Z_MV = 4096
Z_MO = 5120
Z_MERGE = 6144
Z_K = 12288
Z_V = 12544
Z_MQ = 12800
Z_MK = 13312
Z_GATE = 13824
Z_COLS = 14336

VMEM_LIMIT = 56 * 1024 * 1024


def _cp(*sem):
    return pltpu.CompilerParams(dimension_semantics=sem, vmem_limit_bytes=VMEM_LIMIT)


def _modrow(i, tm):
    return jnp.where(i < NCTX // tm, BATCH, (i - NCTX // tm) // (SEQ // tm))


def _mod_kernel(c_ref, w_ref, b_ref, o_ref):
    c = c_ref[...]
    a = (c * jax.nn.sigmoid(c)).astype(bf16)
    o_ref[...] = jnp.dot(a, w_ref[...].astype(bf16), preferred_element_type=f32) + b_ref[...]


def _modulation(c_all, w_mod, b_mod):
    tn = 1024
    n = 6 * D_MODEL
    return pl.pallas_call(
        _mod_kernel,
        grid=(DEPTH, n // tn),
        in_specs=[pl.BlockSpec((8, D_MODEL), lambda l, j: (0, 0)),
                  pl.BlockSpec((None, D_MODEL, tn), lambda l, j: (l, 0, j)),
                  pl.BlockSpec((None, 1, tn), lambda l, j: (l, 0, j))],
        out_specs=pl.BlockSpec((None, 8, tn), lambda l, j: (l, 0, j)),
        out_shape=jax.ShapeDtypeStruct((DEPTH, 8, n), f32),
        compiler_params=_cp("arbitrary", "arbitrary"),
        name="modulation",
    )(c_all, w_mod, b_mod.reshape(DEPTH, 1, n))


def _normmod_kernel(x_ref, g_ref, sh_ref, sc_ref, o_ref):
    x = x_ref[...]
    y = x * lax.rsqrt(jnp.mean(x * x, axis=-1, keepdims=True) + EPS) * g_ref[...]
    o_ref[...] = (y * (1.0 + sc_ref[...]) + sh_ref[...]).astype(o_ref.dtype)


def _normmod(xa, g, modt, which, t0, out_dtype=bf16):
    tm = 512
    nt = R // tm - t0
    return pl.pallas_call(
        _normmod_kernel,
        grid=(nt,),
        in_specs=[pl.BlockSpec((tm, D_MODEL), lambda i: (i + t0, 0)),
                  pl.BlockSpec((1, D_MODEL), lambda i: (0, 0)),
                  pl.BlockSpec((None, 1, D_MODEL), lambda i: (_modrow(i + t0, tm), 0, which)),
                  pl.BlockSpec((None, 1, D_MODEL), lambda i: (_modrow(i + t0, tm), 0, which + 1))],
        out_specs=pl.BlockSpec((tm, D_MODEL), lambda i: (i + t0, 0)),
        out_shape=jax.ShapeDtypeStruct((R, D_MODEL), out_dtype),
        compiler_params=_cp("arbitrary"),
        name="normmod",
    )(xa, g.reshape(1, D_MODEL), modt, modt)


def _mm_kernel(a_ref, w_ref, o_ref):
    o_ref[...] = jnp.dot(a_ref[...], w_ref[...], preferred_element_type=f32).astype(o_ref.dtype)


def _in_proj(u, w):
    tm, tn = 1024, 512
    return pl.pallas_call(
        _mm_kernel,
        grid=(Z_COLS // tn, R // tm),
        in_specs=[pl.BlockSpec((tm, D_MODEL), lambda j, i: (i, 0)),
                  pl.BlockSpec((D_MODEL, tn), lambda j, i: (0, j))],
        out_specs=pl.BlockSpec((tm, tn), lambda j, i: (i, j)),
        out_shape=jax.ShapeDtypeStruct((R, Z_COLS), f32),
        compiler_params=_cp("arbitrary", "arbitrary"),
        name="in_proj",
    )(u, w)


def _mm_resid_kernel(a_ref, w_ref, x_ref, g_ref, o_ref):
    y = jnp.dot(a_ref[...], w_ref[...], preferred_element_type=f32)
    o_ref[...] = x_ref[...] + g_ref[...] * y


def _mm_resid(a, w, xa, modt, which, t0):
    tm, tn = 512, 512
    k = a.shape[1]
    nt = R // tm - t0
    nj = D_MODEL // tn
    return pl.pallas_call(
        _mm_resid_kernel,
        grid=(nj, nt),
        in_specs=[pl.BlockSpec((tm, k), lambda j, i: (i + t0, 0)),
                  pl.BlockSpec((k, tn), lambda j, i: (0, j)),
                  pl.BlockSpec((tm, tn), lambda j, i: (i + t0, j)),
                  pl.BlockSpec((None, 1, tn), lambda j, i: (_modrow(i + t0, tm), 0, which * nj + j))],
        out_specs=pl.BlockSpec((tm, tn), lambda j, i: (i + t0, j)),
        out_shape=jax.ShapeDtypeStruct((R, D_MODEL), f32),
        compiler_params=_cp("arbitrary", "arbitrary"),
        name="mm_resid",
    )(a, w, xa, modt)


def _swiglu_kernel(a_ref, wg_ref, wu_ref, o_ref):
    a = a_ref[...]
    g = jnp.dot(a, wg_ref[...], preferred_element_type=f32)
    u = jnp.dot(a, wu_ref[...], preferred_element_type=f32)
    o_ref[...] = (g * jax.nn.sigmoid(g) * u).astype(o_ref.dtype)


def _ffn_up(f, wg, wu):
    tm, tn = 1024, 512
    n = wg.shape[1]
    return pl.pallas_call(
        _swiglu_kernel,
        grid=(n // tn, R // tm),
        in_specs=[pl.BlockSpec((tm, D_MODEL), lambda j, i: (i, 0)),
                  pl.BlockSpec((D_MODEL, tn), lambda j, i: (0, j)),
                  pl.BlockSpec((D_MODEL, tn), lambda j, i: (0, j))],
        out_specs=pl.BlockSpec((tm, tn), lambda j, i: (i, j)),
        out_shape=jax.ShapeDtypeStruct((R, n), bf16),
        compiler_params=_cp("arbitrary", "arbitrary"),
        name="ffn_up",
    )(f, wg, wu)


def _merge_kernel(a_ref, b_ref, c_ref, w_ref, g0_ref, g1_ref, g2_ref, o_ref):
    y = jax.nn.sigmoid(g0_ref[...]) * jnp.dot(a_ref[...], w_ref[0], preferred_element_type=f32)
    y += jax.nn.sigmoid(g1_ref[...]) * jnp.dot(b_ref[...], w_ref[1], preferred_element_type=f32)
    y += jax.nn.sigmoid(g2_ref[...]) * jnp.dot(c_ref[...], w_ref[2], preferred_element_type=f32)
    o_ref[...] = y.astype(o_ref.dtype)


def _merge(a, b, c, wb, z, t0):
    tm, tn = 512, 512
    nt = R // tm - t0
    nj = D_MODEL // tn
    act = pl.BlockSpec((tm, BRANCH_W), lambda j, i: (i + t0, 0))

    def gate(br):
        return pl.BlockSpec((tm, tn), lambda j, i: (i + t0, (Z_MERGE + br * D_MODEL) // tn + j))

    return pl.pallas_call(
        _merge_kernel,
        grid=(nj, nt),
        in_specs=[act, act, act,
                  pl.BlockSpec((N_BRANCH, BRANCH_W, tn), lambda j, i: (0, 0, j)),
                  gate(0), gate(1), gate(2)],
        out_specs=pl.BlockSpec((tm, tn), lambda j, i: (i + t0, j)),
        out_shape=jax.ShapeDtypeStruct((R, D_MODEL), bf16),
        compiler_params=_cp("arbitrary", "arbitrary"),
        name="merge",
    )(a, b, c, wb, z, z, z)


def _short_conv3(x, w, b):
    xp = jnp.pad(x, ((0, 0), (1, 1), (0, 0)))
    return w[0] * xp[:, :-2] + w[1] * xp[:, 1:-1] + w[2] * xp[:, 2:] + b


def _hyena_filter(L, w1, b1, w2, b2, w3, freq):
    t = jnp.arange(L, dtype=f32) / L
    bands = jnp.arange(1, HY_BANDS + 1, dtype=f32)
    ang = (2.0 * math.pi) * t[:, None] * bands[None, :]
    feats = jnp.concatenate([t[:, None], jnp.sin(ang), jnp.cos(ang)], axis=-1)
    h = jnp.sin(freq[0] * (feats @ w1 + b1))
    h = jnp.sin(freq[1] * (h @ w2 + b2))
    h = (h @ w3).reshape(L, 2, HY_ORDER, HY_WIDTH)
    deltas = jnp.abs(jnp.linspace(HY_MIN_DECAY, HY_MAX_DECAY, HY_WIDTH, dtype=f32))
    h = h * jnp.exp(-t[:, None] * deltas[None, :])[:, None, None, :]
    fwd, bwd = h[:, 0], h[:, 1]
    return jnp.concatenate([fwd, jnp.zeros_like(fwd[:1]), jnp.flip(bwd[1:], axis=0)], axis=0)


def _fft_long_conv(u, k):
    L = u.shape[1]
    n = 2 * L
    uf = jnp.fft.rfft(u, n=n, axis=1)
    kf = jnp.fft.rfft(k, n=n, axis=0)
    return jnp.fft.irfft(uf * kf[None], n=n, axis=1)[:, :L]


def _hyena_branch(z, short_w, short_b, w1, b1, w2, b2, w3, freq, bias):
    L = z.shape[1]
    z = _short_conv3(z, short_w, short_b)
    v, x1, x2 = jnp.split(z, 3, axis=-1)
    kern = _hyena_filter(L, w1, b1, w2, b2, w3, freq)
    y = v
    for o, gate in enumerate((x1, x2)):
        y = gate * (_fft_long_conv(y, kern[:, o]) + bias[o] * y)
    return y


def _rope_tables():
    half = ATT_HD // 2
    nf = half // 2
    inv = ROPE_BASE ** (-jnp.arange(nf, dtype=f32) / nf)
    pos = jnp.arange(SEQ)
    rows = (pos // GRID_W).astype(f32)[:, None] * inv[None, :]
    cols = (pos % GRID_W).astype(f32)[:, None] * inv[None, :]
    zero = jnp.zeros_like(rows)
    cos = jnp.concatenate([jnp.cos(rows)] * 2 + [jnp.cos(cols)] * 2, axis=-1)
    sin_up = jnp.concatenate([-jnp.sin(rows), zero, -jnp.sin(cols), zero], axis=-1)
    sin_dn = jnp.concatenate([zero, jnp.sin(rows), zero, jnp.sin(cols)], axis=-1)
    return cos, sin_up, sin_dn


def _rope_kernel(q_ref, k_ref, cos_ref, su_ref, sd_ref, qo_ref, ko_ref):
    cos, su, sd = cos_ref[...], su_ref[...], sd_ref[...]

    def rot(x):
        return x * cos + pltpu.roll(x, 96, 1) * su + pltpu.roll(x, 32, 1) * sd

    for h in range(ATT_HEADS):
        s = slice(h * ATT_HD, (h + 1) * ATT_HD)
        qo_ref[:, s] = rot(q_ref[:, s]).astype(qo_ref.dtype)
    for h in range(ATT_KV_HEADS):
        s = slice(h * ATT_HD, (h + 1) * ATT_HD)
        ko_ref[:, s] = rot(k_ref[:, s]).astype(ko_ref.dtype)


def _rope(z, tables):
    tm = 512
    t0 = NCTX // tm
    nq = ATT_HEADS * ATT_HD
    nk = ATT_KV_HEADS * ATT_HD
    tab = pl.BlockSpec((tm, ATT_HD), lambda i: (i % (SEQ // tm), 0))
    return pl.pallas_call(
        _rope_kernel,
        grid=(NLAT // tm,),
        in_specs=[pl.BlockSpec((tm, nq), lambda i: (i + t0, Z_Q // nq)),
                  pl.BlockSpec((tm, nk), lambda i: (i + t0, Z_K // nk)),
                  tab, tab, tab],
        out_specs=[pl.BlockSpec((tm, nq), lambda i: (i + t0, 0)),
                   pl.BlockSpec((tm, nk), lambda i: (i + t0, 0))],
        out_shape=[jax.ShapeDtypeStruct((R, nq), bf16), jax.ShapeDtypeStruct((R, nk), bf16)],
        compiler_params=_cp("arbitrary"),
        name="rope",
    )(z, z, *tables)


_ATT_SCALE = ATT_HD ** -0.5
_NEG = float(np.finfo(np.float32).min)
_NT = (((1,), (1,)), ((), ()))


def _sink_column(sink_ref, h, rows):
    rg = lax.broadcasted_iota(jnp.int32, (rows * ATT_GROUP, 1), 0) // rows
    col = jnp.full((rows * ATT_GROUP, 1), sink_ref[h * ATT_GROUP + ATT_GROUP - 1], f32)
    for g in range(ATT_GROUP - 2, -1, -1):
        col = jnp.where(rg == g, sink_ref[h * ATT_GROUP + g], col)
    return col


def _attn_kernel(sink_ref, q_ref, kp_ref, kc_ref, kn_ref, vp_ref, vc_ref, vn_ref, kx_ref, vx_ref, o_ref):
    i = pl.program_id(1)
    nb = pl.num_programs(1)
    blk = ATT_BLOCK
    r = lax.broadcasted_iota(jnp.int32, (ATT_GROUP * blk, 3 * blk), 0) % blk
    c = lax.broadcasted_iota(jnp.int32, (ATT_GROUP * blk, 3 * blk), 1)
    lo = jnp.where(i > 0, 0, blk)
    hi = jnp.where(i < nb - 1, 3 * blk, 2 * blk)
    valid = (c >= r) & (c <= r + 2 * ATT_WINDOW) & (c >= lo) & (c < hi)
    for h in range(ATT_KV_HEADS):
        hs = slice(h * ATT_HD, (h + 1) * ATT_HD)
        k_win = jnp.concatenate([kp_ref[:, hs], kc_ref[:, hs], kn_ref[:, hs]], axis=0)
        v_win = jnp.concatenate([vp_ref[:, hs], vc_ref[:, hs], vn_ref[:, hs]], axis=0).astype(bf16)
        k_ctx = kx_ref[:, hs].astype(bf16)
        v_ctx = vx_ref[:, hs].astype(bf16)
        q = jnp.concatenate([q_ref[:, (h * ATT_GROUP + g) * ATT_HD:(h * ATT_GROUP + g + 1) * ATT_HD]
                             for g in range(ATT_GROUP)], axis=0)
        s_win = lax.dot_general(q, k_win, _NT, preferred_element_type=f32) * _ATT_SCALE
        s_win = jnp.where(valid, s_win, _NEG)
        s_ctx = lax.dot_general(q, k_ctx, _NT, preferred_element_type=f32) * _ATT_SCALE
        sink = _sink_column(sink_ref, h, blk)
        m = jnp.maximum(jnp.maximum(jnp.max(s_win, axis=-1, keepdims=True),
                                    jnp.max(s_ctx, axis=-1, keepdims=True)), sink)
        p_win = jnp.exp(s_win - m)
        p_ctx = jnp.exp(s_ctx - m)
        den = (jnp.sum(p_win, axis=-1, keepdims=True) + jnp.sum(p_ctx, axis=-1, keepdims=True)
               + jnp.exp(sink - m))
        o = (jnp.dot(p_win.astype(bf16), v_win, preferred_element_type=f32)
             + jnp.dot(p_ctx.astype(bf16), v_ctx, preferred_element_type=f32)) / den
        for g in range(ATT_GROUP):
            cs = slice((h * ATT_GROUP + g) * ATT_HD, (h * ATT_GROUP + g + 1) * ATT_HD)
            o_ref[:, cs] = o[g * blk:(g + 1) * blk].astype(o_ref.dtype)


def _window_attention(qr, kr, z, sink):
    blk = ATT_BLOCK
    nb = SEQ // blk
    t0 = NCTX // blk
    nkv = ATT_KV_HEADS * ATT_HD

    def krow(off):
        return lambda b, i: (t0 + b * nb + jnp.clip(i + off, 0, nb - 1), 0)

    def vrow(off):
        return lambda b, i: (t0 + b * nb + jnp.clip(i + off, 0, nb - 1), Z_V // nkv)

    return pl.pallas_call(
        _attn_kernel,
        grid=(BATCH, nb),
        in_specs=[pl.BlockSpec(memory_space=pltpu.SMEM),
                  pl.BlockSpec((blk, ATT_HEADS * ATT_HD), lambda b, i: (t0 + b * nb + i, 0)),
                  pl.BlockSpec((blk, nkv), krow(-1)),
                  pl.BlockSpec((blk, nkv), krow(0)),
                  pl.BlockSpec((blk, nkv), krow(1)),
                  pl.BlockSpec((blk, nkv), vrow(-1)),
                  pl.BlockSpec((blk, nkv), vrow(0)),
                  pl.BlockSpec((blk, nkv), vrow(1)),
                  pl.BlockSpec((CTX_LEN, nkv), lambda b, i: (b, Z_K // nkv)),
                  pl.BlockSpec((CTX_LEN, nkv), lambda b, i: (b, Z_V // nkv))],
        out_specs=pl.BlockSpec((blk, ATT_HEADS * ATT_HD), lambda b, i: (b * nb + i, 0)),
        out_shape=jax.ShapeDtypeStruct((NLAT, ATT_HEADS * ATT_HD), bf16),
        compiler_params=_cp("arbitrary", "arbitrary"),
        name="window_attention",
    )(sink, qr, kr, kr, kr, z, z, z, z, z)


def _ctx_attn_kernel(sink_ref, q_ref, k_ref, v_ref, o_ref):
    for h in range(ATT_KV_HEADS):
        hs = slice(h * ATT_HD, (h + 1) * ATT_HD)
        k = k_ref[:, hs].astype(bf16)
        v = v_ref[:, hs].astype(bf16)
        q = jnp.concatenate([q_ref[:, (h * ATT_GROUP + g) * ATT_HD:(h * ATT_GROUP + g + 1) * ATT_HD]
                             for g in range(ATT_GROUP)], axis=0).astype(bf16)
        s = lax.dot_general(q, k, _NT, preferred_element_type=f32) * _ATT_SCALE
        sink = _sink_column(sink_ref, h, CTX_LEN)
        m = jnp.maximum(jnp.max(s, axis=-1, keepdims=True), sink)
        p = jnp.exp(s - m)
        den = jnp.sum(p, axis=-1, keepdims=True) + jnp.exp(sink - m)
        o = jnp.dot(p.astype(bf16), v, preferred_element_type=f32) / den
        for g in range(ATT_GROUP):
            cs = slice((h * ATT_GROUP + g) * ATT_HD, (h * ATT_GROUP + g + 1) * ATT_HD)
            o_ref[:, cs] = o[g * CTX_LEN:(g + 1) * CTX_LEN].astype(o_ref.dtype)


def _context_attention(z, sink):
    nq = ATT_HEADS * ATT_HD
    nkv = ATT_KV_HEADS * ATT_HD
    return pl.pallas_call(
        _ctx_attn_kernel,
        grid=(BATCH,),
        in_specs=[pl.BlockSpec(memory_space=pltpu.SMEM),
                  pl.BlockSpec((CTX_LEN, nq), lambda b: (b, Z_Q // nq)),
                  pl.BlockSpec((CTX_LEN, nkv), lambda b: (b, Z_K // nkv)),
                  pl.BlockSpec((CTX_LEN, nkv), lambda b: (b, Z_V // nkv))],
        out_specs=pl.BlockSpec((CTX_LEN, nq), lambda b: (b, 0)),
        out_shape=jax.ShapeDtypeStruct((NCTX, nq), bf16),
        compiler_params=_cp("arbitrary"),
        name="context_attention",
    )(sink, z, z, z)


_ML_SCALE = ML_QK ** -0.5
_TN = (((0,), (0,)), ((), ()))


def _split3(x):
    x1 = x.astype(bf16)
    r1 = x - x1.astype(f32)
    x2 = r1.astype(bf16)
    x3 = (r1 - x2.astype(f32)).astype(bf16)
    return x1, x2, x3


def _mlstm_kernel(q_ref, k_ref, v_ref, g_ref, gb_ref, o_ref, ct_ref, n_ref, m_ref):
    d = pl.program_id(1)
    c = pl.program_id(2)
    ch = ML_CHUNK

    @pl.when(c == 0)
    def _():
        ct_ref[...] = jnp.zeros_like(ct_ref)
        n_ref[...] = jnp.zeros_like(n_ref)
        m_ref[...] = jnp.zeros_like(m_ref)

    g = g_ref[...] + gb_ref[...]
    row = lax.broadcasted_iota(jnp.int32, (ch, ch), 0)
    col = lax.broadcasted_iota(jnp.int32, (ch, ch), 1)
    tri = (row - col) * (1 - 2 * d) >= 0
    tri_b = tri.astype(f32).astype(bf16)
    lf = jax.nn.log_sigmoid(g)
    l1, l2, l3 = _split3(lf)
    bcol = (jnp.dot(tri_b, l1, preferred_element_type=f32)
            + jnp.dot(tri_b, l2, preferred_element_type=f32)
            + jnp.dot(tri_b, l3, preferred_element_type=f32))
    bend = jnp.where(d == 0, bcol[ch - 1:ch, :], bcol[0:1, :])
    g_t = g.T
    b_t = bcol.T

    for h in range(ML_HEADS):
        qf = q_ref[:, h * ML_QK:(h + 1) * ML_QK] * _ML_SCALE
        kf = k_ref[:, h * ML_QK:(h + 1) * ML_QK]
        vf = v_ref[:, h * ML_V:(h + 1) * ML_V]
        q = qf.astype(bf16)
        k = kf.astype(bf16)
        li_c = g[:, h:h + 1]
        b_c = bcol[:, ML_HEADS + h:ML_HEADS + h + 1]
        li_r = g_t[h:h + 1, :]
        b_r = b_t[ML_HEADS + h:ML_HEADS + h + 1, :]
        m_prev = m_ref[h][:, 0:1]
        dmat = jnp.where(tri, b_c - b_r + li_r, -jnp.inf)
        inter = b_c + m_prev
        m_t = jnp.maximum(inter, jnp.max(dmat, axis=-1, keepdims=True))
        w_intra = jnp.exp(dmat - m_t)
        w_inter = jnp.exp(inter - m_t)
        s = lax.dot_general(q, k, _NT, preferred_element_type=f32) * w_intra
        qc = jnp.dot(q, ct_ref[h].astype(bf16), preferred_element_type=f32)
        num = jnp.dot(s.astype(bf16), vf.astype(bf16), preferred_element_type=f32) + w_inter * qc
        den = (jnp.sum(s, axis=-1, keepdims=True)
               + w_inter * jnp.sum(qf * n_ref[h], axis=-1, keepdims=True))
        o_ref[:, h * ML_V:(h + 1) * ML_V] = num / jnp.maximum(jnp.abs(den), jnp.exp(-m_t))

        b_e = bend[:, ML_HEADS + h:ML_HEADS + h + 1]
        g_c = b_e - b_c + li_c
        m_new = jnp.maximum(b_e + m_prev, jnp.max(g_c, axis=0, keepdims=True))
        w_s = jnp.exp(g_c - m_new)
        w_c = jnp.exp(b_e + m_prev - m_new)
        vw = (vf * w_s).astype(bf16)
        ct_ref[h] = w_c * ct_ref[h] + jnp.dot(kf.T.astype(bf16), vw, preferred_element_type=f32)
        n_ref[h] = w_c * n_ref[h] + jnp.sum(kf * w_s, axis=0, keepdims=True)
        m_ref[h] = jnp.broadcast_to(m_new, (1, 128))


def _mlstm(z, gate_b):
    ch = ML_CHUNK
    ncl = SEQ // ch
    nsteps = ncl + CTX_LEN // ch
    assert CTX_LEN == ch

    def rt(b, d, c):
        lat = NCTX // ch + b * ncl + jnp.where(d == 0, c - 1, ncl - c)
        return jnp.where(c == 0, b, lat)

    nq = ML_HEADS * ML_QK
    nv = ML_HEADS * ML_V
    return pl.pallas_call(
        _mlstm_kernel,
        grid=(BATCH, 2, nsteps),
        in_specs=[pl.BlockSpec((ch, nq), lambda b, d, c: (rt(b, d, c), Z_MQ // nq)),
                  pl.BlockSpec((ch, nq), lambda b, d, c: (rt(b, d, c), Z_MK // nq)),
                  pl.BlockSpec((ch, nv), lambda b, d, c: (rt(b, d, c), Z_MV // nv)),
                  pl.BlockSpec((ch, 128), lambda b, d, c: (rt(b, d, c), Z_GATE // 128 + d)),
                  pl.BlockSpec((None, 1, 128), lambda b, d, c: (d, 0, 0))],
        out_specs=pl.BlockSpec((None, ch, nv), lambda b, d, c: (d, rt(b, d, c), 0)),
        out_shape=jax.ShapeDtypeStruct((2, R, nv), f32),
        scratch_shapes=[pltpu.VMEM((ML_HEADS, ML_QK, ML_V), f32),
                        pltpu.VMEM((ML_HEADS, 1, ML_QK), f32),
                        pltpu.VMEM((ML_HEADS, 1, 128), f32)],
        compiler_params=_cp("arbitrary", "arbitrary", "arbitrary"),
        name="mlstm",
    )(z, z, z, z, gate_b)


def _mlstm_out_kernel(h_ref, zo_ref, g_ref, o_ref):
    for h in range(ML_HEADS):
        s = slice(h * ML_V, (h + 1) * ML_V)
        x = h_ref[0, :, s] + h_ref[1, :, s]
        xn = x * lax.rsqrt(jnp.mean(x * x, axis=-1, keepdims=True) + EPS) * g_ref[:, s]
        o_ref[:, s] = (xn * jax.nn.sigmoid(zo_ref[:, s])).astype(o_ref.dtype)


def _mlstm_out(hh, z, g, t0):
    tm = 512
    nv = ML_HEADS * ML_V
    nt = R // tm - t0
    return pl.pallas_call(
        _mlstm_out_kernel,
        grid=(nt,),
        in_specs=[pl.BlockSpec((2, tm, nv), lambda i: (0, i + t0, 0)),
                  pl.BlockSpec((tm, nv), lambda i: (i + t0, Z_MO // nv)),
                  pl.BlockSpec((1, nv), lambda i: (0, 0))],
        out_specs=pl.BlockSpec((tm, nv), lambda i: (i + t0, 0)),
        out_shape=jax.ShapeDtypeStruct((R, nv), bf16),
        compiler_params=_cp("arbitrary"),
        name="mlstm_out",
    )(hh, z, g.reshape(1, nv))


def _router_kernel(f_ref, w_ref, idx_ref, p_ref):
    logits = jnp.dot(f_ref[...].astype(bf16), w_ref[...], preferred_element_type=f32)
    lane = lax.broadcasted_iota(jnp.int32, logits.shape, 1).astype(f32)
    logits = jnp.where(lane < N_EXPERTS, logits, -jnp.inf)
    v1 = jnp.max(logits, axis=-1, keepdims=True)
    i1 = jnp.min(jnp.where(logits == v1, lane, 128.0), axis=-1, keepdims=True)
    rest = jnp.where(lane == i1, -jnp.inf, logits)
    v2 = jnp.max(rest, axis=-1, keepdims=True)
    i2 = jnp.min(jnp.where(rest == v2, lane, 128.0), axis=-1, keepdims=True)
    e = jnp.exp(v2 - v1)
    p1 = 1.0 / (1.0 + e)
    p2 = e / (1.0 + e)
    idx_ref[...] = jnp.where(lane == 0, i1, jnp.where(lane == 1, i2, 0.0)).astype(jnp.int32)
    p_ref[...] = jnp.where(lane == 0, p1, jnp.where(lane == 1, p2, 0.0))


def _router(f, w_router):
    tm = 512
    t0 = NCTX // tm
    w = jnp.pad(w_router, ((0, 0), (0, 128 - N_EXPERTS))).astype(bf16)
    return pl.pallas_call(
        _router_kernel,
        grid=(NLAT // tm,),
        in_specs=[pl.BlockSpec((tm, D_MODEL), lambda i: (i + t0, 0)),
                  pl.BlockSpec((D_MODEL, 128), lambda i: (0, 0))],
        out_specs=[pl.BlockSpec((tm, 128), lambda i: (i, 0)),
                   pl.BlockSpec((tm, 128), lambda i: (i, 0))],
        out_shape=[jax.ShapeDtypeStruct((NLAT, 128), jnp.int32),
                   jax.ShapeDtypeStruct((NLAT, 128), f32)],
        compiler_params=_cp("arbitrary"),
        name="router",
    )(f, w)


def _gather_kernel(tok_ref, src_ref, o_ref, sem):
    base = pl.program_id(0) * MOE_TILE

    def copy(r):
        return pltpu.make_async_copy(src_ref.at[pl.ds(tok_ref[base + r], 1)], o_ref.at[pl.ds(r, 1)], sem)

    def start(r, carry):
        copy(r).start()
        return carry

    def wait(r, carry):
        copy(r).wait()
        return carry

    lax.fori_loop(0, MOE_TILE, start, 0)
    lax.fori_loop(0, MOE_TILE, wait, 0)


def _gather_rows(src, rows, n_out):
    width = src.shape[1]
    return pl.pallas_call(
        _gather_kernel,
        grid_spec=pltpu.PrefetchScalarGridSpec(
            num_scalar_prefetch=1,
            grid=(n_out // MOE_TILE,),
            in_specs=[pl.BlockSpec(memory_space=pl.ANY)],
            out_specs=pl.BlockSpec((MOE_TILE, width), lambda i, tok: (i, 0)),
            scratch_shapes=[pltpu.SemaphoreType.DMA(())]),
        out_shape=jax.ShapeDtypeStruct((n_out, width), src.dtype),
        compiler_params=_cp("arbitrary"),
        name="gather_rows",
    )(rows, src)


def _moe_up_kernel(be_ref, nu_ref, a_ref, wg_ref, wu_ref, o_ref):
    i = pl.program_id(1)

    @pl.when(i < nu_ref[0])
    def _():
        a = a_ref[...].astype(bf16)
        g = jnp.dot(a, wg_ref[...], preferred_element_type=f32)
        u = jnp.dot(a, wu_ref[...], preferred_element_type=f32)
        o_ref[...] = (g * jax.nn.sigmoid(g) * u).astype(o_ref.dtype)

    @pl.when(i >= nu_ref[0])
    def _():
        o_ref[...] = jnp.zeros_like(o_ref)


def _moe_up(xs, wg, wu, blk_expert, n_used):
    tn = 512
    nblk = xs.shape[0] // MOE_TILE

    def row(j, i, be, nu):
        return (jnp.minimum(i, nu[0] - 1), 0)

    def wmap(j, i, be, nu):
        return (be[jnp.minimum(i, nu[0] - 1)], 0, j)

    return pl.pallas_call(
        _moe_up_kernel,
        grid_spec=pltpu.PrefetchScalarGridSpec(
            num_scalar_prefetch=2,
            grid=(FFN_EXPERT // tn, nblk),
            in_specs=[pl.BlockSpec((MOE_TILE, D_MODEL), row),
                      pl.BlockSpec((None, D_MODEL, tn), wmap),
                      pl.BlockSpec((None, D_MODEL, tn), wmap)],
            out_specs=pl.BlockSpec((MOE_TILE, tn), lambda j, i, be, nu: (i, j))),
        out_shape=jax.ShapeDtypeStruct((xs.shape[0], FFN_EXPERT), bf16),
        compiler_params=_cp("arbitrary", "arbitrary"),
        name="moe_up",
    )(blk_expert, n_used, xs, wg, wu)


def _moe_down_kernel(be_ref, nu_ref, a_ref, w_ref, o_ref):
    i = pl.program_id(1)

    @pl.when(i < nu_ref[0])
    def _():
        o_ref[...] = jnp.dot(a_ref[...], w_ref[...], preferred_element_type=f32)

    @pl.when(i >= nu_ref[0])
    def _():
        o_ref[...] = jnp.zeros_like(o_ref)


def _moe_down(hs, wd, blk_expert, n_used):
    tn = 512
    nblk = hs.shape[0] // MOE_TILE

    def row(j, i, be, nu):
        return (jnp.minimum(i, nu[0] - 1), 0)

    def wmap(j, i, be, nu):
        return (be[jnp.minimum(i, nu[0] - 1)], 0, j)

    return pl.pallas_call(
        _moe_down_kernel,
        grid_spec=pltpu.PrefetchScalarGridSpec(
            num_scalar_prefetch=2,
            grid=(D_MODEL // tn, nblk),
            in_specs=[pl.BlockSpec((MOE_TILE, FFN_EXPERT), row),
                      pl.BlockSpec((None, FFN_EXPERT, tn), wmap)],
            out_specs=pl.BlockSpec((MOE_TILE, tn), lambda j, i, be, nu: (i, j))),
        out_shape=jax.ShapeDtypeStruct((hs.shape[0], D_MODEL), f32),
        compiler_params=_cp("arbitrary", "arbitrary"),
        name="moe_down",
    )(blk_expert, n_used, hs, wd)


_COMBINE_TM = 256


def _combine_kernel(slot_ref, yb_ref, x_ref, p_ref, g2_ref, fg_ref, o_ref, buf, sem):
    base = pl.program_id(0) * _COMBINE_TM

    def copy(r, k):
        s = slot_ref[(base + r) * TOP_K + k]
        return pltpu.make_async_copy(yb_ref.at[pl.ds(s, 1)], buf.at[k, pl.ds(r, 1)], sem)

    def start(r, carry):
        copy(r, 0).start()
        copy(r, 1).start()
        return carry

    def wait(r, carry):
        copy(r, 0).wait()
        copy(r, 1).wait()
        return carry

    lax.fori_loop(0, _COMBINE_TM, start, 0)
    lax.fori_loop(0, _COMBINE_TM, wait, 0)
    p = p_ref[...]
    y = buf[0] * p[:, 0:1] + buf[1] * p[:, 1:2]
    x = x_ref[...] + g2_ref[...] * y
    o_ref[...] = x * lax.rsqrt(jnp.mean(x * x, axis=-1, keepdims=True) + EPS) * fg_ref[...]


def _combine_final(slot, yb, xa, probs, modt, final_g):
    tm = _COMBINE_TM
    t0 = NCTX // tm
    return pl.pallas_call(
        _combine_kernel,
        grid_spec=pltpu.PrefetchScalarGridSpec(
            num_scalar_prefetch=1,
            grid=(NLAT // tm,),
            in_specs=[pl.BlockSpec(memory_space=pl.ANY),
                      pl.BlockSpec((tm, D_MODEL), lambda i, s: (i + t0, 0)),
                      pl.BlockSpec((tm, 128), lambda i, s: (i, 0)),
                      pl.BlockSpec((None, 1, D_MODEL), lambda i, s: (i // (SEQ // tm), 0, 5)),
                      pl.BlockSpec((1, D_MODEL), lambda i, s: (0, 0))],
            out_specs=pl.BlockSpec((tm, D_MODEL), lambda i, s: (i, 0)),
            scratch_shapes=[pltpu.VMEM((TOP_K, tm, D_MODEL), f32),
                            pltpu.SemaphoreType.DMA(())]),
        out_shape=jax.ShapeDtypeStruct((NLAT, D_MODEL), f32),
        compiler_params=_cp("arbitrary"),
        name="moe_combine_final_norm",
    )(slot, yb, xa, probs, modt, final_g.reshape(1, D_MODEL))


def _moe_routing(top_i):
    a = NLAT * TOP_K
    e_flat = top_i.reshape(a)
    onehot = (e_flat[:, None] == jnp.arange(N_EXPERTS)[None, :]).astype(jnp.int32)
    csum = jnp.cumsum(onehot, axis=0)
    rank = jnp.sum(onehot * csum, axis=1) - 1
    counts = csum[-1]
    padded = (counts + MOE_TILE - 1) // MOE_TILE * MOE_TILE
    pad_end = jnp.cumsum(padded)
    pad_start = pad_end - padded
    slot = (pad_start[e_flat] + rank).astype(jnp.int32)
    n_rows = a + N_EXPERTS * MOE_TILE
    nblk = n_rows // MOE_TILE
    slot_tok = jnp.zeros((n_rows,), jnp.int32).at[slot].set(jnp.arange(a, dtype=jnp.int32) // TOP_K)
    blk_expert = jnp.minimum(jnp.searchsorted(pad_end, jnp.arange(nblk) * MOE_TILE, side='right'),
                             N_EXPERTS - 1).astype(jnp.int32)
    n_used = (pad_end[-1:] // MOE_TILE).astype(jnp.int32)
    return slot, slot_tok, blk_expert, n_used, n_rows


def _in_proj_weight(w, gate_b):
    o = np.cumsum((0,) + (3072, 1024, 256, 256, 512, 512, 1024, 1024, 16, 6144))
    hy, q, k, v, mq, mk, mv, mo, gt, mg = [w[:, o[i]:o[i + 1]] for i in range(10)]
    pad = jnp.zeros((D_MODEL, 128 - 2 * ML_HEADS), w.dtype)
    gates = [jnp.concatenate([gt[:, 8 * d:8 * d + 8], pad], axis=1) for d in range(2)]
    wz = jnp.concatenate([hy, q, mv, mo, mg, k, v, mq, mk] + gates
                         + [jnp.zeros((D_MODEL, Z_COLS - Z_GATE - 256), w.dtype)], axis=1).astype(bf16)
    gb = jnp.pad(gate_b.reshape(2, 1, 2 * ML_HEADS), ((0, 0), (0, 0), (0, 128 - 2 * ML_HEADS)))
    return wz, gb


def kernel(x, c, ctx, c_ctx, w_mod, b_mod, norm_mix_g, norm_ffn_g, w_in, hy_short_w, hy_short_b, hy_w1, hy_b1, hy_w2, hy_b2, hy_w3, hy_freq, hy_bias, att_sink, ml_gate_b, ml_norm_g, w_branch, w_out, ffn_wg, ffn_wu, ffn_wd, moe_router, moe_wg, moe_wu, moe_wd, final_g):
    xa = jnp.concatenate([ctx.reshape(NCTX, D_MODEL), x.reshape(NLAT, D_MODEL)], axis=0)
    c_all = jnp.concatenate([c, c_ctx[None], jnp.zeros((8 - BATCH - 1, D_MODEL), f32)], axis=0)
    mod = _modulation(c_all, w_mod, b_mod)
    rope_tabs = _rope_tables()
    out = None
    for layer in range(DEPTH):
        last = layer == DEPTH - 1
        modt = mod[layer].reshape(8, 1, 6 * D_MODEL)
        t_mix = NCTX // 512 if last else 0

        u = _normmod(xa, norm_mix_g[layer], modt, 0, 0)
        wz, gate_b = _in_proj_weight(w_in[layer], ml_gate_b[layer])
        z = _in_proj(u, wz)

        hy = (hy_short_w[layer], hy_short_b[layer], hy_w1[layer], hy_b1[layer], hy_w2[layer],
              hy_b2[layer], hy_w3[layer], hy_freq[layer], hy_bias[layer])
        a_lat = _hyena_branch(z[NCTX:, :3 * HY_WIDTH].reshape(BATCH, SEQ, 3 * HY_WIDTH), *hy)
        a_lat = a_lat.reshape(NLAT, HY_WIDTH).astype(bf16)

        qr, kr = _rope(z, rope_tabs)
        b_lat = _window_attention(qr, kr, z, att_sink[layer])

        hh = _mlstm(z, gate_b)
        c_all_rows = _mlstm_out(hh, z, ml_norm_g[layer], t_mix)

        if last:
            a_ctx = jnp.zeros((NCTX, HY_WIDTH), bf16)
            b_ctx = jnp.zeros((NCTX, BRANCH_W), bf16)
        else:
            a_ctx = _hyena_branch(z[:NCTX, :3 * HY_WIDTH].reshape(BATCH, CTX_LEN, 3 * HY_WIDTH), *hy)
            a_ctx = a_ctx.reshape(NCTX, HY_WIDTH).astype(bf16)
            b_ctx = _context_attention(z, att_sink[layer])
        a_rows = jnp.concatenate([a_ctx, a_lat], axis=0)
        b_rows = jnp.concatenate([b_ctx, b_lat], axis=0)

        ymid = _merge(a_rows, b_rows, c_all_rows, w_branch[layer].astype(bf16), z, t_mix)
        xa = _mm_resid(ymid, w_out[layer].astype(bf16), xa, modt, 2, t_mix)

        f = _normmod(xa, norm_ffn_g[layer], modt, 3, t_mix, bf16 if layer % 2 == 0 else f32)
        if layer % 2 == 0:
            e = layer // 2
            hmid = _ffn_up(f, ffn_wg[e].astype(bf16), ffn_wu[e].astype(bf16))
            xa = _mm_resid(hmid, ffn_wd[e].astype(bf16), xa, modt, 5, 0)
        else:
            e = layer // 2
            top_i, probs = _router(f, moe_router[e])
            slot, slot_tok, blk_expert, n_used, n_rows = _moe_routing(top_i[:, :TOP_K])
            xs = _gather_rows(f, slot_tok + NCTX, n_rows)
            hs = _moe_up(xs, moe_wg[e].astype(bf16), moe_wu[e].astype(bf16), blk_expert, n_used)
            yb = _moe_down(hs, moe_wd[e].astype(bf16), blk_expert, n_used)
            assert last
            out = _combine_final(slot, yb, xa, probs, modt, final_g)
    return out.reshape(BATCH, SEQ, D_MODEL)
```

```python
import functools
import math

import numpy as np
import jax
import jax.numpy as jnp
from jax import lax
from jax.experimental import pallas as pl
from jax.experimental.pallas import tpu as pltpu

f32 = jnp.float32
bf16 = jnp.bfloat16

D_MODEL = 2048
BATCH = 4
SEQ = 4096
DEPTH = 2
GRID_W = 64
CTX_LEN = 256
EPS = 1e-6

HY_WIDTH = D_MODEL // 2
HY_ORDER = 2
HY_BANDS = 16
HY_EMB = 2 * HY_BANDS + 1
HY_FFN = 64
HY_MIN_DECAY = -3.0701134573253943
HY_MAX_DECAY = -15.35056728662697
HY_N2 = 128
HY_SLABS = 4

ATT_HD = 128
ATT_HEADS = 8
ATT_KV_HEADS = 2
ATT_GROUP = 4
ATT_WINDOW = 128
ATT_BLOCK = 128
ROPE_BASE = 10000.0

ML_HEADS = 4
ML_V = 256
ML_QK = 128
ML_CHUNK = 256

N_BRANCH = 3
BRANCH_W = D_MODEL // 2
FFN_DENSE = 5632
N_EXPERTS = 8
TOP_K = 2
FFN_EXPERT = 7168
MOE_TILE = 512

NCTX = BATCH * CTX_LEN
NLAT = BATCH * SEQ
R = NCTX + NLAT

Z_HY = 0
Z_Q = 3072
Z_MV = 4096
Z_MO = 5120
Z_MERGE = 6144
Z_K = 12288
Z_V = 12544
Z_MQ = 12800
Z_MK = 13312
Z_GATE = 13824
Z_COLS = 14336

VMEM_LIMIT = 56 * 1024 * 1024


def _cp(*sem, vmem=VMEM_LIMIT):
    return pltpu.CompilerParams(dimension_semantics=sem, vmem_limit_bytes=vmem)


def _modrow(i, tm):
    return jnp.where(i >= NLAT // tm, BATCH, i // (SEQ // tm))


def _mod_kernel(c_ref, w_ref, b_ref, o_ref):
    c = c_ref[...]
    a = (c * jax.nn.sigmoid(c)).astype(bf16)
    o_ref[...] = jnp.dot(a, w_ref[...].astype(bf16), preferred_element_type=f32) + b_ref[...]


def _modulation(c_all, w_mod, b_mod):
    tn = 1024
    n = 6 * D_MODEL
    return pl.pallas_call(
        _mod_kernel,
        grid=(DEPTH, n // tn),
        in_specs=[pl.BlockSpec((8, D_MODEL), lambda l, j: (0, 0)),
                  pl.BlockSpec((None, D_MODEL, tn), lambda l, j: (l, 0, j)),
                  pl.BlockSpec((None, 1, tn), lambda l, j: (l, 0, j))],
        out_specs=pl.BlockSpec((None, 8, tn), lambda l, j: (l, 0, j)),
        out_shape=jax.ShapeDtypeStruct((DEPTH, 8, n), f32),
        compiler_params=_cp("arbitrary", "arbitrary"),
        name="modulation",
    )(c_all, w_mod, b_mod.reshape(DEPTH, 1, n))


def _normmod_kernel(x_ref, g_ref, sh_ref, sc_ref, o_ref):
    x = x_ref[...]
    y = x * lax.rsqrt(jnp.mean(x * x, axis=-1, keepdims=True) + EPS) * g_ref[...]
    o_ref[...] = (y * (1.0 + sc_ref[...]) + sh_ref[...]).astype(o_ref.dtype)


def _normmod(xa, g, modt, which, nrows, out_dtype=bf16):
    tm = 512
    return pl.pallas_call(
        _normmod_kernel,
        grid=(nrows // tm,),
        in_specs=[pl.BlockSpec((tm, D_MODEL), lambda i: (i, 0)),
                  pl.BlockSpec((1, D_MODEL), lambda i: (0, 0)),
                  pl.BlockSpec((None, 1, D_MODEL), lambda i: (_modrow(i, tm), 0, which)),
                  pl.BlockSpec((None, 1, D_MODEL), lambda i: (_modrow(i, tm), 0, which + 1))],
        out_specs=pl.BlockSpec((tm, D_MODEL), lambda i: (i, 0)),
        out_shape=jax.ShapeDtypeStruct((nrows, D_MODEL), out_dtype),
        compiler_params=_cp("arbitrary"),
        name="normmod",
    )(xa, g.reshape(1, D_MODEL), modt, modt)


def _mm_kernel(a_ref, w_ref, o_ref):
    o_ref[...] = jnp.dot(a_ref[...], w_ref[...], preferred_element_type=f32).astype(o_ref.dtype)


def _in_proj(u, w):
    tm, tn = 1024, 512
    return pl.pallas_call(
        _mm_kernel,
        grid=(Z_COLS // tn, R // tm),
        in_specs=[pl.BlockSpec((tm, D_MODEL), lambda j, i: (i, 0)),
                  pl.BlockSpec((D_MODEL, tn), lambda j, i: (0, j))],
        out_specs=pl.BlockSpec((tm, tn), lambda j, i: (i, j)),
        out_shape=jax.ShapeDtypeStruct((R, Z_COLS), f32),
        compiler_params=_cp("arbitrary", "arbitrary"),
        name="in_proj",
    )(u, w)


def _mm_resid_kernel(a_ref, w_ref, x_ref, g_ref, o_ref):
    y = jnp.dot(a_ref[...], w_ref[...], preferred_element_type=f32)
    o_ref[...] = x_ref[...] + g_ref[...] * y


def _mm_resid(a, w, xa, modt, which, nrows):
    tm, tn = 512, 512
    k = a.shape[1]
    nj = D_MODEL // tn
    return pl.pallas_call(
        _mm_resid_kernel,
        grid=(nj, nrows // tm),
        in_specs=[pl.BlockSpec((tm, k), lambda j, i: (i, 0)),
                  pl.BlockSpec((k, tn), lambda j, i: (0, j)),
                  pl.BlockSpec((tm, tn), lambda j, i: (i, j)),
                  pl.BlockSpec((None, 1, tn), lambda j, i: (_modrow(i, tm), 0, which * nj + j))],
        out_specs=pl.BlockSpec((tm, tn), lambda j, i: (i, j)),
        out_shape=jax.ShapeDtypeStruct((nrows, D_MODEL), f32),
        compiler_params=_cp("arbitrary", "arbitrary"),
        name="mm_resid",
    )(a, w, xa, modt)


def _swiglu_kernel(a_ref, wg_ref, wu_ref, o_ref):
    a = a_ref[...]
    g = jnp.dot(a, wg_ref[...], preferred_element_type=f32)
    u = jnp.dot(a, wu_ref[...], preferred_element_type=f32)
    o_ref[...] = (g * jax.nn.sigmoid(g) * u).astype(o_ref.dtype)


def _ffn_up(f, wg, wu):
    tm, tn = 1024, 512
    n = wg.shape[1]
    return pl.pallas_call(
        _swiglu_kernel,
        grid=(n // tn, R // tm),
        in_specs=[pl.BlockSpec((tm, D_MODEL), lambda j, i: (i, 0)),
                  pl.BlockSpec((D_MODEL, tn), lambda j, i: (0, j)),
                  pl.BlockSpec((D_MODEL, tn), lambda j, i: (0, j))],
        out_specs=pl.BlockSpec((tm, tn), lambda j, i: (i, j)),
        out_shape=jax.ShapeDtypeStruct((R, n), bf16),
        compiler_params=_cp("arbitrary", "arbitrary"),
        name="ffn_up",
    )(f, wg, wu)


def _merge_kernel(a_ref, b_ref, c_ref, w_ref, g0_ref, g1_ref, g2_ref, o_ref):
    y = jax.nn.sigmoid(g0_ref[...]) * jnp.dot(a_ref[...], w_ref[0], preferred_element_type=f32)
    y += jax.nn.sigmoid(g1_ref[...]) * jnp.dot(b_ref[...], w_ref[1], preferred_element_type=f32)
    y += jax.nn.sigmoid(g2_ref[...]) * jnp.dot(c_ref[...], w_ref[2], preferred_element_type=f32)
    o_ref[...] = y.astype(o_ref.dtype)


def _merge(a, b, c, wb, z, nrows):
    tm, tn = 512, 512
    nj = D_MODEL // tn
    act = pl.BlockSpec((tm, BRANCH_W), lambda j, i: (i, 0))

    def gate(br):
        return pl.BlockSpec((tm, tn), lambda j, i: (i, (Z_MERGE + br * D_MODEL) // tn + j))

    return pl.pallas_call(
        _merge_kernel,
        grid=(nj, nrows // tm),
        in_specs=[act, act, act,
                  pl.BlockSpec((N_BRANCH, BRANCH_W, tn), lambda j, i: (0, 0, j)),
                  gate(0), gate(1), gate(2)],
        out_specs=pl.BlockSpec((tm, tn), lambda j, i: (i, j)),
        out_shape=jax.ShapeDtypeStruct((nrows, D_MODEL), bf16),
        compiler_params=_cp("arbitrary", "arbitrary"),
        name="merge",
    )(a, b, c, wb, z, z, z)


def _short_conv_kernel(x_ref, xp_ref, xn_ref, w_ref, b_ref, o_ref, *, tiles_per_seq):
    i = pl.program_id(0)
    tm = x_ref.shape[0]
    x = x_ref[...]
    first = i % tiles_per_seq == 0
    last = i % tiles_per_seq == tiles_per_seq - 1
    prev = jnp.where(first, 0.0, xp_ref[7:8, :])
    nxt = jnp.where(last, 0.0, xn_ref[0:1, :])
    row = lax.broadcasted_iota(jnp.int32, x.shape, 0)
    up = jnp.where(row == 0, prev, pltpu.roll(x, 1, 0))
    dn = jnp.where(row == tm - 1, nxt, pltpu.roll(x, tm - 1, 0))
    o_ref[...] = w_ref[0:1, :] * up + w_ref[1:2, :] * x + w_ref[2:3, :] * dn + b_ref[...]


def _short_conv(z, w, b, row0, nrows, seq_len):
    tm, tn = 256, 512
    t0 = row0 // tm
    nc = 3 * HY_WIDTH
    last8 = R // 8 - 1
    return pl.pallas_call(
        functools.partial(_short_conv_kernel, tiles_per_seq=seq_len // tm),
        grid=(nrows // tm, nc // tn),
        in_specs=[pl.BlockSpec((tm, tn), lambda i, j: (i + t0, j)),
                  pl.BlockSpec((8, tn), lambda i, j: (jnp.maximum((i + t0) * (tm // 8) - 1, 0), j)),
                  pl.BlockSpec((8, tn), lambda i, j: (jnp.minimum((i + t0 + 1) * (tm // 8), last8), j)),
                  pl.BlockSpec((3, tn), lambda i, j: (0, j)),
                  pl.BlockSpec((1, tn), lambda i, j: (0, j))],
        out_specs=pl.BlockSpec((tm, tn), lambda i, j: (i, j)),
        out_shape=jax.ShapeDtypeStruct((nrows, nc), f32),
        compiler_params=_cp("arbitrary", "arbitrary"),
        name="hyena_short_conv",
    )(z, z, z, w, b.reshape(1, nc))


def _hyfilt_kernel(w1_ref, b1_ref, w2_ref, b2_ref, w3_ref, fr_ref, dl_ref, o_ref, *, seq_len):
    tm = o_ref.shape[0]
    r = pl.program_id(0) * tm + lax.broadcasted_iota(jnp.int32, (tm, 1), 0)
    p = jnp.where(r < seq_len, r, 2 * seq_len - r)
    t = p.astype(f32) / seq_len
    lane = lax.broadcasted_iota(jnp.int32, (tm, 128), 1)
    band = jnp.where(lane <= HY_BANDS, lane, lane - HY_BANDS).astype(f32)
    ang = ((2.0 * math.pi) * t) * band
    feats = jnp.where(lane == 0, t,
                      jnp.where(lane <= HY_BANDS, jnp.sin(ang),
                                jnp.where(lane <= 2 * HY_BANDS, jnp.cos(ang), 0.0)))
    h = jnp.dot(feats.astype(bf16), w1_ref[...].astype(bf16), preferred_element_type=f32) + b1_ref[...]
    h = jnp.sin(fr_ref[0:1, :] * h)
    h = jnp.dot(h.astype(bf16), w2_ref[...].astype(bf16), preferred_element_type=f32) + b2_ref[...]
    h = jnp.sin(fr_ref[1:2, :] * h)
    k = jnp.dot(h.astype(bf16), w3_ref[...].astype(bf16), preferred_element_type=f32)
    k = k * jnp.exp(-t * dl_ref[...])
    o_ref[...] = jnp.where(r == seq_len, 0.0, k)


def _hyena_filter(seq_len, w1, b1, w2, b2, w3, freq):
    tm, tn = 512, 512
    tm = min(tm, seq_len)
    nc = HY_ORDER * HY_WIDTH
    deltas = jnp.abs(jnp.linspace(HY_MIN_DECAY, HY_MAX_DECAY, HY_WIDTH, dtype=f32))
    dl = jnp.tile(deltas, HY_ORDER).reshape(1, nc)
    w1p = jnp.pad(w1, ((0, 128 - HY_EMB), (0, 0)))
    per_dir = nc // tn
    full = lambda i, j: (0, 0)
    return pl.pallas_call(
        functools.partial(_hyfilt_kernel, seq_len=seq_len),
        grid=(2 * seq_len // tm, per_dir),
        in_specs=[pl.BlockSpec((128, HY_FFN), full),
                  pl.BlockSpec((1, HY_FFN), full),
                  pl.BlockSpec((HY_FFN, HY_FFN), full),
                  pl.BlockSpec((1, HY_FFN), full),
                  pl.BlockSpec((HY_FFN, tn), lambda i, j: (0, jnp.where(i >= seq_len // tm, per_dir, 0) + j)),
                  pl.BlockSpec((2, HY_FFN), full),
                  pl.BlockSpec((1, tn), lambda i, j: (0, j))],
        out_specs=pl.BlockSpec((tm, tn), lambda i, j: (i, j)),
        out_shape=jax.ShapeDtypeStruct((2 * seq_len, nc), f32),
        compiler_params=_cp("arbitrary", "arbitrary"),
        name="hyena_filter",
    )(w1p, b1.reshape(1, HY_FFN), w2, b2.reshape(1, HY_FFN), w3, freq, dl)


def _pass3(m):
    hi = m.astype(bf16)
    lo = (m - hi.astype(f32)).astype(bf16)
    return jnp.concatenate([hi, hi, lo], axis=-1)


def _rhs3(x):
    hi = x.astype(bf16)
    lo = (x - hi.astype(f32)).astype(bf16)
    return jnp.concatenate([hi, lo, hi], axis=0)


def _stack_complex(ar, ai):
    return jnp.concatenate([jnp.concatenate([ar, -ai], axis=-1),
                            jnp.concatenate([ai, ar], axis=-1)], axis=-2)


def _dft_tables(seq_len):
    n = 2 * seq_len
    n1 = n // HY_N2
    half = n1 // 2
    i1 = jnp.arange(n1, dtype=jnp.int32)
    ang1 = (2.0 * math.pi / n1) * ((i1[:, None] * i1[None, :]) % n1).astype(f32)
    c1, s1 = jnp.cos(ang1), jnp.sin(ang1)
    m2 = _pass3(_stack_complex(c1[:, :half], -s1[:, :half]))
    m2f = _pass3(jnp.concatenate([c1, -s1], axis=0))
    m8 = _pass3(_stack_complex(c1[:half, :] / n, s1[:half, :] / n))
    i2 = jnp.arange(HY_N2, dtype=jnp.int32)
    k = i1[:, None, None] + n1 * i2[None, :, None]
    ang = (2.0 * math.pi / n) * ((i2[None, None, :] * k) % n).astype(f32)
    c, s = jnp.cos(ang), jnp.sin(ang)
    g4 = _pass3(_stack_complex(c, -s))
    ct, st = jnp.swapaxes(c, 1, 2), jnp.swapaxes(s, 1, 2)
    g6 = _pass3(_stack_complex(ct, st))
    return dict(m2=m2, m2f=m2f, m8=m8, g4=g4, g6=g6)


def _store_halves(ref, rows, val):
    ref[0, rows, :] = val[:, :128]
    ref[1, rows, :] = val[:, 128:]


def _load_halves(ref, rows):
    return jnp.concatenate([ref[0, rows, :], ref[1, rows, :]], axis=1)


def _hyfft_kernel(xa_ref, xb_ref, m2_ref, g4_ref, kr_ref, ki_ref, br, bi, *, n1):
    s = pl.program_id(1)

    @pl.when(s == 0)
    def _():
        def body(n2, carry):
            rows = pl.ds(n2, n1, stride=HY_N2)
            rhs = _rhs3(jnp.concatenate([xa_ref[rows, :], xb_ref[rows, :]], axis=1))
            out = jnp.dot(m2_ref[...], rhs, preferred_element_type=f32)
            _store_halves(br, rows, out[:n1])
            _store_halves(bi, rows, out[n1:])
            return carry
        lax.fori_loop(0, HY_N2, body, 0)

    for j in range(HY_SLABS):
        rows = pl.ds(pl.multiple_of((s * HY_SLABS + j) * HY_N2, HY_N2), HY_N2)
        y = jnp.concatenate([_load_halves(br, rows), _load_halves(bi, rows)], axis=0)
        z = jnp.dot(g4_ref[j], _rhs3(y), preferred_element_type=f32)
        kr_ref[j * HY_N2:(j + 1) * HY_N2, :] = z[:HY_N2]
        ki_ref[j * HY_N2:(j + 1) * HY_N2, :] = z[HY_N2:]


def _hyena_filter_fft(kern, tabs):
    n, nc = kern.shape
    n1 = n // HY_N2
    tn = 256
    sl = HY_SLABS * HY_N2
    spec_out = pl.BlockSpec((sl, tn), lambda t, s: (s, t))
    return pl.pallas_call(
        functools.partial(_hyfft_kernel, n1=n1),
        grid=(nc // tn, n1 // HY_SLABS),
        in_specs=[pl.BlockSpec((n, 128), lambda t, s: (0, 2 * t)),
                  pl.BlockSpec((n, 128), lambda t, s: (0, 2 * t + 1)),
                  pl.BlockSpec(tabs["m2f"].shape, lambda t, s: (0, 0)),
                  pl.BlockSpec((HY_SLABS,) + tabs["g4"].shape[1:], lambda t, s: (s, 0, 0))],
        out_specs=[spec_out, spec_out],
        out_shape=[jax.ShapeDtypeStruct((n, nc), f32)] * 2,
        scratch_shapes=[pltpu.VMEM((2, n, 128), f32), pltpu.VMEM((2, n, 128), f32)],
        compiler_params=_cp("arbitrary", "arbitrary"),
        name="hyena_filter_fft",
    )(kern, kern, tabs["m2f"], tabs["g4"])


def _hyconv_kernel(x_ref, m2_ref, g4_ref, g6_ref, m8_ref, kr_ref, ki_ref, o_ref, br, bi, *, n1):
    s = pl.program_id(1)
    ns = pl.num_programs(1)
    half = n1 // 2
    cw = x_ref.shape[2]

    @pl.when(s == 0)
    def _():
        def body(n2, carry):
            rows = pl.ds(n2, half, stride=HY_N2)
            xr = jnp.concatenate([x_ref[0, rows, :], x_ref[2, rows, :]], axis=1)
            xi = jnp.concatenate([x_ref[1, rows, :], x_ref[3, rows, :]], axis=1)
            rhs = _rhs3(jnp.concatenate([xr, xi], axis=0))
            out = jnp.dot(m2_ref[...], rhs, preferred_element_type=f32)
            brows = pl.ds(n2, n1, stride=HY_N2)
            _store_halves(br, brows, out[:n1])
            _store_halves(bi, brows, out[n1:])
            return carry
        lax.fori_loop(0, HY_N2, body, 0)

    for j in range(HY_SLABS):
        srows = pl.ds(pl.multiple_of((s * HY_SLABS + j) * HY_N2, HY_N2), HY_N2)
        y = jnp.concatenate([_load_halves(br, srows), _load_halves(bi, srows)], axis=0)
        z = jnp.dot(g4_ref[j], _rhs3(y), preferred_element_type=f32)
        zr, zi = z[:HY_N2], z[HY_N2:]
        kr = jnp.concatenate([kr_ref[j * HY_N2:(j + 1) * HY_N2, :]] * 2, axis=1)
        ki = jnp.concatenate([ki_ref[j * HY_N2:(j + 1) * HY_N2, :]] * 2, axis=1)
        w = jnp.concatenate([zr * kr - zi * ki, zr * ki + zi * kr], axis=0)
        t = jnp.dot(g6_ref[j], _rhs3(w), preferred_element_type=f32)
        _store_halves(br, srows, t[:HY_N2])
        _store_halves(bi, srows, t[HY_N2:])

    @pl.when(s == ns - 1)
    def _():
        def body(n2, carry):
            rows = pl.ds(n2, n1, stride=HY_N2)
            rhs = _rhs3(jnp.concatenate([_load_halves(br, rows), _load_halves(bi, rows)], axis=0))
            out = jnp.dot(m8_ref[...], rhs, preferred_element_type=f32)
            orows = pl.ds(n2, half, stride=HY_N2)
            o_ref[0, orows, :] = out[:half, :cw]
            o_ref[2, orows, :] = out[:half, cw:]
            o_ref[1, orows, :] = out[half:, :cw]
            o_ref[3, orows, :] = out[half:, cw:]
            return carry
        lax.fori_loop(0, HY_N2, body, 0)


def _hyena_conv(x, col0, khr, khi, order, tabs):
    seq_len = x.shape[1]
    n = 2 * seq_len
    n1 = n // HY_N2
    cw = 128
    sl = HY_SLABS * HY_N2
    nt = HY_WIDTH // cw
    kspec = pl.BlockSpec((sl, cw), lambda t, s: (s, order * nt + t))
    const = lambda a: pl.BlockSpec(a.shape, lambda t, s: (0,) * a.ndim)
    gspec = pl.BlockSpec((HY_SLABS,) + tabs["g4"].shape[1:], lambda t, s: (s, 0, 0))
    return pl.pallas_call(
        functools.partial(_hyconv_kernel, n1=n1),
        grid=(nt, n1 // HY_SLABS),
        in_specs=[pl.BlockSpec((BATCH, seq_len, cw), lambda t, s: (0, 0, col0 // cw + t)),
                  const(tabs["m2"]), gspec, gspec, const(tabs["m8"]), kspec, kspec],
        out_specs=pl.BlockSpec((BATCH, seq_len, cw), lambda t, s: (0, 0, t)),
        out_shape=jax.ShapeDtypeStruct((BATCH, seq_len, HY_WIDTH), f32),
        scratch_shapes=[pltpu.VMEM((2, n, cw), f32), pltpu.VMEM((2, n, cw), f32)],
        compiler_params=_cp("arbitrary", "arbitrary", vmem=60 * 1024 * 1024),
        name="hyena_long_conv",
    )(x, tabs["m2"], tabs["g4"], tabs["g6"], tabs["m8"], khr, khi)


def _hygate_kernel(c_ref, y_ref, g_ref, b_ref, o_ref):
    y = y_ref[...]
    o_ref[...] = (g_ref[...] * (c_ref[...] + b_ref[...] * y)).astype(o_ref.dtype)


def _hyena_gate(conv, y, ycol0, zc, gcol0, bias, out_dtype):
    tm, tn = 512, 512
    nrows = conv.shape[0]
    return pl.pallas_call(
        _hygate_kernel,
        grid=(nrows // tm, HY_WIDTH // tn),
        in_specs=[pl.BlockSpec((tm, tn), lambda i, j: (i, j)),
                  pl.BlockSpec((tm, tn), lambda i, j: (i, ycol0 // tn + j)),
                  pl.BlockSpec((tm, tn), lambda i, j: (i, gcol0 // tn + j)),
                  pl.BlockSpec((1, tn), lambda i, j: (0, j))],
        out_specs=pl.BlockSpec((tm, tn), lambda i, j: (i, j)),
        out_shape=jax.ShapeDtypeStruct((nrows, HY_WIDTH), out_dtype),
        compiler_params=_cp("arbitrary", "arbitrary"),
        name="hyena_gate",
    )(conv, y, zc, bias.reshape(1, HY_WIDTH))


def _hyena_branch(z, row0, seq_len, tabs, short_w, short_b, w1, b1, w2, b2, w3, freq, bias):
    nrows = BATCH * seq_len
    zc = _short_conv(z, short_w, short_b, row0, nrows, seq_len)
    kern = _hyena_filter(seq_len, w1, b1, w2, b2, w3, freq)
    khr, khi = _hyena_filter_fft(kern, tabs)
    zc3 = zc.reshape(BATCH, seq_len, 3 * HY_WIDTH)
    c1 = _hyena_conv(zc3, 0, khr, khi, 0, tabs).reshape(nrows, HY_WIDTH)
    y1 = _hyena_gate(c1, zc, 0, zc, HY_WIDTH, bias[0], f32)
    c2 = _hyena_conv(y1.reshape(BATCH, seq_len, HY_WIDTH), 0, khr, khi, 1, tabs).reshape(nrows, HY_WIDTH)
    return _hyena_gate(c2, y1, 0, zc, 2 * HY_WIDTH, bias[1], bf16)


def _short_conv3_ref(x, w, b):
    xp = jnp.pad(x, ((0, 0), (1, 1), (0, 0)))
    return w[0] * xp[:, :-2] + w[1] * xp[:, 1:-1] + w[2] * xp[:, 2:] + b


def _hyena_filter_ref(L, w1, b1, w2, b2, w3, freq):
    t = jnp.arange(L, dtype=f32) / L
    bands = jnp.arange(1, HY_BANDS + 1, dtype=f32)
    ang = (2.0 * math.pi) * t[:, None] * bands[None, :]
    feats = jnp.concatenate([t[:, None], jnp.sin(ang), jnp.cos(ang)], axis=-1)
    h = jnp.sin(freq[0] * (feats @ w1 + b1))
    h = jnp.sin(freq[1] * (h @ w2 + b2))
    h = (h @ w3).reshape(L, 2, HY_ORDER, HY_WIDTH)
    deltas = jnp.abs(jnp.linspace(HY_MIN_DECAY, HY_MAX_DECAY, HY_WIDTH, dtype=f32))
    h = h * jnp.exp(-t[:, None] * deltas[None, :])[:, None, None, :]
    fwd, bwd = h[:, 0], h[:, 1]
    return jnp.concatenate([fwd, jnp.zeros_like(fwd[:1]), jnp.flip(bwd[1:], axis=0)], axis=0)


def _fft_long_conv_ref(u, k):
    L = u.shape[1]
    n = 2 * L
    uf = jnp.fft.rfft(u, n=n, axis=1)
    kf = jnp.fft.rfft(k, n=n, axis=0)
    return jnp.fft.irfft(uf * kf[None], n=n, axis=1)[:, :L]


def _hyena_branch_ref(z, short_w, short_b, w1, b1, w2, b2, w3, freq, bias):
    L = z.shape[1]
    z = _short_conv3_ref(z, short_w, short_b)
    v, x1, x2 = jnp.split(z, 3, axis=-1)
    kern = _hyena_filter_ref(L, w1, b1, w2, b2, w3, freq)
    y = v
    for o, gate in enumerate((x1, x2)):
        y = gate * (_fft_long_conv_ref(y, kern[:, o]) + bias[o] * y)
    return y


def _rope_tables():
    half = ATT_HD // 2
    nf = half // 2
    inv = ROPE_BASE ** (-jnp.arange(nf, dtype=f32) / nf)
    pos = jnp.arange(SEQ)
    rows = (pos // GRID_W).astype(f32)[:, None] * inv[None, :]
    cols = (pos % GRID_W).astype(f32)[:, None] * inv[None, :]
    zero = jnp.zeros_like(rows)
    cos = jnp.concatenate([jnp.cos(rows)] * 2 + [jnp.cos(cols)] * 2, axis=-1)
    sin_up = jnp.concatenate([-jnp.sin(rows), zero, -jnp.sin(cols), zero], axis=-1)
    sin_dn = jnp.concatenate([zero, jnp.sin(rows), zero, jnp.sin(cols)], axis=-1)
    return cos, sin_up, sin_dn


def _rope_kernel(q_ref, k_ref, cos_ref, su_ref, sd_ref, qo_ref, ko_ref):
    cos, su, sd = cos_ref[...], su_ref[...], sd_ref[...]

    def rot(x):
        return x * cos + pltpu.roll(x, 96, 1) * su + pltpu.roll(x, 32, 1) * sd

    for h in range(ATT_HEADS):
        s = slice(h * ATT_HD, (h + 1) * ATT_HD)
        qo_ref[:, s] = rot(q_ref[:, s]).astype(qo_ref.dtype)
    for h in range(ATT_KV_HEADS):
        s = slice(h * ATT_HD, (h + 1) * ATT_HD)
        ko_ref[:, s] = rot(k_ref[:, s]).astype(ko_ref.dtype)


def _rope(z, tables):
    tm = 512
    nq = ATT_HEADS * ATT_HD
    nk = ATT_KV_HEADS * ATT_HD
    tab = pl.BlockSpec((tm, ATT_HD), lambda i: (i % (SEQ // tm), 0))
    return pl.pallas_call(
        _rope_kernel,
        grid=(NLAT // tm,),
        in_specs=[pl.BlockSpec((tm, nq), lambda i: (i, Z_Q // nq)),
                  pl.BlockSpec((tm, nk), lambda i: (i, Z_K // nk)),
                  tab, tab, tab],
        out_specs=[pl.BlockSpec((tm, nq), lambda i: (i, 0)),
                   pl.BlockSpec((tm, nk), lambda i: (i, 0))],
        out_shape=[jax.ShapeDtypeStruct((NLAT, nq), bf16), jax.ShapeDtypeStruct((NLAT, nk), bf16)],
        compiler_params=_cp("arbitrary"),
        name="rope",
    )(z, z, *tables)


_ATT_SCALE = ATT_HD ** -0.5
_NEG = float(np.finfo(np.float32).min)
_NT = (((1,), (1,)), ((), ()))


def _sink_column(sink_ref, h, rows):
    rg = lax.broadcasted_iota(jnp.int32, (rows * ATT_GROUP, 1), 0) // rows
    col = jnp.full((rows * ATT_GROUP, 1), sink_ref[h * ATT_GROUP + ATT_GROUP - 1], f32)
    for g in range(ATT_GROUP - 2, -1, -1):
        col = jnp.where(rg == g, sink_ref[h * ATT_GROUP + g], col)
    return col


def _attn_kernel(sink_ref, q_ref, kp_ref, kc_ref, kn_ref, vp_ref, vc_ref, vn_ref, kx_ref, vx_ref, o_ref):
    i = pl.program_id(1)
    nb = pl.num_programs(1)
    blk = ATT_BLOCK
    r = lax.broadcasted_iota(jnp.int32, (ATT_GROUP * blk, 3 * blk), 0) % blk
    c = lax.broadcasted_iota(jnp.int32, (ATT_GROUP * blk, 3 * blk), 1)
    lo = jnp.where(i > 0, 0, blk)
    hi = jnp.where(i < nb - 1, 3 * blk, 2 * blk)
    valid = (c >= r) & (c <= r + 2 * ATT_WINDOW) & (c >= lo) & (c < hi)
    for h in range(ATT_KV_HEADS):
        hs = slice(h * ATT_HD, (h + 1) * ATT_HD)
        k_win = jnp.concatenate([kp_ref[:, hs], kc_ref[:, hs], kn_ref[:, hs]], axis=0)
        v_win = jnp.concatenate([vp_ref[:, hs], vc_ref[:, hs], vn_ref[:, hs]], axis=0).astype(bf16)
        k_ctx = kx_ref[:, hs].astype(bf16)
        v_ctx = vx_ref[:, hs].astype(bf16)
        q = jnp.concatenate([q_ref[:, (h * ATT_GROUP + g) * ATT_HD:(h * ATT_GROUP + g + 1) * ATT_HD]
                             for g in range(ATT_GROUP)], axis=0)
        s_win = lax.dot_general(q, k_win, _NT, preferred_element_type=f32) * _ATT_SCALE
        s_win = jnp.where(valid, s_win, _NEG)
        s_ctx = lax.dot_general(q, k_ctx, _NT, preferred_element_type=f32) * _ATT_SCALE
        sink = _sink_column(sink_ref, h, blk)
        m = jnp.maximum(jnp.maximum(jnp.max(s_win, axis=-1, keepdims=True),
                                    jnp.max(s_ctx, axis=-1, keepdims=True)), sink)
        p_win = jnp.exp(s_win - m)
        p_ctx = jnp.exp(s_ctx - m)
        den = (jnp.sum(p_win, axis=-1, keepdims=True) + jnp.sum(p_ctx, axis=-1, keepdims=True)
               + jnp.exp(sink - m))
        o = (jnp.dot(p_win.astype(bf16), v_win, preferred_element_type=f32)
             + jnp.dot(p_ctx.astype(bf16), v_ctx, preferred_element_type=f32)) / den
        for g in range(ATT_GROUP):
            cs = slice((h * ATT_GROUP + g) * ATT_HD, (h * ATT_GROUP + g + 1) * ATT_HD)
            o_ref[:, cs] = o[g * blk:(g + 1) * blk].astype(o_ref.dtype)


def _window_attention(qr, kr, z, sink):
    blk = ATT_BLOCK
    nb = SEQ // blk
    nkv = ATT_KV_HEADS * ATT_HD
    cx = NLAT // CTX_LEN

    def krow(off):
        return lambda b, i: (b * nb + jnp.clip(i + off, 0, nb - 1), 0)

    def vrow(off):
        return lambda b, i: (b * nb + jnp.clip(i + off, 0, nb - 1), Z_V // nkv)

    return pl.pallas_call(
        _attn_kernel,
        grid=(BATCH, nb),
        in_specs=[pl.BlockSpec(memory_space=pltpu.SMEM),
                  pl.BlockSpec((blk, ATT_HEADS * ATT_HD), lambda b, i: (b * nb + i, 0)),
                  pl.BlockSpec((blk, nkv), krow(-1)),
                  pl.BlockSpec((blk, nkv), krow(0)),
                  pl.BlockSpec((blk, nkv), krow(1)),
                  pl.BlockSpec((blk, nkv), vrow(-1)),
                  pl.BlockSpec((blk, nkv), vrow(0)),
                  pl.BlockSpec((blk, nkv), vrow(1)),
                  pl.BlockSpec((CTX_LEN, nkv), lambda b, i: (cx + b, Z_K // nkv)),
                  pl.BlockSpec((CTX_LEN, nkv), lambda b, i: (cx + b, Z_V // nkv))],
        out_specs=pl.BlockSpec((blk, ATT_HEADS * ATT_HD), lambda b, i: (b * nb + i, 0)),
        out_shape=jax.ShapeDtypeStruct((NLAT, ATT_HEADS * ATT_HD), bf16),
        compiler_params=_cp("arbitrary", "arbitrary"),
        name="window_attention",
    )(sink, qr, kr, kr, kr, z, z, z, z, z)


def _ctx_attn_kernel(sink_ref, q_ref, k_ref, v_ref, o_ref):
    for h in range(ATT_KV_HEADS):
        hs = slice(h * ATT_HD, (h + 1) * ATT_HD)
        k = k_ref[:, hs].astype(bf16)
        v = v_ref[:, hs].astype(bf16)
        q = jnp.concatenate([q_ref[:, (h * ATT_GROUP + g) * ATT_HD:(h * ATT_GROUP + g + 1) * ATT_HD]
                             for g in range(ATT_GROUP)], axis=0).astype(bf16)
        s = lax.dot_general(q, k, _NT, preferred_element_type=f32) * _ATT_SCALE
        sink = _sink_column(sink_ref, h, CTX_LEN)
        m = jnp.maximum(jnp.max(s, axis=-1, keepdims=True), sink)
        p = jnp.exp(s - m)
        den = jnp.sum(p, axis=-1, keepdims=True) + jnp.exp(sink - m)
        o = jnp.dot(p.astype(bf16), v, preferred_element_type=f32) / den
        for g in range(ATT_GROUP):
            cs = slice((h * ATT_GROUP + g) * ATT_HD, (h * ATT_GROUP + g + 1) * ATT_HD)
            o_ref[:, cs] = o[g * CTX_LEN:(g + 1) * CTX_LEN].astype(o_ref.dtype)


def _context_attention(z, sink):
    nq = ATT_HEADS * ATT_HD
    nkv = ATT_KV_HEADS * ATT_HD
    cx = NLAT // CTX_LEN
    return pl.pallas_call(
        _ctx_attn_kernel,
        grid=(BATCH,),
        in_specs=[pl.BlockSpec(memory_space=pltpu.SMEM),
                  pl.BlockSpec((CTX_LEN, nq), lambda b: (cx + b, Z_Q // nq)),
                  pl.BlockSpec((CTX_LEN, nkv), lambda b: (cx + b, Z_K // nkv)),
                  pl.BlockSpec((CTX_LEN, nkv), lambda b: (cx + b, Z_V // nkv))],
        out_specs=pl.BlockSpec((CTX_LEN, nq), lambda b: (b, 0)),
        out_shape=jax.ShapeDtypeStruct((NCTX, nq), bf16),
        compiler_params=_cp("arbitrary"),
        name="context_attention",
    )(sink, z, z, z)


_ML_SCALE = ML_QK ** -0.5


def _split3(x):
    x1 = x.astype(bf16)
    r1 = x - x1.astype(f32)
    x2 = r1.astype(bf16)
    x3 = (r1 - x2.astype(f32)).astype(bf16)
    return x1, x2, x3


def _mlstm_kernel(q_ref, k_ref, v_ref, g_ref, gb_ref, o_ref, ct_ref, n_ref, m_ref):
    d = pl.program_id(1)
    c = pl.program_id(2)
    ch = ML_CHUNK

    @pl.when(c == 0)
    def _():
        ct_ref[...] = jnp.zeros_like(ct_ref)
        n_ref[...] = jnp.zeros_like(n_ref)
        m_ref[...] = jnp.zeros_like(m_ref)

    g = g_ref[...] + gb_ref[...]
    row = lax.broadcasted_iota(jnp.int32, (ch, ch), 0)
    col = lax.broadcasted_iota(jnp.int32, (ch, ch), 1)
    tri = (row - col) * (1 - 2 * d) >= 0
    tri_b = tri.astype(f32).astype(bf16)
    lf = jax.nn.log_sigmoid(g)
    l1, l2, l3 = _split3(lf)
    bcol = (jnp.dot(tri_b, l1, preferred_element_type=f32)
            + jnp.dot(tri_b, l2, preferred_element_type=f32)
            + jnp.dot(tri_b, l3, preferred_element_type=f32))
    bend = jnp.where(d == 0, bcol[ch - 1:ch, :], bcol[0:1, :])
    g_t = g.T
    b_t = bcol.T

    for h in range(ML_HEADS):
        qf = q_ref[:, h * ML_QK:(h + 1) * ML_QK] * _ML_SCALE
        kf = k_ref[:, h * ML_QK:(h + 1) * ML_QK]
        vf = v_ref[:, h * ML_V:(h + 1) * ML_V]
        q = qf.astype(bf16)
        k = kf.astype(bf16)
        li_c = g[:, h:h + 1]
        b_c = bcol[:, ML_HEADS + h:ML_HEADS + h + 1]
        li_r = g_t[h:h + 1, :]
        b_r = b_t[ML_HEADS + h:ML_HEADS + h + 1, :]
        m_prev = m_ref[h][:, 0:1]
        dmat = jnp.where(tri, b_c - b_r + li_r, -jnp.inf)
        inter = b_c + m_prev
        m_t = jnp.maximum(inter, jnp.max(dmat, axis=-1, keepdims=True))
        w_intra = jnp.exp(dmat - m_t)
        w_inter = jnp.exp(inter - m_t)
        s = lax.dot_general(q, k, _NT, preferred_element_type=f32) * w_intra
        qc = jnp.dot(q, ct_ref[h].astype(bf16), preferred_element_type=f32)
        num = jnp.dot(s.astype(bf16), vf.astype(bf16), preferred_element_type=f32) + w_inter * qc
        den = (jnp.sum(s, axis=-1, keepdims=True)
               + w_inter * jnp.sum(qf * n_ref[h], axis=-1, keepdims=True))
        o_ref[:, h * ML_V:(h + 1) * ML_V] = num / jnp.maximum(jnp.abs(den), jnp.exp(-m_t))

        b_e = bend[:, ML_HEADS + h:ML_HEADS + h + 1]
        g_c = b_e - b_c + li_c
        m_new = jnp.maximum(b_e + m_prev, jnp.max(g_c, axis=0, keepdims=True))
        w_s = jnp.exp(g_c - m_new)
        w_c = jnp.exp(b_e + m_prev - m_new)
        vw = (vf * w_s).astype(bf16)
        ct_ref[h] = w_c * ct_ref[h] + jnp.dot(kf.T.astype(bf16), vw, preferred_element_type=f32)
        n_ref[h] = w_c * n_ref[h] + jnp.sum(kf * w_s, axis=0, keepdims=True)
        m_ref[h] = jnp.broadcast_to(m_new, (1, 128))


def _mlstm(z, gate_b):
    ch = ML_CHUNK
    ncl = SEQ // ch
    nsteps = ncl + CTX_LEN // ch
    assert CTX_LEN == ch

    def rt(b, d, c):
        lat = b * ncl + jnp.where(d == 0, c - 1, ncl - c)
        return jnp.where(c == 0, NLAT // ch + b, lat)

    nq = ML_HEADS * ML_QK
    nv = ML_HEADS * ML_V
    return pl.pallas_call(
        _mlstm_kernel,
        grid=(BATCH, 2, nsteps),
        in_specs=[pl.BlockSpec((ch, nq), lambda b, d, c: (rt(b, d, c), Z_MQ // nq)),
                  pl.BlockSpec((ch, nq), lambda b, d, c: (rt(b, d, c), Z_MK // nq)),
                  pl.BlockSpec((ch, nv), lambda b, d, c: (rt(b, d, c), Z_MV // nv)),
                  pl.BlockSpec((ch, 128), lambda b, d, c: (rt(b, d, c), Z_GATE // 128 + d)),
                  pl.BlockSpec((None, 1, 128), lambda b, d, c: (d, 0, 0))],
        out_specs=pl.BlockSpec((None, ch, nv), lambda b, d, c: (d, rt(b, d, c), 0)),
        out_shape=jax.ShapeDtypeStruct((2, R, nv), f32),
        scratch_shapes=[pltpu.VMEM((ML_HEADS, ML_QK, ML_V), f32),
                        pltpu.VMEM((ML_HEADS, 1, ML_QK), f32),
                        pltpu.VMEM((ML_HEADS, 1, 128), f32)],
        compiler_params=_cp("arbitrary", "arbitrary", "arbitrary"),
        name="mlstm",
    )(z, z, z, z, gate_b)


def _mlstm_out_kernel(h_ref, zo_ref, g_ref, o_ref):
    for h in range(ML_HEADS):
        s = slice(h * ML_V, (h + 1) * ML_V)
        x = h_ref[0, :, s] + h_ref[1, :, s]
        xn = x * lax.rsqrt(jnp.mean(x * x, axis=-1, keepdims=True) + EPS) * g_ref[:, s]
        o_ref[:, s] = (xn * jax.nn.sigmoid(zo_ref[:, s])).astype(o_ref.dtype)


def _mlstm_out(hh, z, g, nrows):
    tm = 512
    nv = ML_HEADS * ML_V
    return pl.pallas_call(
        _mlstm_out_kernel,
        grid=(nrows // tm,),
        in_specs=[pl.BlockSpec((2, tm, nv), lambda i: (0, i, 0)),
                  pl.BlockSpec((tm, nv), lambda i: (i, Z_MO // nv)),
                  pl.BlockSpec((1, nv), lambda i: (0, 0))],
        out_specs=pl.BlockSpec((tm, nv), lambda i: (i, 0)),
        out_shape=jax.ShapeDtypeStruct((nrows, nv), bf16),
        compiler_params=_cp("arbitrary"),
        name="mlstm_out",
    )(hh, z, g.reshape(1, nv))


def _router_kernel(f_ref, w_ref, idx_ref, p_ref):
    logits = jnp.dot(f_ref[...].astype(bf16), w_ref[...], preferred_element_type=f32)
    lane = lax.broadcasted_iota(jnp.int32, logits.shape, 1).astype(f32)
    logits = jnp.where(lane < N_EXPERTS, logits, -jnp.inf)
    v1 = jnp.max(logits, axis=-1, keepdims=True)
    i1 = jnp.min(jnp.where(logits == v1, lane, 128.0), axis=-1, keepdims=True)
    rest = jnp.where(lane == i1, -jnp.inf, logits)
    v2 = jnp.max(rest, axis=-1, keepdims=True)
    i2 = jnp.min(jnp.where(rest == v2, lane, 128.0), axis=-1, keepdims=True)
    e = jnp.exp(v2 - v1)
    p1 = 1.0 / (1.0 + e)
    p2 = e / (1.0 + e)
    idx_ref[...] = jnp.where(lane == 0, i1, jnp.where(lane == 1, i2, 0.0)).astype(jnp.int32)
    p_ref[...] = jnp.where(lane == 0, p1, jnp.where(lane == 1, p2, 0.0))


def _router(f, w_router):
    tm = 512
    w = jnp.pad(w_router, ((0, 0), (0, 128 - N_EXPERTS))).astype(bf16)
    return pl.pallas_call(
        _router_kernel,
        grid=(NLAT // tm,),
        in_specs=[pl.BlockSpec((tm, D_MODEL), lambda i: (i, 0)),
                  pl.BlockSpec((D_MODEL, 128), lambda i: (0, 0))],
        out_specs=[pl.BlockSpec((tm, 128), lambda i: (i, 0)),
                   pl.BlockSpec((tm, 128), lambda i: (i, 0))],
        out_shape=[jax.ShapeDtypeStruct((NLAT, 128), jnp.int32),
                   jax.ShapeDtypeStruct((NLAT, 128), f32)],
        compiler_params=_cp("arbitrary"),
        name="router",
    )(f, w)


def _gather_kernel(tok_ref, src_ref, o_ref, sem):
    base = pl.program_id(0) * MOE_TILE

    def copy(r):
        return pltpu.make_async_copy(src_ref.at[pl.ds(tok_ref[base + r], 1)], o_ref.at[pl.ds(r, 1)], sem)

    def start(r, carry):
        copy(r).start()
        return carry

    def wait(r, carry):
        copy(r).wait()
        return carry

    lax.fori_loop(0, MOE_TILE, start, 0)
    lax.fori_loop(0, MOE_TILE, wait, 0)


def _gather_rows(src, rows, n_out):
    width = src.shape[1]
    return pl.pallas_call(
        _gather_kernel,
        grid_spec=pltpu.PrefetchScalarGridSpec(
            num_scalar_prefetch=1,
            grid=(n_out // MOE_TILE,),
            in_specs=[pl.BlockSpec(memory_space=pl.ANY)],
            out_specs=pl.BlockSpec((MOE_TILE, width), lambda i, tok: (i, 0)),
            scratch_shapes=[pltpu.SemaphoreType.DMA(())]),
        out_shape=jax.ShapeDtypeStruct((n_out, width), src.dtype),
        compiler_params=_cp("arbitrary"),
        name="gather_rows",
    )(rows, src)


def _moe_up_kernel(be_ref, nu_ref, a_ref, wg_ref, wu_ref, o_ref):
    i = pl.program_id(1)

    @pl.when(i < nu_ref[0])
    def _():
        a = a_ref[...].astype(bf16)
        g = jnp.dot(a, wg_ref[...], preferred_element_type=f32)
        u = jnp.dot(a, wu_ref[...], preferred_element_type=f32)
        o_ref[...] = (g * jax.nn.sigmoid(g) * u).astype(o_ref.dtype)

    @pl.when(i >= nu_ref[0])
    def _():
        o_ref[...] = jnp.zeros_like(o_ref)


def _moe_up(xs, wg, wu, blk_expert, n_used):
    tn = 512
    nblk = xs.shape[0] // MOE_TILE

    def row(j, i, be, nu):
        return (jnp.minimum(i, nu[0] - 1), 0)

    def wmap(j, i, be, nu):
        return (be[jnp.minimum(i, nu[0] - 1)], 0, j)

    return pl.pallas_call(
        _moe_up_kernel,
        grid_spec=pltpu.PrefetchScalarGridSpec(
            num_scalar_prefetch=2,
            grid=(FFN_EXPERT // tn, nblk),
            in_specs=[pl.BlockSpec((MOE_TILE, D_MODEL), row),
                      pl.BlockSpec((None, D_MODEL, tn), wmap),
                      pl.BlockSpec((None, D_MODEL, tn), wmap)],
            out_specs=pl.BlockSpec((MOE_TILE, tn), lambda j, i, be, nu: (i, j))),
        out_shape=jax.ShapeDtypeStruct((xs.shape[0], FFN_EXPERT), bf16),
        compiler_params=_cp("arbitrary", "arbitrary"),
        name="moe_up",
    )(blk_expert, n_used, xs, wg, wu)


def _moe_down_kernel(be_ref, nu_ref, a_ref, w_ref, o_ref):
    i = pl.program_id(1)

    @pl.when(i < nu_ref[0])
    def _():
        o_ref[...] = jnp.dot(a_ref[...], w_ref[...], preferred_element_type=f32)

    @pl.when(i >= nu_ref[0])
    def _():
        o_ref[...] = jnp.zeros_like(o_ref)


def _moe_down(hs, wd, blk_expert, n_used):
    tn = 512
    nblk = hs.shape[0] // MOE_TILE

    def row(j, i, be, nu):
        return (jnp.minimum(i, nu[0] - 1), 0)

    def wmap(j, i, be, nu):
        return (be[jnp.minimum(i, nu[0] - 1)], 0, j)

    return pl.pallas_call(
        _moe_down_kernel,
        grid_spec=pltpu.PrefetchScalarGridSpec(
            num_scalar_prefetch=2,
            grid=(D_MODEL // tn, nblk),
            in_specs=[pl.BlockSpec((MOE_TILE, FFN_EXPERT), row),
                      pl.BlockSpec((None, FFN_EXPERT, tn), wmap)],
            out_specs=pl.BlockSpec((MOE_TILE, tn), lambda j, i, be, nu: (i, j))),
        out_shape=jax.ShapeDtypeStruct((hs.shape[0], D_MODEL), f32),
        compiler_params=_cp("arbitrary", "arbitrary"),
        name="moe_down",
    )(blk_expert, n_used, hs, wd)


_COMBINE_TM = 256


def _combine_kernel(slot_ref, yb_ref, x_ref, p_ref, g2_ref, fg_ref, o_ref, buf, sem):
    base = pl.program_id(0) * _COMBINE_TM

    def copy(r, k):
        s = slot_ref[(base + r) * TOP_K + k]
        return pltpu.make_async_copy(yb_ref.at[pl.ds(s, 1)], buf.at[k, pl.ds(r, 1)], sem)

    def start(r, carry):
        copy(r, 0).start()
        copy(r, 1).start()
        return carry

    def wait(r, carry):
        copy(r, 0).wait()
        copy(r, 1).wait()
        return carry

    lax.fori_loop(0, _COMBINE_TM, start, 0)
    lax.fori_loop(0, _COMBINE_TM, wait, 0)
    p = p_ref[...]
    y = buf[0] * p[:, 0:1] + buf[1] * p[:, 1:2]
    x = x_ref[...] + g2_ref[...] * y
    o_ref[...] = x * lax.rsqrt(jnp.mean(x * x, axis=-1, keepdims=True) + EPS) * fg_ref[...]


def _combine_final(slot, yb, xa, probs, modt, final_g):
    tm = _COMBINE_TM
    return pl.pallas_call(
        _combine_kernel,
        grid_spec=pltpu.PrefetchScalarGridSpec(
            num_scalar_prefetch=1,
            grid=(NLAT // tm,),
            in_specs=[pl.BlockSpec(memory_space=pl.ANY),
                      pl.BlockSpec((tm, D_MODEL), lambda i, s: (i, 0)),
                      pl.BlockSpec((tm, 128), lambda i, s: (i, 0)),
                      pl.BlockSpec((None, 1, D_MODEL), lambda i, s: (i // (SEQ // tm), 0, 5)),
                      pl.BlockSpec((1, D_MODEL), lambda i, s: (0, 0))],
            out_specs=pl.BlockSpec((tm, D_MODEL), lambda i, s: (i, 0)),
            scratch_shapes=[pltpu.VMEM((TOP_K, tm, D_MODEL), f32),
                            pltpu.SemaphoreType.DMA(())]),
        out_shape=jax.ShapeDtypeStruct((NLAT, D_MODEL), f32),
        compiler_params=_cp("arbitrary"),
        name="moe_combine_final_norm",
    )(slot, yb, xa, probs, modt, final_g.reshape(1, D_MODEL))


def _moe_routing(top_i):
    a = NLAT * TOP_K
    e_flat = top_i.reshape(a)
    onehot = (e_flat[:, None] == jnp.arange(N_EXPERTS)[None, :]).astype(jnp.int32)
    csum = jnp.cumsum(onehot, axis=0)
    rank = jnp.sum(onehot * csum, axis=1) - 1
    counts = csum[-1]
    padded = (counts + MOE_TILE - 1) // MOE_TILE * MOE_TILE
    pad_end = jnp.cumsum(padded)
    pad_start = pad_end - padded
    slot = (pad_start[e_flat] + rank).astype(jnp.int32)
    n_rows = a + N_EXPERTS * MOE_TILE
    nblk = n_rows // MOE_TILE
    slot_tok = jnp.zeros((n_rows,), jnp.int32).at[slot].set(jnp.arange(a, dtype=jnp.int32) // TOP_K)
    blk_expert = jnp.minimum(jnp.searchsorted(pad_end, jnp.arange(nblk) * MOE_TILE, side='right'),
                             N_EXPERTS - 1).astype(jnp.int32)
    n_used = (pad_end[-1:] // MOE_TILE).astype(jnp.int32)
    return slot, slot_tok, blk_expert, n_used, n_rows


def _in_proj_weight(w, gate_b):
    o = np.cumsum((0,) + (3072, 1024, 256, 256, 512, 512, 1024, 1024, 16, 6144))
    hy, q, k, v, mq, mk, mv, mo, gt, mg = [w[:, o[i]:o[i + 1]] for i in range(10)]
    pad = jnp.zeros((D_MODEL, 128 - 2 * ML_HEADS), w.dtype)
    gates = [jnp.concatenate([gt[:, 8 * d:8 * d + 8], pad], axis=1) for d in range(2)]
    wz = jnp.concatenate([hy, q, mv, mo, mg, k, v, mq, mk] + gates
                         + [jnp.zeros((D_MODEL, Z_COLS - Z_GATE - 256), w.dtype)], axis=1).astype(bf16)
    gb = jnp.pad(gate_b.reshape(2, 1, 2 * ML_HEADS), ((0, 0), (0, 0), (0, 128 - 2 * ML_HEADS)))
    return wz, gb


def kernel(x, c, ctx, c_ctx, w_mod, b_mod, norm_mix_g, norm_ffn_g, w_in, hy_short_w, hy_short_b, hy_w1, hy_b1, hy_w2, hy_b2, hy_w3, hy_freq, hy_bias, att_sink, ml_gate_b, ml_norm_g, w_branch, w_out, ffn_wg, ffn_wu, ffn_wd, moe_router, moe_wg, moe_wu, moe_wd, final_g):
    xa = jnp.concatenate([x.reshape(NLAT, D_MODEL), ctx.reshape(NCTX, D_MODEL)], axis=0)
    c_all = jnp.concatenate([c, c_ctx[None], jnp.zeros((8 - BATCH - 1, D_MODEL), f32)], axis=0)
    mod = _modulation(c_all, w_mod, b_mod)
    rope_tabs = _rope_tables()
    dft_tabs = _dft_tables(SEQ)
    out = None
    for layer in range(DEPTH):
        last = layer == DEPTH - 1
        modt = mod[layer].reshape(8, 1, 6 * D_MODEL)
        n_mix = NLAT if last else R

        u = _normmod(xa, norm_mix_g[layer], modt, 0, R)
        wz, gate_b = _in_proj_weight(w_in[layer], ml_gate_b[layer])
        z = _in_proj(u, wz)

        hy = (hy_short_w[layer], hy_short_b[layer], hy_w1[layer], hy_b1[layer], hy_w2[layer],
              hy_b2[layer], hy_w3[layer], hy_freq[layer], hy_bias[layer])
        a_rows = _hyena_branch(z, 0, SEQ, dft_tabs, *hy)

        qr, kr = _rope(z, rope_tabs)
        b_rows = _window_attention(qr, kr, z, att_sink[layer])

        hh = _mlstm(z, gate_b)
        c_rows = _mlstm_out(hh, z, ml_norm_g[layer], n_mix)

        if not last:
            a_ctx = _hyena_branch_ref(z[NLAT:, :3 * HY_WIDTH].reshape(BATCH, CTX_LEN, 3 * HY_WIDTH), *hy)
            a_rows = jnp.concatenate([a_rows, a_ctx.reshape(NCTX, HY_WIDTH).astype(bf16)], axis=0)
            b_rows = jnp.concatenate([b_rows, _context_attention(z, att_sink[layer])], axis=0)

        ymid = _merge(a_rows, b_rows, c_rows, w_branch[layer].astype(bf16), z, n_mix)
        xa = _mm_resid(ymid, w_out[layer].astype(bf16), xa, modt, 2, n_mix)

        f = _normmod(xa, norm_ffn_g[layer], modt, 3, n_mix, bf16 if layer % 2 == 0 else f32)
        if layer % 2 == 0:
            e = layer // 2
            hmid = _ffn_up(f, ffn_wg[e].astype(bf16), ffn_wu[e].astype(bf16))
            xa = _mm_resid(hmid, ffn_wd[e].astype(bf16), xa, modt, 5, R)
        else:
            e = layer // 2
            top_i, probs = _router(f, moe_router[e])
            slot, slot_tok, blk_expert, n_used, n_rows = _moe_routing(top_i[:, :TOP_K])
            xs = _gather_rows(f, slot_tok, n_rows)
            hs = _moe_up(xs, moe_wg[e].astype(bf16), moe_wu[e].astype(bf16), blk_expert, n_used)
            yb = _moe_down(hs, moe_wd[e].astype(bf16), blk_expert, n_used)
            assert last
            out = _combine_final(slot, yb, xa, probs, modt, final_g)
    return out.reshape(BATCH, SEQ, D_MODEL)
```

```python
import functools
import math

import numpy as np
import jax
import jax.numpy as jnp
from jax import lax
from jax.experimental import pallas as pl
from jax.experimental.pallas import tpu as pltpu

f32 = jnp.float32
bf16 = jnp.bfloat16

D_MODEL = 2048
BATCH = 4
SEQ = 4096
DEPTH = 2
GRID_W = 64
CTX_LEN = 256
EPS = 1e-6

HY_WIDTH = D_MODEL // 2
HY_ORDER = 2
HY_BANDS = 16
HY_EMB = 2 * HY_BANDS + 1
HY_FFN = 64
HY_MIN_DECAY = -3.0701134573253943
HY_MAX_DECAY = -15.35056728662697
HY_N2 = 128
HY_SLABS = 4
HY_UNROLL = 8

ATT_HD = 128
ATT_HEADS = 8
ATT_KV_HEADS = 2
ATT_GROUP = 4
ATT_WINDOW = 128
ATT_BLOCK = 128
ROPE_BASE = 10000.0

ML_HEADS = 4
ML_V = 256
ML_QK = 128
ML_CHUNK = 256

N_BRANCH = 3
BRANCH_W = D_MODEL // 2
FFN_DENSE = 5632
N_EXPERTS = 8
TOP_K = 2
FFN_EXPERT = 7168
MOE_TILE = 512

NCTX = BATCH * CTX_LEN
NLAT = BATCH * SEQ
R = NCTX + NLAT

Z_HY = 0
Z_Q = 3072
Z_MV = 4096
Z_MO = 5120
Z_MERGE = 6144
Z_K = 12288
Z_V = 12544
Z_MQ = 12800
Z_MK = 13312
Z_GATE = 13824
Z_COLS = 14336

VMEM_LIMIT = 56 * 1024 * 1024


def _cp(*sem, vmem=VMEM_LIMIT):
    return pltpu.CompilerParams(dimension_semantics=sem, vmem_limit_bytes=vmem)


def _modrow(i, tm):
    return jnp.where(i >= NLAT // tm, BATCH, i // (SEQ // tm))


def _mod_kernel(c_ref, w_ref, b_ref, o_ref):
    c = c_ref[...]
    a = (c * jax.nn.sigmoid(c)).astype(bf16)
    o_ref[...] = jnp.dot(a, w_ref[...].astype(bf16), preferred_element_type=f32) + b_ref[...]


def _modulation(c_all, w_mod, b_mod):
    tn = 1024
    n = 6 * D_MODEL
    return pl.pallas_call(
        _mod_kernel,
        grid=(DEPTH, n // tn),
        in_specs=[pl.BlockSpec((8, D_MODEL), lambda l, j: (0, 0)),
                  pl.BlockSpec((None, D_MODEL, tn), lambda l, j: (l, 0, j)),
                  pl.BlockSpec((None, 1, tn), lambda l, j: (l, 0, j))],
        out_specs=pl.BlockSpec((None, 8, tn), lambda l, j: (l, 0, j)),
        out_shape=jax.ShapeDtypeStruct((DEPTH, 8, n), f32),
        compiler_params=_cp("arbitrary", "arbitrary"),
        name="modulation",
    )(c_all, w_mod, b_mod.reshape(DEPTH, 1, n))


def _normmod_kernel(x_ref, g_ref, sh_ref, sc_ref, o_ref):
    x = x_ref[...]
    y = x * lax.rsqrt(jnp.mean(x * x, axis=-1, keepdims=True) + EPS) * g_ref[...]
    o_ref[...] = (y * (1.0 + sc_ref[...]) + sh_ref[...]).astype(o_ref.dtype)


def _normmod(xa, g, modt, which, nrows, out_dtype=bf16):
    tm = 512
    return pl.pallas_call(
        _normmod_kernel,
        grid=(nrows // tm,),
        in_specs=[pl.BlockSpec((tm, D_MODEL), lambda i: (i, 0)),
                  pl.BlockSpec((1, D_MODEL), lambda i: (0, 0)),
                  pl.BlockSpec((None, 1, D_MODEL), lambda i: (_modrow(i, tm), 0, which)),
                  pl.BlockSpec((None, 1, D_MODEL), lambda i: (_modrow(i, tm), 0, which + 1))],
        out_specs=pl.BlockSpec((tm, D_MODEL), lambda i: (i, 0)),
        out_shape=jax.ShapeDtypeStruct((nrows, D_MODEL), out_dtype),
        compiler_params=_cp("arbitrary"),
        name="normmod",
    )(xa, g.reshape(1, D_MODEL), modt, modt)


def _mm_kernel(a_ref, w_ref, o_ref):
    o_ref[...] = jnp.dot(a_ref[...], w_ref[...], preferred_element_type=f32).astype(o_ref.dtype)


def _in_proj(u, w):
    tm, tn = 1024, 1024
    return pl.pallas_call(
        _mm_kernel,
        grid=(Z_COLS // tn, R // tm),
        in_specs=[pl.BlockSpec((tm, D_MODEL), lambda j, i: (i, 0)),
                  pl.BlockSpec((D_MODEL, tn), lambda j, i: (0, j))],
        out_specs=pl.BlockSpec((tm, tn), lambda j, i: (i, j)),
        out_shape=jax.ShapeDtypeStruct((R, Z_COLS), f32),
        compiler_params=_cp("arbitrary", "arbitrary"),
        name="in_proj",
    )(u, w)


def _mm_resid_kernel(a_ref, w_ref, x_ref, g_ref, o_ref):
    y = jnp.dot(a_ref[...], w_ref[...], preferred_element_type=f32)
    o_ref[...] = x_ref[...] + g_ref[...] * y


def _mm_resid(a, w, xa, modt, which, nrows):
    k = a.shape[1]
    tm, tn = (1024, 1024) if k <= D_MODEL else (512, 1024)
    nj = D_MODEL // tn
    return pl.pallas_call(
        _mm_resid_kernel,
        grid=(nj, nrows // tm),
        in_specs=[pl.BlockSpec((tm, k), lambda j, i: (i, 0)),
                  pl.BlockSpec((k, tn), lambda j, i: (0, j)),
                  pl.BlockSpec((tm, tn), lambda j, i: (i, j)),
                  pl.BlockSpec((None, 1, tn), lambda j, i: (_modrow(i, tm), 0, which * nj + j))],
        out_specs=pl.BlockSpec((tm, tn), lambda j, i: (i, j)),
        out_shape=jax.ShapeDtypeStruct((nrows, D_MODEL), f32),
        compiler_params=_cp("arbitrary", "arbitrary"),
        name="mm_resid",
    )(a, w, xa, modt)


def _swiglu_kernel(a_ref, wg_ref, wu_ref, o_ref):
    a = a_ref[...]
    g = jnp.dot(a, wg_ref[...], preferred_element_type=f32)
    u = jnp.dot(a, wu_ref[...], preferred_element_type=f32)
    o_ref[...] = (g * jax.nn.sigmoid(g) * u).astype(o_ref.dtype)


def _ffn_up(f, wg, wu):
    tm, tn = 1024, 512
    n = wg.shape[1]
    return pl.pallas_call(
        _swiglu_kernel,
        grid=(n // tn, R // tm),
        in_specs=[pl.BlockSpec((tm, D_MODEL), lambda j, i: (i, 0)),
                  pl.BlockSpec((D_MODEL, tn), lambda j, i: (0, j)),
                  pl.BlockSpec((D_MODEL, tn), lambda j, i: (0, j))],
        out_specs=pl.BlockSpec((tm, tn), lambda j, i: (i, j)),
        out_shape=jax.ShapeDtypeStruct((R, n), bf16),
        compiler_params=_cp("arbitrary", "arbitrary"),
        name="ffn_up",
    )(f, wg, wu)


def _merge_kernel(a_ref, b_ref, c_ref, w_ref, g0_ref, g1_ref, g2_ref, o_ref):
    y = jax.nn.sigmoid(g0_ref[...]) * jnp.dot(a_ref[...], w_ref[0], preferred_element_type=f32)
    y += jax.nn.sigmoid(g1_ref[...]) * jnp.dot(b_ref[...], w_ref[1], preferred_element_type=f32)
    y += jax.nn.sigmoid(g2_ref[...]) * jnp.dot(c_ref[...], w_ref[2], preferred_element_type=f32)
    o_ref[...] = y.astype(o_ref.dtype)


def _merge(a, b, c, wb, z, nrows):
    tm, tn = 512, 512
    nj = D_MODEL // tn
    act = pl.BlockSpec((tm, BRANCH_W), lambda j, i: (i, 0))

    def gate(br):
        return pl.BlockSpec((tm, tn), lambda j, i: (i, (Z_MERGE + br * D_MODEL) // tn + j))

    return pl.pallas_call(
        _merge_kernel,
        grid=(nj, nrows // tm),
        in_specs=[act, act, act,
                  pl.BlockSpec((N_BRANCH, BRANCH_W, tn), lambda j, i: (0, 0, j)),
                  gate(0), gate(1), gate(2)],
        out_specs=pl.BlockSpec((tm, tn), lambda j, i: (i, j)),
        out_shape=jax.ShapeDtypeStruct((nrows, D_MODEL), bf16),
        compiler_params=_cp("arbitrary", "arbitrary"),
        name="merge",
    )(a, b, c, wb, z, z, z)


def _short_conv_kernel(x_ref, xp_ref, xn_ref, w_ref, b_ref, o_ref, *, tiles_per_seq):
    i = pl.program_id(0)
    tm = x_ref.shape[0]
    x = x_ref[...]
    first = i % tiles_per_seq == 0
    last = i % tiles_per_seq == tiles_per_seq - 1
    prev = jnp.where(first, 0.0, xp_ref[7:8, :])
    nxt = jnp.where(last, 0.0, xn_ref[0:1, :])
    row = lax.broadcasted_iota(jnp.int32, x.shape, 0)
    up = jnp.where(row == 0, prev, pltpu.roll(x, 1, 0))
    dn = jnp.where(row == tm - 1, nxt, pltpu.roll(x, tm - 1, 0))
    o_ref[...] = w_ref[0:1, :] * up + w_ref[1:2, :] * x + w_ref[2:3, :] * dn + b_ref[...]


def _short_conv(z, w, b, row0, nrows, seq_len):
    tm, tn = 256, 1536
    t0 = row0 // tm
    nc = 3 * HY_WIDTH
    last8 = R // 8 - 1
    return pl.pallas_call(
        functools.partial(_short_conv_kernel, tiles_per_seq=seq_len // tm),
        grid=(nrows // tm, nc // tn),
        in_specs=[pl.BlockSpec((tm, tn), lambda i, j: (i + t0, j)),
                  pl.BlockSpec((8, tn), lambda i, j: (jnp.maximum((i + t0) * (tm // 8) - 1, 0), j)),
                  pl.BlockSpec((8, tn), lambda i, j: (jnp.minimum((i + t0 + 1) * (tm // 8), last8), j)),
                  pl.BlockSpec((3, tn), lambda i, j: (0, j)),
                  pl.BlockSpec((1, tn), lambda i, j: (0, j))],
        out_specs=pl.BlockSpec((tm, tn), lambda i, j: (i, j)),
        out_shape=jax.ShapeDtypeStruct((nrows, nc), f32),
        compiler_params=_cp("arbitrary", "arbitrary"),
        name="hyena_short_conv",
    )(z, z, z, w, b.reshape(1, nc))


def _hyfilt_kernel(w1_ref, b1_ref, w2_ref, b2_ref, w3_ref, fr_ref, dl_ref, o_ref, *, seq_len):
    tm = o_ref.shape[0]
    r = pl.program_id(0) * tm + lax.broadcasted_iota(jnp.int32, (tm, 1), 0)
    p = jnp.where(r < seq_len, r, 2 * seq_len - r)
    t = p.astype(f32) / seq_len
    lane = lax.broadcasted_iota(jnp.int32, (tm, 128), 1)
    band = jnp.where(lane <= HY_BANDS, lane, lane - HY_BANDS).astype(f32)
    ang = ((2.0 * math.pi) * t) * band
    feats = jnp.where(lane == 0, t,
                      jnp.where(lane <= HY_BANDS, jnp.sin(ang),
                                jnp.where(lane <= 2 * HY_BANDS, jnp.cos(ang), 0.0)))
    h = jnp.dot(feats.astype(bf16), w1_ref[...].astype(bf16), preferred_element_type=f32) + b1_ref[...]
    h = jnp.sin(fr_ref[0:1, :] * h)
    h = jnp.dot(h.astype(bf16), w2_ref[...].astype(bf16), preferred_element_type=f32) + b2_ref[...]
    h = jnp.sin(fr_ref[1:2, :] * h)
    k = jnp.dot(h.astype(bf16), w3_ref[...].astype(bf16), preferred_element_type=f32)
    k = k * jnp.exp(-t * dl_ref[...])
    o_ref[...] = jnp.where(r == seq_len, 0.0, k)


def _hyena_filter(seq_len, w1, b1, w2, b2, w3, freq):
    nc = HY_ORDER * HY_WIDTH
    tm, tn = min(512, seq_len), nc
    deltas = jnp.abs(jnp.linspace(HY_MIN_DECAY, HY_MAX_DECAY, HY_WIDTH, dtype=f32))
    dl = jnp.tile(deltas, HY_ORDER).reshape(1, nc)
    w1p = jnp.pad(w1, ((0, 128 - HY_EMB), (0, 0)))
    per_dir = nc // tn
    full = lambda i, j: (0, 0)
    return pl.pallas_call(
        functools.partial(_hyfilt_kernel, seq_len=seq_len),
        grid=(2 * seq_len // tm, per_dir),
        in_specs=[pl.BlockSpec((128, HY_FFN), full),
                  pl.BlockSpec((1, HY_FFN), full),
                  pl.BlockSpec((HY_FFN, HY_FFN), full),
                  pl.BlockSpec((1, HY_FFN), full),
                  pl.BlockSpec((HY_FFN, tn), lambda i, j: (0, jnp.where(i >= seq_len // tm, per_dir, 0) + j)),
                  pl.BlockSpec((2, HY_FFN), full),
                  pl.BlockSpec((1, tn), lambda i, j: (0, j))],
        out_specs=pl.BlockSpec((tm, tn), lambda i, j: (i, j)),
        out_shape=jax.ShapeDtypeStruct((2 * seq_len, nc), f32),
        compiler_params=_cp("arbitrary", "arbitrary"),
        name="hyena_filter",
    )(w1p, b1.reshape(1, HY_FFN), w2, b2.reshape(1, HY_FFN), w3, freq, dl)


def _pass3(m):
    hi = m.astype(bf16)
    lo = (m - hi.astype(f32)).astype(bf16)
    return jnp.concatenate([hi, hi, lo], axis=-1)


def _rhs3(x):
    hi = x.astype(bf16)
    lo = (x - hi.astype(f32)).astype(bf16)
    return jnp.concatenate([hi, lo, hi], axis=0)


def _stack_complex(ar, ai):
    return jnp.concatenate([jnp.concatenate([ar, -ai], axis=-1),
                            jnp.concatenate([ai, ar], axis=-1)], axis=-2)


def _dft_tables(seq_len):
    n = 2 * seq_len
    n1 = n // HY_N2
    half = n1 // 2
    i1 = jnp.arange(n1, dtype=jnp.int32)
    ang1 = (2.0 * math.pi / n1) * ((i1[:, None] * i1[None, :]) % n1).astype(f32)
    c1, s1 = jnp.cos(ang1), jnp.sin(ang1)
    m2 = _pass3(_stack_complex(c1[:, :half], -s1[:, :half]))
    m2f = _pass3(jnp.concatenate([c1, -s1], axis=0))
    m8 = _pass3(_stack_complex(c1[:half, :] / n, s1[:half, :] / n))
    i2 = jnp.arange(HY_N2, dtype=jnp.int32)
    k = i1[:, None, None] + n1 * i2[None, :, None]
    ang = (2.0 * math.pi / n) * ((i2[None, None, :] * k) % n).astype(f32)
    c, s = jnp.cos(ang), jnp.sin(ang)
    g4 = _pass3(_stack_complex(c, -s))
    ct, st = jnp.swapaxes(c, 1, 2), jnp.swapaxes(s, 1, 2)
    g6 = _pass3(_stack_complex(ct, st))
    return dict(m2=m2, m2f=m2f, m8=m8, g4=g4, g6=g6)


def _store_halves(ref, rows, val):
    ref[0, rows, :] = val[:, :128]
    ref[1, rows, :] = val[:, 128:]


def _load_halves(ref, rows):
    return jnp.concatenate([ref[0, rows, :], ref[1, rows, :]], axis=1)


def _hyfft_kernel(xa_ref, xb_ref, m2_ref, g4_ref, kr_ref, ki_ref, br, bi, *, n1):
    s = pl.program_id(1)

    @pl.when(s == 0)
    def _():
        def body(n2, carry):
            rows = pl.ds(n2, n1, stride=HY_N2)
            rhs = _rhs3(jnp.concatenate([xa_ref[rows, :], xb_ref[rows, :]], axis=1))
            out = jnp.dot(m2_ref[...], rhs, preferred_element_type=f32)
            _store_halves(br, rows, out[:n1])
            _store_halves(bi, rows, out[n1:])
            return carry
        lax.fori_loop(0, HY_N2, body, 0, unroll=HY_UNROLL)

    for j in range(HY_SLABS):
        rows = pl.ds(pl.multiple_of((s * HY_SLABS + j) * HY_N2, HY_N2), HY_N2)
        y = jnp.concatenate([_load_halves(br, rows), _load_halves(bi, rows)], axis=0)
        z = jnp.dot(g4_ref[j], _rhs3(y), preferred_element_type=f32)
        kr_ref[j * HY_N2:(j + 1) * HY_N2, :] = z[:HY_N2]
        ki_ref[j * HY_N2:(j + 1) * HY_N2, :] = z[HY_N2:]


def _hyena_filter_fft(kern, tabs):
    n, nc = kern.shape
    n1 = n // HY_N2
    tn = 256
    sl = HY_SLABS * HY_N2
    spec_out = pl.BlockSpec((sl, tn), lambda t, s: (s, t))
    return pl.pallas_call(
        functools.partial(_hyfft_kernel, n1=n1),
        grid=(nc // tn, n1 // HY_SLABS),
        in_specs=[pl.BlockSpec((n, 128), lambda t, s: (0, 2 * t)),
                  pl.BlockSpec((n, 128), lambda t, s: (0, 2 * t + 1)),
                  pl.BlockSpec(tabs["m2f"].shape, lambda t, s: (0, 0)),
                  pl.BlockSpec((HY_SLABS,) + tabs["g4"].shape[1:], lambda t, s: (s, 0, 0))],
        out_specs=[spec_out, spec_out],
        out_shape=[jax.ShapeDtypeStruct((n, nc), f32)] * 2,
        scratch_shapes=[pltpu.VMEM((2, n, 128), f32), pltpu.VMEM((2, n, 128), f32)],
        compiler_params=_cp("arbitrary", "arbitrary"),
        name="hyena_filter_fft",
    )(kern, kern, tabs["m2f"], tabs["g4"])


def _hyconv_kernel(x_ref, m2_ref, g4_ref, g6_ref, m8_ref, kr_ref, ki_ref, o_ref, br, bi, *, n1):
    s = pl.program_id(1)
    ns = pl.num_programs(1)
    half = n1 // 2
    cw = x_ref.shape[2]

    @pl.when(s == 0)
    def _():
        def body(n2, carry):
            rows = pl.ds(n2, half, stride=HY_N2)
            xr = jnp.concatenate([x_ref[0, rows, :], x_ref[2, rows, :]], axis=1)
            xi = jnp.concatenate([x_ref[1, rows, :], x_ref[3, rows, :]], axis=1)
            rhs = _rhs3(jnp.concatenate([xr, xi], axis=0))
            out = jnp.dot(m2_ref[...], rhs, preferred_element_type=f32)
            brows = pl.ds(n2, n1, stride=HY_N2)
            _store_halves(br, brows, out[:n1])
            _store_halves(bi, brows, out[n1:])
            return carry
        lax.fori_loop(0, HY_N2, body, 0, unroll=HY_UNROLL)

    for j in range(HY_SLABS):
        srows = pl.ds(pl.multiple_of((s * HY_SLABS + j) * HY_N2, HY_N2), HY_N2)
        y = jnp.concatenate([_load_halves(br, srows), _load_halves(bi, srows)], axis=0)
        z = jnp.dot(g4_ref[j], _rhs3(y), preferred_element_type=f32)
        zr, zi = z[:HY_N2], z[HY_N2:]
        kr = jnp.concatenate([kr_ref[j * HY_N2:(j + 1) * HY_N2, :]] * 2, axis=1)
        ki = jnp.concatenate([ki_ref[j * HY_N2:(j + 1) * HY_N2, :]] * 2, axis=1)
        w = jnp.concatenate([zr * kr - zi * ki, zr * ki + zi * kr], axis=0)
        t = jnp.dot(g6_ref[j], _rhs3(w), preferred_element_type=f32)
        _store_halves(br, srows, t[:HY_N2])
        _store_halves(bi, srows, t[HY_N2:])

    @pl.when(s == ns - 1)
    def _():
        def body(n2, carry):
            rows = pl.ds(n2, n1, stride=HY_N2)
            rhs = _rhs3(jnp.concatenate([_load_halves(br, rows), _load_halves(bi, rows)], axis=0))
            out = jnp.dot(m8_ref[...], rhs, preferred_element_type=f32)
            orows = pl.ds(n2, half, stride=HY_N2)
            o_ref[0, orows, :] = out[:half, :cw]
            o_ref[2, orows, :] = out[:half, cw:]
            o_ref[1, orows, :] = out[half:, :cw]
            o_ref[3, orows, :] = out[half:, cw:]
            return carry
        lax.fori_loop(0, HY_N2, body, 0, unroll=HY_UNROLL)


def _hyena_conv(x, col0, khr, khi, order, tabs):
    seq_len = x.shape[1]
    n = 2 * seq_len
    n1 = n // HY_N2
    cw = 128
    sl = HY_SLABS * HY_N2
    nt = HY_WIDTH // cw
    kspec = pl.BlockSpec((sl, cw), lambda t, s: (s, order * nt + t))
    const = lambda a: pl.BlockSpec(a.shape, lambda t, s: (0,) * a.ndim)
    gspec = pl.BlockSpec((HY_SLABS,) + tabs["g4"].shape[1:], lambda t, s: (s, 0, 0))
    return pl.pallas_call(
        functools.partial(_hyconv_kernel, n1=n1),
        grid=(nt, n1 // HY_SLABS),
        in_specs=[pl.BlockSpec((BATCH, seq_len, cw), lambda t, s: (0, 0, col0 // cw + t)),
                  const(tabs["m2"]), gspec, gspec, const(tabs["m8"]), kspec, kspec],
        out_specs=pl.BlockSpec((BATCH, seq_len, cw), lambda t, s: (0, 0, t)),
        out_shape=jax.ShapeDtypeStruct((BATCH, seq_len, HY_WIDTH), f32),
        scratch_shapes=[pltpu.VMEM((2, n, cw), f32), pltpu.VMEM((2, n, cw), f32)],
        compiler_params=_cp("arbitrary", "arbitrary", vmem=60 * 1024 * 1024),
        name="hyena_long_conv",
    )(x, tabs["m2"], tabs["g4"], tabs["g6"], tabs["m8"], khr, khi)


def _hygate_kernel(c_ref, y_ref, g_ref, b_ref, o_ref):
    y = y_ref[...]
    o_ref[...] = (g_ref[...] * (c_ref[...] + b_ref[...] * y)).astype(o_ref.dtype)


def _hyena_gate(conv, y, ycol0, zc, gcol0, bias, out_dtype):
    tm, tn = 512, 512
    nrows = conv.shape[0]
    return pl.pallas_call(
        _hygate_kernel,
        grid=(nrows // tm, HY_WIDTH // tn),
        in_specs=[pl.BlockSpec((tm, tn), lambda i, j: (i, j)),
                  pl.BlockSpec((tm, tn), lambda i, j: (i, ycol0 // tn + j)),
                  pl.BlockSpec((tm, tn), lambda i, j: (i, gcol0 // tn + j)),
                  pl.BlockSpec((1, tn), lambda i, j: (0, j))],
        out_specs=pl.BlockSpec((tm, tn), lambda i, j: (i, j)),
        out_shape=jax.ShapeDtypeStruct((nrows, HY_WIDTH), out_dtype),
        compiler_params=_cp("arbitrary", "arbitrary"),
        name="hyena_gate",
    )(conv, y, zc, bias.reshape(1, HY_WIDTH))


def _short_dft_tables(seq_len):
    n = 2 * seq_len
    k = jnp.arange(n, dtype=jnp.int32)
    ang = (2.0 * math.pi / n) * ((k[:, None] * k[None, :]) % n).astype(f32)
    c, s = jnp.cos(ang), jnp.sin(ang)
    mx = _pass3(_stack_complex(c[:, :seq_len], -s[:, :seq_len]))
    mf = _pass3(jnp.concatenate([c, -s], axis=0))
    mi = _pass3(_stack_complex(c[:seq_len, :] / n, s[:seq_len, :] / n))
    return dict(mx=mx, mf=mf, mi=mi)


def _short_fft_kernel(x_ref, mf_ref, kr_ref, ki_ref):
    n = x_ref.shape[0]
    z = jnp.dot(mf_ref[...], _rhs3(x_ref[...]), preferred_element_type=f32)
    kr_ref[...] = z[:n]
    ki_ref[...] = z[n:]


def _short_filter_fft(kern, tabs):
    n, nc = kern.shape
    tn = 256
    spec = pl.BlockSpec((n, tn), lambda t: (0, t))
    return pl.pallas_call(
        _short_fft_kernel,
        grid=(nc // tn,),
        in_specs=[spec, pl.BlockSpec(tabs["mf"].shape, lambda t: (0, 0))],
        out_specs=[spec, spec],
        out_shape=[jax.ShapeDtypeStruct((n, nc), f32)] * 2,
        compiler_params=_cp("arbitrary"),
        name="hyena_short_filter_fft",
    )(kern, tabs["mf"])


def _short_conv_fft_kernel(x_ref, mx_ref, mi_ref, kr_ref, ki_ref, o_ref):
    seq_len = x_ref.shape[1]
    cw = x_ref.shape[2]
    n = 2 * seq_len
    xr = jnp.concatenate([x_ref[0], x_ref[2]], axis=1)
    xi = jnp.concatenate([x_ref[1], x_ref[3]], axis=1)
    z = jnp.dot(mx_ref[...], _rhs3(jnp.concatenate([xr, xi], axis=0)), preferred_element_type=f32)
    zr, zi = z[:n], z[n:]
    kr = jnp.concatenate([kr_ref[...]] * 2, axis=1)
    ki = jnp.concatenate([ki_ref[...]] * 2, axis=1)
    w = jnp.concatenate([zr * kr - zi * ki, zr * ki + zi * kr], axis=0)
    y = jnp.dot(mi_ref[...], _rhs3(w), preferred_element_type=f32)
    o_ref[0] = y[:seq_len, :cw]
    o_ref[2] = y[:seq_len, cw:]
    o_ref[1] = y[seq_len:, :cw]
    o_ref[3] = y[seq_len:, cw:]


def _short_long_conv(x, col0, khr, khi, order, tabs):
    seq_len = x.shape[1]
    n = 2 * seq_len
    cw = 128
    nt = HY_WIDTH // cw
    kspec = pl.BlockSpec((n, cw), lambda t: (0, order * nt + t))
    const = lambda a: pl.BlockSpec(a.shape, lambda t: (0,) * a.ndim)
    return pl.pallas_call(
        _short_conv_fft_kernel,
        grid=(nt,),
        in_specs=[pl.BlockSpec((BATCH, seq_len, cw), lambda t: (0, 0, col0 // cw + t)),
                  const(tabs["mx"]), const(tabs["mi"]), kspec, kspec],
        out_specs=pl.BlockSpec((BATCH, seq_len, cw), lambda t: (0, 0, t)),
        out_shape=jax.ShapeDtypeStruct((BATCH, seq_len, HY_WIDTH), f32),
        compiler_params=_cp("arbitrary"),
        name="hyena_short_long_conv",
    )(x, tabs["mx"], tabs["mi"], khr, khi)


def _hyena_branch(z, row0, seq_len, tabs, short_w, short_b, w1, b1, w2, b2, w3, freq, bias):
    two_stage = "g4" in tabs
    conv = _hyena_conv if two_stage else _short_long_conv
    nrows = BATCH * seq_len
    zc = _short_conv(z, short_w, short_b, row0, nrows, seq_len)
    kern = _hyena_filter(seq_len, w1, b1, w2, b2, w3, freq)
    khr, khi = (_hyena_filter_fft if two_stage else _short_filter_fft)(kern, tabs)
    zc3 = zc.reshape(BATCH, seq_len, 3 * HY_WIDTH)
    c1 = conv(zc3, 0, khr, khi, 0, tabs).reshape(nrows, HY_WIDTH)
    y1 = _hyena_gate(c1, zc, 0, zc, HY_WIDTH, bias[0], f32)
    c2 = conv(y1.reshape(BATCH, seq_len, HY_WIDTH), 0, khr, khi, 1, tabs).reshape(nrows, HY_WIDTH)
    return _hyena_gate(c2, y1, 0, zc, 2 * HY_WIDTH, bias[1], bf16)


def _rope_tables():
    half = ATT_HD // 2
    nf = half // 2
    inv = ROPE_BASE ** (-jnp.arange(nf, dtype=f32) / nf)
    pos = jnp.arange(SEQ)
    rows = (pos // GRID_W).astype(f32)[:, None] * inv[None, :]
    cols = (pos % GRID_W).astype(f32)[:, None] * inv[None, :]
    zero = jnp.zeros_like(rows)
    cos = jnp.concatenate([jnp.cos(rows)] * 2 + [jnp.cos(cols)] * 2, axis=-1)
    sin_up = jnp.concatenate([-jnp.sin(rows), zero, -jnp.sin(cols), zero], axis=-1)
    sin_dn = jnp.concatenate([zero, jnp.sin(rows), zero, jnp.sin(cols)], axis=-1)
    return cos, sin_up, sin_dn


def _rope_kernel(q_ref, k_ref, cos_ref, su_ref, sd_ref, qo_ref, ko_ref):
    cos, su, sd = cos_ref[...], su_ref[...], sd_ref[...]

    def rot(x):
        return x * cos + pltpu.roll(x, 96, 1) * su + pltpu.roll(x, 32, 1) * sd

    for h in range(ATT_HEADS):
        s = slice(h * ATT_HD, (h + 1) * ATT_HD)
        qo_ref[:, s] = rot(q_ref[:, s]).astype(qo_ref.dtype)
    for h in range(ATT_KV_HEADS):
        s = slice(h * ATT_HD, (h + 1) * ATT_HD)
        ko_ref[:, s] = rot(k_ref[:, s]).astype(ko_ref.dtype)


def _rope(z, tables):
    tm = 512
    nq = ATT_HEADS * ATT_HD
    nk = ATT_KV_HEADS * ATT_HD
    tab = pl.BlockSpec((tm, ATT_HD), lambda i: (i % (SEQ // tm), 0))
    return pl.pallas_call(
        _rope_kernel,
        grid=(NLAT // tm,),
        in_specs=[pl.BlockSpec((tm, nq), lambda i: (i, Z_Q // nq)),
                  pl.BlockSpec((tm, nk), lambda i: (i, Z_K // nk)),
                  tab, tab, tab],
        out_specs=[pl.BlockSpec((tm, nq), lambda i: (i, 0)),
                   pl.BlockSpec((tm, nk), lambda i: (i, 0))],
        out_shape=[jax.ShapeDtypeStruct((NLAT, nq), bf16), jax.ShapeDtypeStruct((NLAT, nk), bf16)],
        compiler_params=_cp("arbitrary"),
        name="rope",
    )(z, z, *tables)


_ATT_SCALE = ATT_HD ** -0.5
_NEG = float(np.finfo(np.float32).min)
_NT = (((1,), (1,)), ((), ()))


def _sink_column(sink_ref, h, rows):
    rg = lax.broadcasted_iota(jnp.int32, (rows * ATT_GROUP, 1), 0) // rows
    col = jnp.full((rows * ATT_GROUP, 1), sink_ref[h * ATT_GROUP + ATT_GROUP - 1], f32)
    for g in range(ATT_GROUP - 2, -1, -1):
        col = jnp.where(rg == g, sink_ref[h * ATT_GROUP + g], col)
    return col


def _attn_kernel(sink_ref, q_ref, kp_ref, kc_ref, kn_ref, vp_ref, vc_ref, vn_ref, kx_ref, vx_ref, o_ref):
    i = pl.program_id(1)
    nb = pl.num_programs(1)
    blk = ATT_BLOCK
    r = lax.broadcasted_iota(jnp.int32, (ATT_GROUP * blk, 3 * blk), 0) % blk
    c = lax.broadcasted_iota(jnp.int32, (ATT_GROUP * blk, 3 * blk), 1)
    lo = jnp.where(i > 0, 0, blk)
    hi = jnp.where(i < nb - 1, 3 * blk, 2 * blk)
    valid = (c >= r) & (c <= r + 2 * ATT_WINDOW) & (c >= lo) & (c < hi)
    for h in range(ATT_KV_HEADS):
        hs = slice(h * ATT_HD, (h + 1) * ATT_HD)
        k_win = jnp.concatenate([kp_ref[:, hs], kc_ref[:, hs], kn_ref[:, hs]], axis=0)
        v_win = jnp.concatenate([vp_ref[:, hs], vc_ref[:, hs], vn_ref[:, hs]], axis=0).astype(bf16)
        k_ctx = kx_ref[:, hs].astype(bf16)
        v_ctx = vx_ref[:, hs].astype(bf16)
        q = jnp.concatenate([q_ref[:, (h * ATT_GROUP + g) * ATT_HD:(h * ATT_GROUP + g + 1) * ATT_HD]
                             for g in range(ATT_GROUP)], axis=0)
        s_win = lax.dot_general(q, k_win, _NT, preferred_element_type=f32) * _ATT_SCALE
        s_win = jnp.where(valid, s_win, _NEG)
        s_ctx = lax.dot_general(q, k_ctx, _NT, preferred_element_type=f32) * _ATT_SCALE
        sink = _sink_column(sink_ref, h, blk)
        m = jnp.maximum(jnp.maximum(jnp.max(s_win, axis=-1, keepdims=True),
                                    jnp.max(s_ctx, axis=-1, keepdims=True)), sink)
        p_win = jnp.exp(s_win - m)
        p_ctx = jnp.exp(s_ctx - m)
        den = (jnp.sum(p_win, axis=-1, keepdims=True) + jnp.sum(p_ctx, axis=-1, keepdims=True)
               + jnp.exp(sink - m))
        o = (jnp.dot(p_win.astype(bf16), v_win, preferred_element_type=f32)
             + jnp.dot(p_ctx.astype(bf16), v_ctx, preferred_element_type=f32)) / den
        for g in range(ATT_GROUP):
            cs = slice((h * ATT_GROUP + g) * ATT_HD, (h * ATT_GROUP + g + 1) * ATT_HD)
            o_ref[:, cs] = o[g * blk:(g + 1) * blk].astype(o_ref.dtype)


def _window_attention(qr, kr, z, sink):
    blk = ATT_BLOCK
    nb = SEQ // blk
    nkv = ATT_KV_HEADS * ATT_HD
    cx = NLAT // CTX_LEN

    def krow(off):
        return lambda b, i: (b * nb + jnp.clip(i + off, 0, nb - 1), 0)

    def vrow(off):
        return lambda b, i: (b * nb + jnp.clip(i + off, 0, nb - 1), Z_V // nkv)

    return pl.pallas_call(
        _attn_kernel,
        grid=(BATCH, nb),
        in_specs=[pl.BlockSpec(memory_space=pltpu.SMEM),
                  pl.BlockSpec((blk, ATT_HEADS * ATT_HD), lambda b, i: (b * nb + i, 0)),
                  pl.BlockSpec((blk, nkv), krow(-1)),
                  pl.BlockSpec((blk, nkv), krow(0)),
                  pl.BlockSpec((blk, nkv), krow(1)),
                  pl.BlockSpec((blk, nkv), vrow(-1)),
                  pl.BlockSpec((blk, nkv), vrow(0)),
                  pl.BlockSpec((blk, nkv), vrow(1)),
                  pl.BlockSpec((CTX_LEN, nkv), lambda b, i: (cx + b, Z_K // nkv)),
                  pl.BlockSpec((CTX_LEN, nkv), lambda b, i: (cx + b, Z_V // nkv))],
        out_specs=pl.BlockSpec((blk, ATT_HEADS * ATT_HD), lambda b, i: (b * nb + i, 0)),
        out_shape=jax.ShapeDtypeStruct((NLAT, ATT_HEADS * ATT_HD), bf16),
        compiler_params=_cp("arbitrary", "arbitrary"),
        name="window_attention",
    )(sink, qr, kr, kr, kr, z, z, z, z, z)


def _ctx_attn_kernel(sink_ref, q_ref, k_ref, v_ref, o_ref):
    for h in range(ATT_KV_HEADS):
        hs = slice(h * ATT_HD, (h + 1) * ATT_HD)
        k = k_ref[:, hs].astype(bf16)
        v = v_ref[:, hs].astype(bf16)
        q = jnp.concatenate([q_ref[:, (h * ATT_GROUP + g) * ATT_HD:(h * ATT_GROUP + g + 1) * ATT_HD]
                             for g in range(ATT_GROUP)], axis=0).astype(bf16)
        s = lax.dot_general(q, k, _NT, preferred_element_type=f32) * _ATT_SCALE
        sink = _sink_column(sink_ref, h, CTX_LEN)
        m = jnp.maximum(jnp.max(s, axis=-1, keepdims=True), sink)
        p = jnp.exp(s - m)
        den = jnp.sum(p, axis=-1, keepdims=True) + jnp.exp(sink - m)
        o = jnp.dot(p.astype(bf16), v, preferred_element_type=f32) / den
        for g in range(ATT_GROUP):
            cs = slice((h * ATT_GROUP + g) * ATT_HD, (h * ATT_GROUP + g + 1) * ATT_HD)
            o_ref[:, cs] = o[g * CTX_LEN:(g + 1) * CTX_LEN].astype(o_ref.dtype)


def _context_attention(z, sink):
    nq = ATT_HEADS * ATT_HD
    nkv = ATT_KV_HEADS * ATT_HD
    cx = NLAT // CTX_LEN
    return pl.pallas_call(
        _ctx_attn_kernel,
        grid=(BATCH,),
        in_specs=[pl.BlockSpec(memory_space=pltpu.SMEM),
                  pl.BlockSpec((CTX_LEN, nq), lambda b: (cx + b, Z_Q // nq)),
                  pl.BlockSpec((CTX_LEN, nkv), lambda b: (cx + b, Z_K // nkv)),
                  pl.BlockSpec((CTX_LEN, nkv), lambda b: (cx + b, Z_V // nkv))],
        out_specs=pl.BlockSpec((CTX_LEN, nq), lambda b: (b, 0)),
        out_shape=jax.ShapeDtypeStruct((NCTX, nq), bf16),
        compiler_params=_cp("arbitrary"),
        name="context_attention",
    )(sink, z, z, z)


_ML_SCALE = ML_QK ** -0.5


def _split3(x):
    x1 = x.astype(bf16)
    r1 = x - x1.astype(f32)
    x2 = r1.astype(bf16)
    x3 = (r1 - x2.astype(f32)).astype(bf16)
    return x1, x2, x3


def _mlstm_kernel(q_ref, k_ref, v_ref, g_ref, gb_ref, o_ref, ct_ref, n_ref, m_ref):
    d = pl.program_id(1)
    c = pl.program_id(2)
    ch = ML_CHUNK

    @pl.when(c == 0)
    def _():
        ct_ref[...] = jnp.zeros_like(ct_ref)
        n_ref[...] = jnp.zeros_like(n_ref)
        m_ref[...] = jnp.zeros_like(m_ref)

    g = g_ref[...] + gb_ref[...]
    row = lax.broadcasted_iota(jnp.int32, (ch, ch), 0)
    col = lax.broadcasted_iota(jnp.int32, (ch, ch), 1)
    tri = (row - col) * (1 - 2 * d) >= 0
    tri_b = tri.astype(f32).astype(bf16)
    lf = jax.nn.log_sigmoid(g)
    l1, l2, l3 = _split3(lf)
    bcol = (jnp.dot(tri_b, l1, preferred_element_type=f32)
            + jnp.dot(tri_b, l2, preferred_element_type=f32)
            + jnp.dot(tri_b, l3, preferred_element_type=f32))
    bend = jnp.where(d == 0, bcol[ch - 1:ch, :], bcol[0:1, :])
    g_t = g.T
    b_t = bcol.T

    for h in range(ML_HEADS):
        qf = q_ref[:, h * ML_QK:(h + 1) * ML_QK] * _ML_SCALE
        kf = k_ref[:, h * ML_QK:(h + 1) * ML_QK]
        vf = v_ref[:, h * ML_V:(h + 1) * ML_V]
        q = qf.astype(bf16)
        k = kf.astype(bf16)
        li_c = g[:, h:h + 1]
        b_c = bcol[:, ML_HEADS + h:ML_HEADS + h + 1]
        li_r = g_t[h:h + 1, :]
        b_r = b_t[ML_HEADS + h:ML_HEADS + h + 1, :]
        m_prev = m_ref[h][:, 0:1]
        dmat = jnp.where(tri, b_c - b_r + li_r, -jnp.inf)
        inter = b_c + m_prev
        m_t = jnp.maximum(inter, jnp.max(dmat, axis=-1, keepdims=True))
        w_intra = jnp.exp(dmat - m_t)
        w_inter = jnp.exp(inter - m_t)
        s = lax.dot_general(q, k, _NT, preferred_element_type=f32) * w_intra
        qc = jnp.dot(q, ct_ref[h].astype(bf16), preferred_element_type=f32)
        num = jnp.dot(s.astype(bf16), vf.astype(bf16), preferred_element_type=f32) + w_inter * qc
        den = (jnp.sum(s, axis=-1, keepdims=True)
               + w_inter * jnp.sum(qf * n_ref[h], axis=-1, keepdims=True))
        o_ref[:, h * ML_V:(h + 1) * ML_V] = num / jnp.maximum(jnp.abs(den), jnp.exp(-m_t))

        b_e = bend[:, ML_HEADS + h:ML_HEADS + h + 1]
        g_c = b_e - b_c + li_c
        m_new = jnp.maximum(b_e + m_prev, jnp.max(g_c, axis=0, keepdims=True))
        w_s = jnp.exp(g_c - m_new)
        w_c = jnp.exp(b_e + m_prev - m_new)
        vw = (vf * w_s).astype(bf16)
        ct_ref[h] = w_c * ct_ref[h] + jnp.dot(kf.T.astype(bf16), vw, preferred_element_type=f32)
        n_ref[h] = w_c * n_ref[h] + jnp.sum(kf * w_s, axis=0, keepdims=True)
        m_ref[h] = jnp.broadcast_to(m_new, (1, 128))


def _mlstm(z, gate_b):
    ch = ML_CHUNK
    ncl = SEQ // ch
    nsteps = ncl + CTX_LEN // ch
    assert CTX_LEN == ch

    def rt(b, d, c):
        lat = b * ncl + jnp.where(d == 0, c - 1, ncl - c)
        return jnp.where(c == 0, NLAT // ch + b, lat)

    nq = ML_HEADS * ML_QK
    nv = ML_HEADS * ML_V
    return pl.pallas_call(
        _mlstm_kernel,
        grid=(BATCH, 2, nsteps),
        in_specs=[pl.BlockSpec((ch, nq), lambda b, d, c: (rt(b, d, c), Z_MQ // nq)),
                  pl.BlockSpec((ch, nq), lambda b, d, c: (rt(b, d, c), Z_MK // nq)),
                  pl.BlockSpec((ch, nv), lambda b, d, c: (rt(b, d, c), Z_MV // nv)),
                  pl.BlockSpec((ch, 128), lambda b, d, c: (rt(b, d, c), Z_GATE // 128 + d)),
                  pl.BlockSpec((None, 1, 128), lambda b, d, c: (d, 0, 0))],
        out_specs=pl.BlockSpec((None, ch, nv), lambda b, d, c: (d, rt(b, d, c), 0)),
        out_shape=jax.ShapeDtypeStruct((2, R, nv), f32),
        scratch_shapes=[pltpu.VMEM((ML_HEADS, ML_QK, ML_V), f32),
                        pltpu.VMEM((ML_HEADS, 1, ML_QK), f32),
                        pltpu.VMEM((ML_HEADS, 1, 128), f32)],
        compiler_params=_cp("arbitrary", "arbitrary", "arbitrary"),
        name="mlstm",
    )(z, z, z, z, gate_b)


def _mlstm_out_kernel(h_ref, zo_ref, g_ref, o_ref):
    for h in range(ML_HEADS):
        s = slice(h * ML_V, (h + 1) * ML_V)
        x = h_ref[0, :, s] + h_ref[1, :, s]
        xn = x * lax.rsqrt(jnp.mean(x * x, axis=-1, keepdims=True) + EPS) * g_ref[:, s]
        o_ref[:, s] = (xn * jax.nn.sigmoid(zo_ref[:, s])).astype(o_ref.dtype)


def _mlstm_out(hh, z, g, nrows):
    tm = 512
    nv = ML_HEADS * ML_V
    return pl.pallas_call(
        _mlstm_out_kernel,
        grid=(nrows // tm,),
        in_specs=[pl.BlockSpec((2, tm, nv), lambda i: (0, i, 0)),
                  pl.BlockSpec((tm, nv), lambda i: (i, Z_MO // nv)),
                  pl.BlockSpec((1, nv), lambda i: (0, 0))],
        out_specs=pl.BlockSpec((tm, nv), lambda i: (i, 0)),
        out_shape=jax.ShapeDtypeStruct((nrows, nv), bf16),
        compiler_params=_cp("arbitrary"),
        name="mlstm_out",
    )(hh, z, g.reshape(1, nv))


def _router_kernel(f_ref, w_ref, idx_ref, p_ref):
    logits = jnp.dot(f_ref[...].astype(bf16), w_ref[...], preferred_element_type=f32)
    lane = lax.broadcasted_iota(jnp.int32, logits.shape, 1).astype(f32)
    logits = jnp.where(lane < N_EXPERTS, logits, -jnp.inf)
    v1 = jnp.max(logits, axis=-1, keepdims=True)
    i1 = jnp.min(jnp.where(logits == v1, lane, 128.0), axis=-1, keepdims=True)
    rest = jnp.where(lane == i1, -jnp.inf, logits)
    v2 = jnp.max(rest, axis=-1, keepdims=True)
    i2 = jnp.min(jnp.where(rest == v2, lane, 128.0), axis=-1, keepdims=True)
    e = jnp.exp(v2 - v1)
    p1 = 1.0 / (1.0 + e)
    p2 = e / (1.0 + e)
    idx_ref[...] = jnp.where(lane == 0, i1, jnp.where(lane == 1, i2, 0.0)).astype(jnp.int32)
    p_ref[...] = jnp.where(lane == 0, p1, jnp.where(lane == 1, p2, 0.0))


def _router(f, w_router):
    tm = 512
    w = jnp.pad(w_router, ((0, 0), (0, 128 - N_EXPERTS))).astype(bf16)
    return pl.pallas_call(
        _router_kernel,
        grid=(NLAT // tm,),
        in_specs=[pl.BlockSpec((tm, D_MODEL), lambda i: (i, 0)),
                  pl.BlockSpec((D_MODEL, 128), lambda i: (0, 0))],
        out_specs=[pl.BlockSpec((tm, 128), lambda i: (i, 0)),
                   pl.BlockSpec((tm, 128), lambda i: (i, 0))],
        out_shape=[jax.ShapeDtypeStruct((NLAT, 128), jnp.int32),
                   jax.ShapeDtypeStruct((NLAT, 128), f32)],
        compiler_params=_cp("arbitrary"),
        name="router",
    )(f, w)


def _gather_kernel(tok_ref, src_ref, o_ref, sem):
    base = pl.program_id(0) * MOE_TILE

    def copy(r):
        return pltpu.make_async_copy(src_ref.at[pl.ds(tok_ref[base + r], 1)], o_ref.at[pl.ds(r, 1)], sem)

    def start(r, carry):
        copy(r).start()
        return carry

    def wait(r, carry):
        copy(r).wait()
        return carry

    lax.fori_loop(0, MOE_TILE, start, 0)
    lax.fori_loop(0, MOE_TILE, wait, 0)


def _gather_rows(src, rows, n_out):
    width = src.shape[1]
    return pl.pallas_call(
        _gather_kernel,
        grid_spec=pltpu.PrefetchScalarGridSpec(
            num_scalar_prefetch=1,
            grid=(n_out // MOE_TILE,),
            in_specs=[pl.BlockSpec(memory_space=pl.ANY)],
            out_specs=pl.BlockSpec((MOE_TILE, width), lambda i, tok: (i, 0)),
            scratch_shapes=[pltpu.SemaphoreType.DMA(())]),
        out_shape=jax.ShapeDtypeStruct((n_out, width), src.dtype),
        compiler_params=_cp("arbitrary"),
        name="gather_rows",
    )(rows, src)


def _moe_up_kernel(be_ref, nu_ref, a_ref, wg_ref, wu_ref, o_ref, wg_bf, wu_bf):
    i = pl.program_id(1)
    used = i < nu_ref[0]
    fresh = jnp.logical_or(i == 0, be_ref[i] != be_ref[jnp.maximum(i - 1, 0)])

    @pl.when(jnp.logical_and(used, fresh))
    def _():
        wg_bf[...] = wg_ref[...].astype(bf16)
        wu_bf[...] = wu_ref[...].astype(bf16)

    @pl.when(used)
    def _():
        a = a_ref[...].astype(bf16)
        g = jnp.dot(a, wg_bf[...], preferred_element_type=f32)
        u = jnp.dot(a, wu_bf[...], preferred_element_type=f32)
        o_ref[...] = (g * jax.nn.sigmoid(g) * u).astype(o_ref.dtype)

    @pl.when(i >= nu_ref[0])
    def _():
        o_ref[...] = jnp.zeros_like(o_ref)


def _moe_up(xs, wg, wu, blk_expert, n_used):
    tn = 512
    nblk = xs.shape[0] // MOE_TILE

    def row(j, i, be, nu):
        return (jnp.minimum(i, nu[0] - 1), 0)

    def wmap(j, i, be, nu):
        return (be[jnp.minimum(i, nu[0] - 1)], 0, j)

    return pl.pallas_call(
        _moe_up_kernel,
        grid_spec=pltpu.PrefetchScalarGridSpec(
            num_scalar_prefetch=2,
            grid=(FFN_EXPERT // tn, nblk),
            in_specs=[pl.BlockSpec((MOE_TILE, D_MODEL), row),
                      pl.BlockSpec((None, D_MODEL, tn), wmap),
                      pl.BlockSpec((None, D_MODEL, tn), wmap)],
            out_specs=pl.BlockSpec((MOE_TILE, tn), lambda j, i, be, nu: (i, j)),
            scratch_shapes=[pltpu.VMEM((D_MODEL, tn), bf16), pltpu.VMEM((D_MODEL, tn), bf16)]),
        out_shape=jax.ShapeDtypeStruct((xs.shape[0], FFN_EXPERT), bf16),
        compiler_params=_cp("arbitrary", "arbitrary"),
        name="moe_up",
    )(blk_expert, n_used, xs, wg, wu)


def _moe_down_kernel(be_ref, nu_ref, a_ref, w_ref, o_ref):
    i = pl.program_id(1)

    @pl.when(i < nu_ref[0])
    def _():
        o_ref[...] = jnp.dot(a_ref[...], w_ref[...], preferred_element_type=f32)

    @pl.when(i >= nu_ref[0])
    def _():
        o_ref[...] = jnp.zeros_like(o_ref)


def _moe_down(hs, wd, blk_expert, n_used):
    tn = 512
    nblk = hs.shape[0] // MOE_TILE

    def row(j, i, be, nu):
        return (jnp.minimum(i, nu[0] - 1), 0)

    def wmap(j, i, be, nu):
        return (be[jnp.minimum(i, nu[0] - 1)], 0, j)

    return pl.pallas_call(
        _moe_down_kernel,
        grid_spec=pltpu.PrefetchScalarGridSpec(
            num_scalar_prefetch=2,
            grid=(D_MODEL // tn, nblk),
            in_specs=[pl.BlockSpec((MOE_TILE, FFN_EXPERT), row),
                      pl.BlockSpec((None, FFN_EXPERT, tn), wmap)],
            out_specs=pl.BlockSpec((MOE_TILE, tn), lambda j, i, be, nu: (i, j))),
        out_shape=jax.ShapeDtypeStruct((hs.shape[0], D_MODEL), f32),
        compiler_params=_cp("arbitrary", "arbitrary"),
        name="moe_down",
    )(blk_expert, n_used, hs, wd)


_COMBINE_TM = 256


def _combine_kernel(slot_ref, yb_ref, x_ref, p_ref, g2_ref, fg_ref, o_ref, buf, sem):
    i = pl.program_id(0)
    cur = i % 2

    def copy(step, b, r, k):
        s = slot_ref[(step * _COMBINE_TM + r) * TOP_K + k]
        return pltpu.make_async_copy(yb_ref.at[pl.ds(s, 1)], buf.at[b, k, pl.ds(r, 1)], sem.at[b])

    def fetch(step, b):
        def start(r, carry):
            copy(step, b, r, 0).start()
            copy(step, b, r, 1).start()
            return carry
        lax.fori_loop(0, _COMBINE_TM, start, 0)

    @pl.when(i == 0)
    def _():
        fetch(0, 0)

    @pl.when(i + 1 < pl.num_programs(0))
    def _():
        fetch(i + 1, 1 - cur)

    def wait(r, carry):
        copy(i, cur, r, 0).wait()
        copy(i, cur, r, 1).wait()
        return carry

    lax.fori_loop(0, _COMBINE_TM, wait, 0)
    p = p_ref[...]
    y = buf[cur, 0] * p[:, 0:1] + buf[cur, 1] * p[:, 1:2]
    x = x_ref[...] + g2_ref[...] * y
    o_ref[...] = x * lax.rsqrt(jnp.mean(x * x, axis=-1, keepdims=True) + EPS) * fg_ref[...]


def _combine_final(slot, yb, xa, probs, modt, final_g):
    tm = _COMBINE_TM
    return pl.pallas_call(
        _combine_kernel,
        grid_spec=pltpu.PrefetchScalarGridSpec(
            num_scalar_prefetch=1,
            grid=(NLAT // tm,),
            in_specs=[pl.BlockSpec(memory_space=pl.ANY),
                      pl.BlockSpec((tm, D_MODEL), lambda i, s: (i, 0)),
                      pl.BlockSpec((tm, 128), lambda i, s: (i, 0)),
                      pl.BlockSpec((None, 1, D_MODEL), lambda i, s: (i // (SEQ // tm), 0, 5)),
                      pl.BlockSpec((1, D_MODEL), lambda i, s: (0, 0))],
            out_specs=pl.BlockSpec((tm, D_MODEL), lambda i, s: (i, 0)),
            scratch_shapes=[pltpu.VMEM((2, TOP_K, tm, D_MODEL), f32),
                            pltpu.SemaphoreType.DMA((2,))]),
        out_shape=jax.ShapeDtypeStruct((NLAT, D_MODEL), f32),
        compiler_params=_cp("arbitrary"),
        name="moe_combine_final_norm",
    )(slot, yb, xa, probs, modt, final_g.reshape(1, D_MODEL))


def _moe_routing(top_i):
    a = NLAT * TOP_K
    e_flat = top_i.reshape(a)
    onehot = (e_flat[:, None] == jnp.arange(N_EXPERTS)[None, :]).astype(jnp.int32)
    csum = jnp.cumsum(onehot, axis=0)
    rank = jnp.sum(onehot * csum, axis=1) - 1
    counts = csum[-1]
    padded = (counts + MOE_TILE - 1) // MOE_TILE * MOE_TILE
    pad_end = jnp.cumsum(padded)
    pad_start = pad_end - padded
    slot = (pad_start[e_flat] + rank).astype(jnp.int32)
    n_rows = a + N_EXPERTS * MOE_TILE
    nblk = n_rows // MOE_TILE
    slot_tok = jnp.zeros((n_rows,), jnp.int32).at[slot].set(jnp.arange(a, dtype=jnp.int32) // TOP_K)
    blk_expert = jnp.minimum(jnp.searchsorted(pad_end, jnp.arange(nblk) * MOE_TILE, side='right'),
                             N_EXPERTS - 1).astype(jnp.int32)
    n_used = (pad_end[-1:] // MOE_TILE).astype(jnp.int32)
    return slot, slot_tok, blk_expert, n_used, n_rows


def _in_proj_weight(w, gate_b):
    o = np.cumsum((0,) + (3072, 1024, 256, 256, 512, 512, 1024, 1024, 16, 6144))
    hy, q, k, v, mq, mk, mv, mo, gt, mg = [w[:, o[i]:o[i + 1]] for i in range(10)]
    pad = jnp.zeros((D_MODEL, 128 - 2 * ML_HEADS), w.dtype)
    gates = [jnp.concatenate([gt[:, 8 * d:8 * d + 8], pad], axis=1) for d in range(2)]
    wz = jnp.concatenate([hy, q, mv, mo, mg, k, v, mq, mk] + gates
                         + [jnp.zeros((D_MODEL, Z_COLS - Z_GATE - 256), w.dtype)], axis=1).astype(bf16)
    gb = jnp.pad(gate_b.reshape(2, 1, 2 * ML_HEADS), ((0, 0), (0, 0), (0, 128 - 2 * ML_HEADS)))
    return wz, gb


def kernel(x, c, ctx, c_ctx, w_mod, b_mod, norm_mix_g, norm_ffn_g, w_in, hy_short_w, hy_short_b, hy_w1, hy_b1, hy_w2, hy_b2, hy_w3, hy_freq, hy_bias, att_sink, ml_gate_b, ml_norm_g, w_branch, w_out, ffn_wg, ffn_wu, ffn_wd, moe_router, moe_wg, moe_wu, moe_wd, final_g):
    xa = jnp.concatenate([x.reshape(NLAT, D_MODEL), ctx.reshape(NCTX, D_MODEL)], axis=0)
    c_all = jnp.concatenate([c, c_ctx[None], jnp.zeros((8 - BATCH - 1, D_MODEL), f32)], axis=0)
    mod = _modulation(c_all, w_mod, b_mod)
    rope_tabs = _rope_tables()
    dft_tabs = _dft_tables(SEQ)
    ctx_tabs = _short_dft_tables(CTX_LEN)
    out = None
    for layer in range(DEPTH):
        last = layer == DEPTH - 1
        modt = mod[layer].reshape(8, 1, 6 * D_MODEL)
        n_mix = NLAT if last else R

        u = _normmod(xa, norm_mix_g[layer], modt, 0, R)
        wz, gate_b = _in_proj_weight(w_in[layer], ml_gate_b[layer])
        z = _in_proj(u, wz)

        hy = (hy_short_w[layer], hy_short_b[layer], hy_w1[layer], hy_b1[layer], hy_w2[layer],
              hy_b2[layer], hy_w3[layer], hy_freq[layer], hy_bias[layer])
        a_rows = _hyena_branch(z, 0, SEQ, dft_tabs, *hy)

        qr, kr = _rope(z, rope_tabs)
        b_rows = _window_attention(qr, kr, z, att_sink[layer])

        hh = _mlstm(z, gate_b)
        c_rows = _mlstm_out(hh, z, ml_norm_g[layer], n_mix)

        if not last:
            a_rows = jnp.concatenate([a_rows, _hyena_branch(z, NLAT, CTX_LEN, ctx_tabs, *hy)], axis=0)
            b_rows = jnp.concatenate([b_rows, _context_attention(z, att_sink[layer])], axis=0)

        ymid = _merge(a_rows, b_rows, c_rows, w_branch[layer].astype(bf16), z, n_mix)
        xa = _mm_resid(ymid, w_out[layer].astype(bf16), xa, modt, 2, n_mix)

        f = _normmod(xa, norm_ffn_g[layer], modt, 3, n_mix, bf16 if layer % 2 == 0 else f32)
        if layer % 2 == 0:
            e = layer // 2
            hmid = _ffn_up(f, ffn_wg[e].astype(bf16), ffn_wu[e].astype(bf16))
            xa = _mm_resid(hmid, ffn_wd[e].astype(bf16), xa, modt, 5, R)
        else:
            e = layer // 2
            top_i, probs = _router(f, moe_router[e])
            slot, slot_tok, blk_expert, n_used, n_rows = _moe_routing(top_i[:, :TOP_K])
            xs = _gather_rows(f, slot_tok, n_rows)
            hs = _moe_up(xs, moe_wg[e], moe_wu[e], blk_expert, n_used)
            yb = _moe_down(hs, moe_wd[e].astype(bf16), blk_expert, n_used)
            assert last
            out = _combine_final(slot, yb, xa, probs, modt, final_g)
    return out.reshape(BATCH, SEQ, D_MODEL)
```

```python
import functools
import math

import numpy as np
import jax
import jax.numpy as jnp
from jax import lax
from jax.experimental import pallas as pl
from jax.experimental.pallas import tpu as pltpu

f32 = jnp.float32
bf16 = jnp.bfloat16

D_MODEL = 2048
BATCH = 4
SEQ = 4096
DEPTH = 2
GRID_W = 64
CTX_LEN = 256
EPS = 1e-6

HY_WIDTH = D_MODEL // 2
HY_ORDER = 2
HY_BANDS = 16
HY_EMB = 2 * HY_BANDS + 1
HY_FFN = 64
HY_MIN_DECAY = -3.0701134573253943
HY_MAX_DECAY = -15.35056728662697
HY_N2 = 128
HY_SLABS = 4
HY_UNROLL = 8

ATT_HD = 128
ATT_HEADS = 8
ATT_KV_HEADS = 2
ATT_GROUP = 4
ATT_WINDOW = 128
ATT_BLOCK = 128
ROPE_BASE = 10000.0

ML_HEADS = 4
ML_V = 256
ML_QK = 128
ML_CHUNK = 256

N_BRANCH = 3
BRANCH_W = D_MODEL // 2
FFN_DENSE = 5632
N_EXPERTS = 8
TOP_K = 2
FFN_EXPERT = 7168
MOE_TILE = 512

NCTX = BATCH * CTX_LEN
NLAT = BATCH * SEQ
R = NCTX + NLAT

Z_HY = 0
Z_Q = 3072
Z_MV = 4096
Z_MO = 5120
Z_MERGE = 6144
Z_K = 12288
Z_V = 12544
Z_MQ = 12800
Z_MK = 13312
Z_GATE = 13824
Z_COLS = 14336

VMEM_LIMIT = 56 * 1024 * 1024


def _cp(*sem, vmem=VMEM_LIMIT):
    return pltpu.CompilerParams(dimension_semantics=sem, vmem_limit_bytes=vmem)


def _modrow(i, tm):
    return jnp.where(i >= NLAT // tm, BATCH, i // (SEQ // tm))


def _mod_kernel(c_ref, w_ref, b_ref, o_ref):
    c = c_ref[...]
    a = (c * jax.nn.sigmoid(c)).astype(bf16)
    o_ref[...] = jnp.dot(a, w_ref[...].astype(bf16), preferred_element_type=f32) + b_ref[...]


def _modulation(c_all, w_mod, b_mod):
    tn = 1024
    n = 6 * D_MODEL
    return pl.pallas_call(
        _mod_kernel,
        grid=(DEPTH, n // tn),
        in_specs=[pl.BlockSpec((8, D_MODEL), lambda l, j: (0, 0)),
                  pl.BlockSpec((None, D_MODEL, tn), lambda l, j: (l, 0, j)),
                  pl.BlockSpec((None, 1, tn), lambda l, j: (l, 0, j))],
        out_specs=pl.BlockSpec((None, 8, tn), lambda l, j: (l, 0, j)),
        out_shape=jax.ShapeDtypeStruct((DEPTH, 8, n), f32),
        compiler_params=_cp("arbitrary", "arbitrary"),
        name="modulation",
    )(c_all, w_mod, b_mod.reshape(DEPTH, 1, n))


def _normmod_kernel(x_ref, g_ref, sh_ref, sc_ref, o_ref):
    x = x_ref[...]
    y = x * lax.rsqrt(jnp.mean(x * x, axis=-1, keepdims=True) + EPS) * g_ref[...]
    o_ref[...] = (y * (1.0 + sc_ref[...]) + sh_ref[...]).astype(o_ref.dtype)


def _normmod(xa, g, modt, which, nrows, out_dtype=bf16):
    tm = 512
    return pl.pallas_call(
        _normmod_kernel,
        grid=(nrows // tm,),
        in_specs=[pl.BlockSpec((tm, D_MODEL), lambda i: (i, 0)),
                  pl.BlockSpec((1, D_MODEL), lambda i: (0, 0)),
                  pl.BlockSpec((None, 1, D_MODEL), lambda i: (_modrow(i, tm), 0, which)),
                  pl.BlockSpec((None, 1, D_MODEL), lambda i: (_modrow(i, tm), 0, which + 1))],
        out_specs=pl.BlockSpec((tm, D_MODEL), lambda i: (i, 0)),
        out_shape=jax.ShapeDtypeStruct((nrows, D_MODEL), out_dtype),
        compiler_params=_cp("arbitrary"),
        name="normmod",
    )(xa, g.reshape(1, D_MODEL), modt, modt)


def _mm_kernel(a_ref, w_ref, o_ref):
    o_ref[...] = jnp.dot(a_ref[...], w_ref[...], preferred_element_type=f32).astype(o_ref.dtype)


def _in_proj(u, w):
    tm, tn = 1024, 1024
    return pl.pallas_call(
        _mm_kernel,
        grid=(Z_COLS // tn, R // tm),
        in_specs=[pl.BlockSpec((tm, D_MODEL), lambda j, i: (i, 0)),
                  pl.BlockSpec((D_MODEL, tn), lambda j, i: (0, j))],
        out_specs=pl.BlockSpec((tm, tn), lambda j, i: (i, j)),
        out_shape=jax.ShapeDtypeStruct((R, Z_COLS), f32),
        compiler_params=_cp("arbitrary", "arbitrary"),
        name="in_proj",
    )(u, w)


def _mm_resid_kernel(a_ref, w_ref, x_ref, g_ref, o_ref):
    y = jnp.dot(a_ref[...], w_ref[...], preferred_element_type=f32)
    o_ref[...] = x_ref[...] + g_ref[...] * y


def _mm_resid(a, w, xa, modt, which, nrows):
    k = a.shape[1]
    tm, tn = (1024, 1024) if k <= D_MODEL else (512, 1024)
    nj = D_MODEL // tn
    return pl.pallas_call(
        _mm_resid_kernel,
        grid=(nj, nrows // tm),
        in_specs=[pl.BlockSpec((tm, k), lambda j, i: (i, 0)),
                  pl.BlockSpec((k, tn), lambda j, i: (0, j)),
                  pl.BlockSpec((tm, tn), lambda j, i: (i, j)),
                  pl.BlockSpec((None, 1, tn), lambda j, i: (_modrow(i, tm), 0, which * nj + j))],
        out_specs=pl.BlockSpec((tm, tn), lambda j, i: (i, j)),
        out_shape=jax.ShapeDtypeStruct((nrows, D_MODEL), f32),
        compiler_params=_cp("arbitrary", "arbitrary"),
        name="mm_resid",
    )(a, w, xa, modt)


def _swiglu_kernel(a_ref, wg_ref, wu_ref, o_ref):
    a = a_ref[...]
    g = jnp.dot(a, wg_ref[...], preferred_element_type=f32)
    u = jnp.dot(a, wu_ref[...], preferred_element_type=f32)
    o_ref[...] = (g * jax.nn.sigmoid(g) * u).astype(o_ref.dtype)


def _ffn_up(f, wg, wu):
    tm, tn = 1024, 512
    n = wg.shape[1]
    return pl.pallas_call(
        _swiglu_kernel,
        grid=(n // tn, R // tm),
        in_specs=[pl.BlockSpec((tm, D_MODEL), lambda j, i: (i, 0)),
                  pl.BlockSpec((D_MODEL, tn), lambda j, i: (0, j)),
                  pl.BlockSpec((D_MODEL, tn), lambda j, i: (0, j))],
        out_specs=pl.BlockSpec((tm, tn), lambda j, i: (i, j)),
        out_shape=jax.ShapeDtypeStruct((R, n), bf16),
        compiler_params=_cp("arbitrary", "arbitrary"),
        name="ffn_up",
    )(f, wg, wu)


def _merge_kernel(a_ref, b_ref, c_ref, w_ref, g0_ref, g1_ref, g2_ref, o_ref):
    y = jax.nn.sigmoid(g0_ref[...]) * jnp.dot(a_ref[...], w_ref[0], preferred_element_type=f32)
    y += jax.nn.sigmoid(g1_ref[...]) * jnp.dot(b_ref[...], w_ref[1], preferred_element_type=f32)
    y += jax.nn.sigmoid(g2_ref[...]) * jnp.dot(c_ref[...], w_ref[2], preferred_element_type=f32)
    o_ref[...] = y.astype(o_ref.dtype)


def _merge(a, b, c, wb, z, nrows):
    tm, tn = 512, 1024
    nj = D_MODEL // tn
    act = pl.BlockSpec((tm, BRANCH_W), lambda j, i: (i, 0))

    def gate(br):
        return pl.BlockSpec((tm, tn), lambda j, i: (i, (Z_MERGE + br * D_MODEL) // tn + j))

    return pl.pallas_call(
        _merge_kernel,
        grid=(nj, nrows // tm),
        in_specs=[act, act, act,
                  pl.BlockSpec((N_BRANCH, BRANCH_W, tn), lambda j, i: (0, 0, j)),
                  gate(0), gate(1), gate(2)],
        out_specs=pl.BlockSpec((tm, tn), lambda j, i: (i, j)),
        out_shape=jax.ShapeDtypeStruct((nrows, D_MODEL), bf16),
        compiler_params=_cp("arbitrary", "arbitrary"),
        name="merge",
    )(a, b, c, wb, z, z, z)


def _short_conv_kernel(x_ref, xp_ref, xn_ref, w_ref, b_ref, o_ref, *, tiles_per_seq):
    i = pl.program_id(0)
    tm = x_ref.shape[0]
    x = x_ref[...]
    first = i % tiles_per_seq == 0
    last = i % tiles_per_seq == tiles_per_seq - 1
    prev = jnp.where(first, 0.0, xp_ref[7:8, :])
    nxt = jnp.where(last, 0.0, xn_ref[0:1, :])
    row = lax.broadcasted_iota(jnp.int32, x.shape, 0)
    up = jnp.where(row == 0, prev, pltpu.roll(x, 1, 0))
    dn = jnp.where(row == tm - 1, nxt, pltpu.roll(x, tm - 1, 0))
    o_ref[...] = w_ref[0:1, :] * up + w_ref[1:2, :] * x + w_ref[2:3, :] * dn + b_ref[...]


def _short_conv(z, w, b, row0, nrows, seq_len):
    tm, tn = 256, 1536
    t0 = row0 // tm
    nc = 3 * HY_WIDTH
    last8 = R // 8 - 1
    return pl.pallas_call(
        functools.partial(_short_conv_kernel, tiles_per_seq=seq_len // tm),
        grid=(nrows // tm, nc // tn),
        in_specs=[pl.BlockSpec((tm, tn), lambda i, j: (i + t0, j)),
                  pl.BlockSpec((8, tn), lambda i, j: (jnp.maximum((i + t0) * (tm // 8) - 1, 0), j)),
                  pl.BlockSpec((8, tn), lambda i, j: (jnp.minimum((i + t0 + 1) * (tm // 8), last8), j)),
                  pl.BlockSpec((3, tn), lambda i, j: (0, j)),
                  pl.BlockSpec((1, tn), lambda i, j: (0, j))],
        out_specs=pl.BlockSpec((tm, tn), lambda i, j: (i, j)),
        out_shape=jax.ShapeDtypeStruct((nrows, nc), f32),
        compiler_params=_cp("arbitrary", "arbitrary"),
        name="hyena_short_conv",
    )(z, z, z, w, b.reshape(1, nc))


def _hyfilt_kernel(w1_ref, b1_ref, w2_ref, b2_ref, w3_ref, fr_ref, dl_ref, o_ref, *, seq_len):
    tm = o_ref.shape[0]
    r = pl.program_id(0) * tm + lax.broadcasted_iota(jnp.int32, (tm, 1), 0)
    p = jnp.where(r < seq_len, r, 2 * seq_len - r)
    t = p.astype(f32) / seq_len
    lane = lax.broadcasted_iota(jnp.int32, (tm, 128), 1)
    band = jnp.where(lane <= HY_BANDS, lane, lane - HY_BANDS).astype(f32)
    ang = ((2.0 * math.pi) * t) * band
    feats = jnp.where(lane == 0, t,
                      jnp.where(lane <= HY_BANDS, jnp.sin(ang),
                                jnp.where(lane <= 2 * HY_BANDS, jnp.cos(ang), 0.0)))
    h = jnp.dot(feats.astype(bf16), w1_ref[...].astype(bf16), preferred_element_type=f32) + b1_ref[...]
    h = jnp.sin(fr_ref[0:1, :] * h)
    h = jnp.dot(h.astype(bf16), w2_ref[...].astype(bf16), preferred_element_type=f32) + b2_ref[...]
    h = jnp.sin(fr_ref[1:2, :] * h)
    k = jnp.dot(h.astype(bf16), w3_ref[...].astype(bf16), preferred_element_type=f32)
    k = k * jnp.exp(-t * dl_ref[...])
    o_ref[...] = jnp.where(r == seq_len, 0.0, k)


def _hyena_filter(seq_len, w1, b1, w2, b2, w3, freq):
    nc = HY_ORDER * HY_WIDTH
    tm, tn = min(512, seq_len), nc
    deltas = jnp.abs(jnp.linspace(HY_MIN_DECAY, HY_MAX_DECAY, HY_WIDTH, dtype=f32))
    dl = jnp.tile(deltas, HY_ORDER).reshape(1, nc)
    w1p = jnp.pad(w1, ((0, 128 - HY_EMB), (0, 0)))
    per_dir = nc // tn
    full = lambda i, j: (0, 0)
    return pl.pallas_call(
        functools.partial(_hyfilt_kernel, seq_len=seq_len),
        grid=(2 * seq_len // tm, per_dir),
        in_specs=[pl.BlockSpec((128, HY_FFN), full),
                  pl.BlockSpec((1, HY_FFN), full),
                  pl.BlockSpec((HY_FFN, HY_FFN), full),
                  pl.BlockSpec((1, HY_FFN), full),
                  pl.BlockSpec((HY_FFN, tn), lambda i, j: (0, jnp.where(i >= seq_len // tm, per_dir, 0) + j)),
                  pl.BlockSpec((2, HY_FFN), full),
                  pl.BlockSpec((1, tn), lambda i, j: (0, j))],
        out_specs=pl.BlockSpec((tm, tn), lambda i, j: (i, j)),
        out_shape=jax.ShapeDtypeStruct((2 * seq_len, nc), f32),
        compiler_params=_cp("arbitrary", "arbitrary"),
        name="hyena_filter",
    )(w1p, b1.reshape(1, HY_FFN), w2, b2.reshape(1, HY_FFN), w3, freq, dl)


def _pass3(m):
    hi = m.astype(bf16)
    lo = (m - hi.astype(f32)).astype(bf16)
    return jnp.concatenate([hi, hi, lo], axis=-1)


def _rhs3(x):
    hi = x.astype(bf16)
    lo = (x - hi.astype(f32)).astype(bf16)
    return jnp.concatenate([hi, lo, hi], axis=0)


def _stack_complex(ar, ai):
    return jnp.concatenate([jnp.concatenate([ar, -ai], axis=-1),
                            jnp.concatenate([ai, ar], axis=-1)], axis=-2)


def _dft_tables(seq_len):
    n = 2 * seq_len
    n1 = n // HY_N2
    half = n1 // 2
    i1 = jnp.arange(n1, dtype=jnp.int32)
    ang1 = (2.0 * math.pi / n1) * ((i1[:, None] * i1[None, :]) % n1).astype(f32)
    c1, s1 = jnp.cos(ang1), jnp.sin(ang1)
    m2 = _pass3(_stack_complex(c1[:, :half], -s1[:, :half]))
    m2f = _pass3(jnp.concatenate([c1, -s1], axis=0))
    m8 = _pass3(_stack_complex(c1[:half, :] / n, s1[:half, :] / n))
    i2 = jnp.arange(HY_N2, dtype=jnp.int32)
    k = i1[:, None, None] + n1 * i2[None, :, None]
    ang = (2.0 * math.pi / n) * ((i2[None, None, :] * k) % n).astype(f32)
    c, s = jnp.cos(ang), jnp.sin(ang)
    g4 = _pass3(_stack_complex(c, -s))
    ct, st = jnp.swapaxes(c, 1, 2), jnp.swapaxes(s, 1, 2)
    g6 = _pass3(_stack_complex(ct, st))
    return dict(m2=m2, m2f=m2f, m8=m8, g4=g4, g6=g6)


def _store_halves(ref, rows, val):
    ref[0, rows, :] = val[:, :128]
    ref[1, rows, :] = val[:, 128:]


def _load_halves(ref, rows):
    return jnp.concatenate([ref[0, rows, :], ref[1, rows, :]], axis=1)


def _hyfft_kernel(xa_ref, xb_ref, m2_ref, g4_ref, kr_ref, ki_ref, br, bi, *, n1):
    s = pl.program_id(1)

    @pl.when(s == 0)
    def _():
        def body(n2, carry):
            rows = pl.ds(n2, n1, stride=HY_N2)
            rhs = _rhs3(jnp.concatenate([xa_ref[rows, :], xb_ref[rows, :]], axis=1))
            out = jnp.dot(m2_ref[...], rhs, preferred_element_type=f32)
            _store_halves(br, rows, out[:n1])
            _store_halves(bi, rows, out[n1:])
            return carry
        lax.fori_loop(0, HY_N2, body, 0, unroll=HY_UNROLL)

    for j in range(HY_SLABS):
        rows = pl.ds(pl.multiple_of((s * HY_SLABS + j) * HY_N2, HY_N2), HY_N2)
        y = jnp.concatenate([_load_halves(br, rows), _load_halves(bi, rows)], axis=0)
        z = jnp.dot(g4_ref[j], _rhs3(y), preferred_element_type=f32)
        kr_ref[j * HY_N2:(j + 1) * HY_N2, :] = z[:HY_N2]
        ki_ref[j * HY_N2:(j + 1) * HY_N2, :] = z[HY_N2:]


def _hyena_filter_fft(kern, tabs):
    n, nc = kern.shape
    n1 = n // HY_N2
    tn = 256
    sl = HY_SLABS * HY_N2
    spec_out = pl.BlockSpec((sl, tn), lambda t, s: (s, t))
    return pl.pallas_call(
        functools.partial(_hyfft_kernel, n1=n1),
        grid=(nc // tn, n1 // HY_SLABS),
        in_specs=[pl.BlockSpec((n, 128), lambda t, s: (0, 2 * t)),
                  pl.BlockSpec((n, 128), lambda t, s: (0, 2 * t + 1)),
                  pl.BlockSpec(tabs["m2f"].shape, lambda t, s: (0, 0)),
                  pl.BlockSpec((HY_SLABS,) + tabs["g4"].shape[1:], lambda t, s: (s, 0, 0))],
        out_specs=[spec_out, spec_out],
        out_shape=[jax.ShapeDtypeStruct((n, nc), f32)] * 2,
        scratch_shapes=[pltpu.VMEM((2, n, 128), f32), pltpu.VMEM((2, n, 128), f32)],
        compiler_params=_cp("arbitrary", "arbitrary"),
        name="hyena_filter_fft",
    )(kern, kern, tabs["m2f"], tabs["g4"])


def _hyconv_kernel(x_ref, m2_ref, g4_ref, g6_ref, m8_ref, kr_ref, ki_ref, o_ref, br, bi, *, n1):
    s = pl.program_id(1)
    ns = pl.num_programs(1)
    half = n1 // 2
    cw = x_ref.shape[2]

    @pl.when(s == 0)
    def _():
        def body(n2, carry):
            rows = pl.ds(n2, half, stride=HY_N2)
            xr = jnp.concatenate([x_ref[0, rows, :], x_ref[2, rows, :]], axis=1)
            xi = jnp.concatenate([x_ref[1, rows, :], x_ref[3, rows, :]], axis=1)
            rhs = _rhs3(jnp.concatenate([xr, xi], axis=0))
            out = jnp.dot(m2_ref[...], rhs, preferred_element_type=f32)
            brows = pl.ds(n2, n1, stride=HY_N2)
            _store_halves(br, brows, out[:n1])
            _store_halves(bi, brows, out[n1:])
            return carry
        lax.fori_loop(0, HY_N2, body, 0, unroll=HY_UNROLL)

    for j in range(HY_SLABS):
        srows = pl.ds(pl.multiple_of((s * HY_SLABS + j) * HY_N2, HY_N2), HY_N2)
        y = jnp.concatenate([_load_halves(br, srows), _load_halves(bi, srows)], axis=0)
        z = jnp.dot(g4_ref[j], _rhs3(y), preferred_element_type=f32)
        zr, zi = z[:HY_N2], z[HY_N2:]
        kr = jnp.concatenate([kr_ref[j * HY_N2:(j + 1) * HY_N2, :]] * 2, axis=1)
        ki = jnp.concatenate([ki_ref[j * HY_N2:(j + 1) * HY_N2, :]] * 2, axis=1)
        w = jnp.concatenate([zr * kr - zi * ki, zr * ki + zi * kr], axis=0)
        t = jnp.dot(g6_ref[j], _rhs3(w), preferred_element_type=f32)
        _store_halves(br, srows, t[:HY_N2])
        _store_halves(bi, srows, t[HY_N2:])

    @pl.when(s == ns - 1)
    def _():
        def body(n2, carry):
            rows = pl.ds(n2, n1, stride=HY_N2)
            rhs = _rhs3(jnp.concatenate([_load_halves(br, rows), _load_halves(bi, rows)], axis=0))
            out = jnp.dot(m8_ref[...], rhs, preferred_element_type=f32)
            orows = pl.ds(n2, half, stride=HY_N2)
            o_ref[0, orows, :] = out[:half, :cw]
            o_ref[2, orows, :] = out[:half, cw:]
            o_ref[1, orows, :] = out[half:, :cw]
            o_ref[3, orows, :] = out[half:, cw:]
            return carry
        lax.fori_loop(0, HY_N2, body, 0, unroll=HY_UNROLL)


def _hyena_conv(x, col0, khr, khi, order, tabs):
    seq_len = x.shape[1]
    n = 2 * seq_len
    n1 = n // HY_N2
    cw = 128
    sl = HY_SLABS * HY_N2
    nt = HY_WIDTH // cw
    kspec = pl.BlockSpec((sl, cw), lambda t, s: (s, order * nt + t))
    const = lambda a: pl.BlockSpec(a.shape, lambda t, s: (0,) * a.ndim)
    gspec = pl.BlockSpec((HY_SLABS,) + tabs["g4"].shape[1:], lambda t, s: (s, 0, 0))
    return pl.pallas_call(
        functools.partial(_hyconv_kernel, n1=n1),
        grid=(nt, n1 // HY_SLABS),
        in_specs=[pl.BlockSpec((BATCH, seq_len, cw), lambda t, s: (0, 0, col0 // cw + t)),
                  const(tabs["m2"]), gspec, gspec, const(tabs["m8"]), kspec, kspec],
        out_specs=pl.BlockSpec((BATCH, seq_len, cw), lambda t, s: (0, 0, t)),
        out_shape=jax.ShapeDtypeStruct((BATCH, seq_len, HY_WIDTH), f32),
        scratch_shapes=[pltpu.VMEM((2, n, cw), f32), pltpu.VMEM((2, n, cw), f32)],
        compiler_params=_cp("arbitrary", "arbitrary", vmem=60 * 1024 * 1024),
        name="hyena_long_conv",
    )(x, tabs["m2"], tabs["g4"], tabs["g6"], tabs["m8"], khr, khi)


def _hygate_kernel(c_ref, y_ref, g_ref, b_ref, o_ref):
    y = y_ref[...]
    o_ref[...] = (g_ref[...] * (c_ref[...] + b_ref[...] * y)).astype(o_ref.dtype)


def _hyena_gate(conv, y, ycol0, zc, gcol0, bias, out_dtype):
    tm, tn = 512, 512
    nrows = conv.shape[0]
    return pl.pallas_call(
        _hygate_kernel,
        grid=(nrows // tm, HY_WIDTH // tn),
        in_specs=[pl.BlockSpec((tm, tn), lambda i, j: (i, j)),
                  pl.BlockSpec((tm, tn), lambda i, j: (i, ycol0 // tn + j)),
                  pl.BlockSpec((tm, tn), lambda i, j: (i, gcol0 // tn + j)),
                  pl.BlockSpec((1, tn), lambda i, j: (0, j))],
        out_specs=pl.BlockSpec((tm, tn), lambda i, j: (i, j)),
        out_shape=jax.ShapeDtypeStruct((nrows, HY_WIDTH), out_dtype),
        compiler_params=_cp("arbitrary", "arbitrary"),
        name="hyena_gate",
    )(conv, y, zc, bias.reshape(1, HY_WIDTH))


def _short_dft_tables(seq_len):
    n = 2 * seq_len
    k = jnp.arange(n, dtype=jnp.int32)
    ang = (2.0 * math.pi / n) * ((k[:, None] * k[None, :]) % n).astype(f32)
    c, s = jnp.cos(ang), jnp.sin(ang)
    mx = _pass3(_stack_complex(c[:, :seq_len], -s[:, :seq_len]))
    mf = _pass3(jnp.concatenate([c, -s], axis=0))
    mi = _pass3(_stack_complex(c[:seq_len, :] / n, s[:seq_len, :] / n))
    return dict(mx=mx, mf=mf, mi=mi)


def _short_fft_kernel(x_ref, mf_ref, kr_ref, ki_ref):
    n = x_ref.shape[0]
    z = jnp.dot(mf_ref[...], _rhs3(x_ref[...]), preferred_element_type=f32)
    kr_ref[...] = z[:n]
    ki_ref[...] = z[n:]


def _short_filter_fft(kern, tabs):
    n, nc = kern.shape
    tn = 256
    spec = pl.BlockSpec((n, tn), lambda t: (0, t))
    return pl.pallas_call(
        _short_fft_kernel,
        grid=(nc // tn,),
        in_specs=[spec, pl.BlockSpec(tabs["mf"].shape, lambda t: (0, 0))],
        out_specs=[spec, spec],
        out_shape=[jax.ShapeDtypeStruct((n, nc), f32)] * 2,
        compiler_params=_cp("arbitrary"),
        name="hyena_short_filter_fft",
    )(kern, tabs["mf"])


def _short_conv_fft_kernel(x_ref, mx_ref, mi_ref, kr_ref, ki_ref, o_ref):
    seq_len = x_ref.shape[1]
    cw = x_ref.shape[2]
    n = 2 * seq_len
    xr = jnp.concatenate([x_ref[0], x_ref[2]], axis=1)
    xi = jnp.concatenate([x_ref[1], x_ref[3]], axis=1)
    z = jnp.dot(mx_ref[...], _rhs3(jnp.concatenate([xr, xi], axis=0)), preferred_element_type=f32)
    zr, zi = z[:n], z[n:]
    kr = jnp.concatenate([kr_ref[...]] * 2, axis=1)
    ki = jnp.concatenate([ki_ref[...]] * 2, axis=1)
    w = jnp.concatenate([zr * kr - zi * ki, zr * ki + zi * kr], axis=0)
    y = jnp.dot(mi_ref[...], _rhs3(w), preferred_element_type=f32)
    o_ref[0] = y[:seq_len, :cw]
    o_ref[2] = y[:seq_len, cw:]
    o_ref[1] = y[seq_len:, :cw]
    o_ref[3] = y[seq_len:, cw:]


def _short_long_conv(x, col0, khr, khi, order, tabs):
    seq_len = x.shape[1]
    n = 2 * seq_len
    cw = 128
    nt = HY_WIDTH // cw
    kspec = pl.BlockSpec((n, cw), lambda t: (0, order * nt + t))
    const = lambda a: pl.BlockSpec(a.shape, lambda t: (0,) * a.ndim)
    return pl.pallas_call(
        _short_conv_fft_kernel,
        grid=(nt,),
        in_specs=[pl.BlockSpec((BATCH, seq_len, cw), lambda t: (0, 0, col0 // cw + t)),
                  const(tabs["mx"]), const(tabs["mi"]), kspec, kspec],
        out_specs=pl.BlockSpec((BATCH, seq_len, cw), lambda t: (0, 0, t)),
        out_shape=jax.ShapeDtypeStruct((BATCH, seq_len, HY_WIDTH), f32),
        compiler_params=_cp("arbitrary"),
        name="hyena_short_long_conv",
    )(x, tabs["mx"], tabs["mi"], khr, khi)


def _hyena_branch(z, row0, seq_len, tabs, short_w, short_b, w1, b1, w2, b2, w3, freq, bias):
    two_stage = "g4" in tabs
    conv = _hyena_conv if two_stage else _short_long_conv
    nrows = BATCH * seq_len
    zc = _short_conv(z, short_w, short_b, row0, nrows, seq_len)
    kern = _hyena_filter(seq_len, w1, b1, w2, b2, w3, freq)
    khr, khi = (_hyena_filter_fft if two_stage else _short_filter_fft)(kern, tabs)
    zc3 = zc.reshape(BATCH, seq_len, 3 * HY_WIDTH)
    c1 = conv(zc3, 0, khr, khi, 0, tabs).reshape(nrows, HY_WIDTH)
    y1 = _hyena_gate(c1, zc, 0, zc, HY_WIDTH, bias[0], f32)
    c2 = conv(y1.reshape(BATCH, seq_len, HY_WIDTH), 0, khr, khi, 1, tabs).reshape(nrows, HY_WIDTH)
    return _hyena_gate(c2, y1, 0, zc, 2 * HY_WIDTH, bias[1], bf16)


def _rope_tables():
    half = ATT_HD // 2
    nf = half // 2
    inv = ROPE_BASE ** (-jnp.arange(nf, dtype=f32) / nf)
    pos = jnp.arange(SEQ)
    rows = (pos // GRID_W).astype(f32)[:, None] * inv[None, :]
    cols = (pos % GRID_W).astype(f32)[:, None] * inv[None, :]
    zero = jnp.zeros_like(rows)
    cos = jnp.concatenate([jnp.cos(rows)] * 2 + [jnp.cos(cols)] * 2, axis=-1)
    sin_up = jnp.concatenate([-jnp.sin(rows), zero, -jnp.sin(cols), zero], axis=-1)
    sin_dn = jnp.concatenate([zero, jnp.sin(rows), zero, jnp.sin(cols)], axis=-1)
    return cos, sin_up, sin_dn


def _rope_kernel(q_ref, k_ref, cos_ref, su_ref, sd_ref, qo_ref, ko_ref):
    cos, su, sd = cos_ref[...], su_ref[...], sd_ref[...]

    def rot(x):
        return x * cos + pltpu.roll(x, 96, 1) * su + pltpu.roll(x, 32, 1) * sd

    for h in range(ATT_HEADS):
        s = slice(h * ATT_HD, (h + 1) * ATT_HD)
        qo_ref[:, s] = rot(q_ref[:, s]).astype(qo_ref.dtype)
    for h in range(ATT_KV_HEADS):
        s = slice(h * ATT_HD, (h + 1) * ATT_HD)
        ko_ref[:, s] = rot(k_ref[:, s]).astype(ko_ref.dtype)


def _rope(z, tables):
    tm = 512
    nq = ATT_HEADS * ATT_HD
    nk = ATT_KV_HEADS * ATT_HD
    tab = pl.BlockSpec((tm, ATT_HD), lambda i: (i % (SEQ // tm), 0))
    return pl.pallas_call(
        _rope_kernel,
        grid=(NLAT // tm,),
        in_specs=[pl.BlockSpec((tm, nq), lambda i: (i, Z_Q // nq)),
                  pl.BlockSpec((tm, nk), lambda i: (i, Z_K // nk)),
                  tab, tab, tab],
        out_specs=[pl.BlockSpec((tm, nq), lambda i: (i, 0)),
                   pl.BlockSpec((tm, nk), lambda i: (i, 0))],
        out_shape=[jax.ShapeDtypeStruct((NLAT, nq), bf16), jax.ShapeDtypeStruct((NLAT, nk), bf16)],
        compiler_params=_cp("arbitrary"),
        name="rope",
    )(z, z, *tables)


_ATT_SCALE = ATT_HD ** -0.5
_NEG = float(np.finfo(np.float32).min)
_NT = (((1,), (1,)), ((), ()))


def _sink_column(sink_ref, h, rows):
    rg = lax.broadcasted_iota(jnp.int32, (rows * ATT_GROUP, 1), 0) // rows
    col = jnp.full((rows * ATT_GROUP, 1), sink_ref[h * ATT_GROUP + ATT_GROUP - 1], f32)
    for g in range(ATT_GROUP - 2, -1, -1):
        col = jnp.where(rg == g, sink_ref[h * ATT_GROUP + g], col)
    return col


def _attn_kernel(sink_ref, q_ref, kp_ref, kc_ref, kn_ref, vp_ref, vc_ref, vn_ref, kx_ref, vx_ref, o_ref):
    i = pl.program_id(1)
    nb = pl.num_programs(1)
    blk = ATT_BLOCK
    r = lax.broadcasted_iota(jnp.int32, (ATT_GROUP * blk, 3 * blk), 0) % blk
    c = lax.broadcasted_iota(jnp.int32, (ATT_GROUP * blk, 3 * blk), 1)
    lo = jnp.where(i > 0, 0, blk)
    hi = jnp.where(i < nb - 1, 3 * blk, 2 * blk)
    valid = (c >= r) & (c <= r + 2 * ATT_WINDOW) & (c >= lo) & (c < hi)
    for h in range(ATT_KV_HEADS):
        hs = slice(h * ATT_HD, (h + 1) * ATT_HD)
        k_win = jnp.concatenate([kp_ref[:, hs], kc_ref[:, hs], kn_ref[:, hs]], axis=0)
        v_win = jnp.concatenate([vp_ref[:, hs], vc_ref[:, hs], vn_ref[:, hs]], axis=0).astype(bf16)
        k_ctx = kx_ref[:, hs].astype(bf16)
        v_ctx = vx_ref[:, hs].astype(bf16)
        q = jnp.concatenate([q_ref[:, (h * ATT_GROUP + g) * ATT_HD:(h * ATT_GROUP + g + 1) * ATT_HD]
                             for g in range(ATT_GROUP)], axis=0)
        s_win = lax.dot_general(q, k_win, _NT, preferred_element_type=f32) * _ATT_SCALE
        s_win = jnp.where(valid, s_win, _NEG)
        s_ctx = lax.dot_general(q, k_ctx, _NT, preferred_element_type=f32) * _ATT_SCALE
        sink = _sink_column(sink_ref, h, blk)
        m = jnp.maximum(jnp.maximum(jnp.max(s_win, axis=-1, keepdims=True),
                                    jnp.max(s_ctx, axis=-1, keepdims=True)), sink)
        p_win = jnp.exp(s_win - m)
        p_ctx = jnp.exp(s_ctx - m)
        den = (jnp.sum(p_win, axis=-1, keepdims=True) + jnp.sum(p_ctx, axis=-1, keepdims=True)
               + jnp.exp(sink - m))
        o = (jnp.dot(p_win.astype(bf16), v_win, preferred_element_type=f32)
             + jnp.dot(p_ctx.astype(bf16), v_ctx, preferred_element_type=f32)) / den
        for g in range(ATT_GROUP):
            cs = slice((h * ATT_GROUP + g) * ATT_HD, (h * ATT_GROUP + g + 1) * ATT_HD)
            o_ref[:, cs] = o[g * blk:(g + 1) * blk].astype(o_ref.dtype)


def _window_attention(qr, kr, z, sink):
    blk = ATT_BLOCK
    nb = SEQ // blk
    nkv = ATT_KV_HEADS * ATT_HD
    cx = NLAT // CTX_LEN

    def krow(off):
        return lambda b, i: (b * nb + jnp.clip(i + off, 0, nb - 1), 0)

    def vrow(off):
        return lambda b, i: (b * nb + jnp.clip(i + off, 0, nb - 1), Z_V // nkv)

    return pl.pallas_call(
        _attn_kernel,
        grid=(BATCH, nb),
        in_specs=[pl.BlockSpec(memory_space=pltpu.SMEM),
                  pl.BlockSpec((blk, ATT_HEADS * ATT_HD), lambda b, i: (b * nb + i, 0)),
                  pl.BlockSpec((blk, nkv), krow(-1)),
                  pl.BlockSpec((blk, nkv), krow(0)),
                  pl.BlockSpec((blk, nkv), krow(1)),
                  pl.BlockSpec((blk, nkv), vrow(-1)),
                  pl.BlockSpec((blk, nkv), vrow(0)),
                  pl.BlockSpec((blk, nkv), vrow(1)),
                  pl.BlockSpec((CTX_LEN, nkv), lambda b, i: (cx + b, Z_K // nkv)),
                  pl.BlockSpec((CTX_LEN, nkv), lambda b, i: (cx + b, Z_V // nkv))],
        out_specs=pl.BlockSpec((blk, ATT_HEADS * ATT_HD), lambda b, i: (b * nb + i, 0)),
        out_shape=jax.ShapeDtypeStruct((NLAT, ATT_HEADS * ATT_HD), bf16),
        compiler_params=_cp("arbitrary", "arbitrary"),
        name="window_attention",
    )(sink, qr, kr, kr, kr, z, z, z, z, z)


def _ctx_attn_kernel(sink_ref, q_ref, k_ref, v_ref, o_ref):
    for h in range(ATT_KV_HEADS):
        hs = slice(h * ATT_HD, (h + 1) * ATT_HD)
        k = k_ref[:, hs].astype(bf16)
        v = v_ref[:, hs].astype(bf16)
        q = jnp.concatenate([q_ref[:, (h * ATT_GROUP + g) * ATT_HD:(h * ATT_GROUP + g + 1) * ATT_HD]
                             for g in range(ATT_GROUP)], axis=0).astype(bf16)
        s = lax.dot_general(q, k, _NT, preferred_element_type=f32) * _ATT_SCALE
        sink = _sink_column(sink_ref, h, CTX_LEN)
        m = jnp.maximum(jnp.max(s, axis=-1, keepdims=True), sink)
        p = jnp.exp(s - m)
        den = jnp.sum(p, axis=-1, keepdims=True) + jnp.exp(sink - m)
        o = jnp.dot(p.astype(bf16), v, preferred_element_type=f32) / den
        for g in range(ATT_GROUP):
            cs = slice((h * ATT_GROUP + g) * ATT_HD, (h * ATT_GROUP + g + 1) * ATT_HD)
            o_ref[:, cs] = o[g * CTX_LEN:(g + 1) * CTX_LEN].astype(o_ref.dtype)


def _context_attention(z, sink):
    nq = ATT_HEADS * ATT_HD
    nkv = ATT_KV_HEADS * ATT_HD
    cx = NLAT // CTX_LEN
    return pl.pallas_call(
        _ctx_attn_kernel,
        grid=(BATCH,),
        in_specs=[pl.BlockSpec(memory_space=pltpu.SMEM),
                  pl.BlockSpec((CTX_LEN, nq), lambda b: (cx + b, Z_Q // nq)),
                  pl.BlockSpec((CTX_LEN, nkv), lambda b: (cx + b, Z_K // nkv)),
                  pl.BlockSpec((CTX_LEN, nkv), lambda b: (cx + b, Z_V // nkv))],
        out_specs=pl.BlockSpec((CTX_LEN, nq), lambda b: (b, 0)),
        out_shape=jax.ShapeDtypeStruct((NCTX, nq), bf16),
        compiler_params=_cp("arbitrary"),
        name="context_attention",
    )(sink, z, z, z)


_ML_SCALE = ML_QK ** -0.5


def _split3(x):
    x1 = x.astype(bf16)
    r1 = x - x1.astype(f32)
    x2 = r1.astype(bf16)
    x3 = (r1 - x2.astype(f32)).astype(bf16)
    return x1, x2, x3


def _mlstm_kernel(q_ref, k_ref, v_ref, g_ref, gb_ref, o_ref, ct_ref, n_ref, m_ref):
    d = pl.program_id(1)
    c = pl.program_id(2)
    ch = ML_CHUNK

    @pl.when(c == 0)
    def _():
        ct_ref[...] = jnp.zeros_like(ct_ref)
        n_ref[...] = jnp.zeros_like(n_ref)
        m_ref[...] = jnp.zeros_like(m_ref)

    g = g_ref[...] + gb_ref[...]
    row = lax.broadcasted_iota(jnp.int32, (ch, ch), 0)
    col = lax.broadcasted_iota(jnp.int32, (ch, ch), 1)
    tri = (row - col) * (1 - 2 * d) >= 0
    tri_b = tri.astype(f32).astype(bf16)
    lf = jax.nn.log_sigmoid(g)
    l1, l2, l3 = _split3(lf)
    bcol = (jnp.dot(tri_b, l1, preferred_element_type=f32)
            + jnp.dot(tri_b, l2, preferred_element_type=f32)
            + jnp.dot(tri_b, l3, preferred_element_type=f32))
    bend = jnp.where(d == 0, bcol[ch - 1:ch, :], bcol[0:1, :])
    g_t = g.T
    b_t = bcol.T

    for h in range(ML_HEADS):
        qf = q_ref[:, h * ML_QK:(h + 1) * ML_QK] * _ML_SCALE
        kf = k_ref[:, h * ML_QK:(h + 1) * ML_QK]
        vf = v_ref[:, h * ML_V:(h + 1) * ML_V]
        q = qf.astype(bf16)
        k = kf.astype(bf16)
        li_c = g[:, h:h + 1]
        b_c = bcol[:, ML_HEADS + h:ML_HEADS + h + 1]
        li_r = g_t[h:h + 1, :]
        b_r = b_t[ML_HEADS + h:ML_HEADS + h + 1, :]
        m_prev = m_ref[h][:, 0:1]
        dmat = jnp.where(tri, b_c - b_r + li_r, -jnp.inf)
        inter = b_c + m_prev
        m_t = jnp.maximum(inter, jnp.max(dmat, axis=-1, keepdims=True))
        w_intra = jnp.exp(dmat - m_t)
        w_inter = jnp.exp(inter - m_t)
        s = lax.dot_general(q, k, _NT, preferred_element_type=f32) * w_intra
        qc = jnp.dot(q, ct_ref[h].astype(bf16), preferred_element_type=f32)
        num = jnp.dot(s.astype(bf16), vf.astype(bf16), preferred_element_type=f32) + w_inter * qc
        den = (jnp.sum(s, axis=-1, keepdims=True)
               + w_inter * jnp.sum(qf * n_ref[h], axis=-1, keepdims=True))
        o_ref[:, h * ML_V:(h + 1) * ML_V] = num / jnp.maximum(jnp.abs(den), jnp.exp(-m_t))

        b_e = bend[:, ML_HEADS + h:ML_HEADS + h + 1]
        g_c = b_e - b_c + li_c
        m_new = jnp.maximum(b_e + m_prev, jnp.max(g_c, axis=0, keepdims=True))
        w_s = jnp.exp(g_c - m_new)
        w_c = jnp.exp(b_e + m_prev - m_new)
        vw = (vf * w_s).astype(bf16)
        ct_ref[h] = w_c * ct_ref[h] + jnp.dot(kf.T.astype(bf16), vw, preferred_element_type=f32)
        n_ref[h] = w_c * n_ref[h] + jnp.sum(kf * w_s, axis=0, keepdims=True)
        m_ref[h] = jnp.broadcast_to(m_new, (1, 128))


def _mlstm(z, gate_b):
    ch = ML_CHUNK
    ncl = SEQ // ch
    nsteps = ncl + CTX_LEN // ch
    assert CTX_LEN == ch

    def rt(b, d, c):
        lat = b * ncl + jnp.where(d == 0, c - 1, ncl - c)
        return jnp.where(c == 0, NLAT // ch + b, lat)

    nq = ML_HEADS * ML_QK
    nv = ML_HEADS * ML_V
    return pl.pallas_call(
        _mlstm_kernel,
        grid=(BATCH, 2, nsteps),
        in_specs=[pl.BlockSpec((ch, nq), lambda b, d, c: (rt(b, d, c), Z_MQ // nq)),
                  pl.BlockSpec((ch, nq), lambda b, d, c: (rt(b, d, c), Z_MK // nq)),
                  pl.BlockSpec((ch, nv), lambda b, d, c: (rt(b, d, c), Z_MV // nv)),
                  pl.BlockSpec((ch, 128), lambda b, d, c: (rt(b, d, c), Z_GATE // 128 + d)),
                  pl.BlockSpec((None, 1, 128), lambda b, d, c: (d, 0, 0))],
        out_specs=pl.BlockSpec((None, ch, nv), lambda b, d, c: (d, rt(b, d, c), 0)),
        out_shape=jax.ShapeDtypeStruct((2, R, nv), f32),
        scratch_shapes=[pltpu.VMEM((ML_HEADS, ML_QK, ML_V), f32),
                        pltpu.VMEM((ML_HEADS, 1, ML_QK), f32),
                        pltpu.VMEM((ML_HEADS, 1, 128), f32)],
        compiler_params=_cp("arbitrary", "arbitrary", "arbitrary"),
        name="mlstm",
    )(z, z, z, z, gate_b)


def _mlstm_out_kernel(h_ref, zo_ref, g_ref, o_ref):
    for h in range(ML_HEADS):
        s = slice(h * ML_V, (h + 1) * ML_V)
        x = h_ref[0, :, s] + h_ref[1, :, s]
        xn = x * lax.rsqrt(jnp.mean(x * x, axis=-1, keepdims=True) + EPS) * g_ref[:, s]
        o_ref[:, s] = (xn * jax.nn.sigmoid(zo_ref[:, s])).astype(o_ref.dtype)


def _mlstm_out(hh, z, g, nrows):
    tm = 512
    nv = ML_HEADS * ML_V
    return pl.pallas_call(
        _mlstm_out_kernel,
        grid=(nrows // tm,),
        in_specs=[pl.BlockSpec((2, tm, nv), lambda i: (0, i, 0)),
                  pl.BlockSpec((tm, nv), lambda i: (i, Z_MO // nv)),
                  pl.BlockSpec((1, nv), lambda i: (0, 0))],
        out_specs=pl.BlockSpec((tm, nv), lambda i: (i, 0)),
        out_shape=jax.ShapeDtypeStruct((nrows, nv), bf16),
        compiler_params=_cp("arbitrary"),
        name="mlstm_out",
    )(hh, z, g.reshape(1, nv))


def _router_kernel(f_ref, w_ref, idx_ref, p_ref):
    logits = jnp.dot(f_ref[...].astype(bf16), w_ref[...], preferred_element_type=f32)
    lane = lax.broadcasted_iota(jnp.int32, logits.shape, 1).astype(f32)
    logits = jnp.where(lane < N_EXPERTS, logits, -jnp.inf)
    v1 = jnp.max(logits, axis=-1, keepdims=True)
    i1 = jnp.min(jnp.where(logits == v1, lane, 128.0), axis=-1, keepdims=True)
    rest = jnp.where(lane == i1, -jnp.inf, logits)
    v2 = jnp.max(rest, axis=-1, keepdims=True)
    i2 = jnp.min(jnp.where(rest == v2, lane, 128.0), axis=-1, keepdims=True)
    e = jnp.exp(v2 - v1)
    p1 = 1.0 / (1.0 + e)
    p2 = e / (1.0 + e)
    idx_ref[...] = jnp.where(lane == 0, i1, jnp.where(lane == 1, i2, 0.0)).astype(jnp.int32)
    p_ref[...] = jnp.where(lane == 0, p1, jnp.where(lane == 1, p2, 0.0))


def _router(f, w_router):
    tm = 512
    w = jnp.pad(w_router, ((0, 0), (0, 128 - N_EXPERTS))).astype(bf16)
    return pl.pallas_call(
        _router_kernel,
        grid=(NLAT // tm,),
        in_specs=[pl.BlockSpec((tm, D_MODEL), lambda i: (i, 0)),
                  pl.BlockSpec((D_MODEL, 128), lambda i: (0, 0))],
        out_specs=[pl.BlockSpec((tm, 128), lambda i: (i, 0)),
                   pl.BlockSpec((tm, 128), lambda i: (i, 0))],
        out_shape=[jax.ShapeDtypeStruct((NLAT, 128), jnp.int32),
                   jax.ShapeDtypeStruct((NLAT, 128), f32)],
        compiler_params=_cp("arbitrary"),
        name="router",
    )(f, w)


def _gather_kernel(tok_ref, src_ref, o_ref, buf, sem):
    base = pl.program_id(0) * MOE_TILE

    def copy(r):
        return pltpu.make_async_copy(src_ref.at[pl.ds(tok_ref[base + r], 1)], buf.at[pl.ds(r, 1)], sem)

    def start(r, carry):
        copy(r).start()
        return carry

    def wait(r, carry):
        copy(r).wait()
        return carry

    lax.fori_loop(0, MOE_TILE, start, 0)
    lax.fori_loop(0, MOE_TILE, wait, 0)
    o_ref[...] = buf[...].astype(o_ref.dtype)


def _gather_rows(src, rows, n_out, out_dtype):
    width = src.shape[1]
    return pl.pallas_call(
        _gather_kernel,
        grid_spec=pltpu.PrefetchScalarGridSpec(
            num_scalar_prefetch=1,
            grid=(n_out // MOE_TILE,),
            in_specs=[pl.BlockSpec(memory_space=pl.ANY)],
            out_specs=pl.BlockSpec((MOE_TILE, width), lambda i, tok: (i, 0)),
            scratch_shapes=[pltpu.VMEM((MOE_TILE, width), src.dtype), pltpu.SemaphoreType.DMA(())]),
        out_shape=jax.ShapeDtypeStruct((n_out, width), out_dtype),
        compiler_params=_cp("arbitrary"),
        name="gather_rows",
    )(rows, src)


def _moe_up_kernel(be_ref, nu_ref, a_ref, wg_ref, wu_ref, o_ref, wg_bf, wu_bf):
    i = pl.program_id(1)
    used = i < nu_ref[0]
    fresh = jnp.logical_or(i == 0, be_ref[i] != be_ref[jnp.maximum(i - 1, 0)])

    @pl.when(jnp.logical_and(used, fresh))
    def _():
        wg_bf[...] = wg_ref[...].astype(bf16)
        wu_bf[...] = wu_ref[...].astype(bf16)

    @pl.when(used)
    def _():
        a = a_ref[...]
        g = jnp.dot(a, wg_bf[...], preferred_element_type=f32)
        u = jnp.dot(a, wu_bf[...], preferred_element_type=f32)
        o_ref[...] = (g * jax.nn.sigmoid(g) * u).astype(o_ref.dtype)

    @pl.when(i >= nu_ref[0])
    def _():
        o_ref[...] = jnp.zeros_like(o_ref)


def _moe_up(xs, wg, wu, blk_expert, n_used):
    tn = 512
    nblk = xs.shape[0] // MOE_TILE

    def row(j, i, be, nu):
        return (jnp.minimum(i, nu[0] - 1), 0)

    def wmap(j, i, be, nu):
        return (be[jnp.minimum(i, nu[0] - 1)], 0, j)

    return pl.pallas_call(
        _moe_up_kernel,
        grid_spec=pltpu.PrefetchScalarGridSpec(
            num_scalar_prefetch=2,
            grid=(FFN_EXPERT // tn, nblk),
            in_specs=[pl.BlockSpec((MOE_TILE, D_MODEL), row),
                      pl.BlockSpec((None, D_MODEL, tn), wmap),
                      pl.BlockSpec((None, D_MODEL, tn), wmap)],
            out_specs=pl.BlockSpec((MOE_TILE, tn), lambda j, i, be, nu: (i, j)),
            scratch_shapes=[pltpu.VMEM((D_MODEL, tn), bf16), pltpu.VMEM((D_MODEL, tn), bf16)]),
        out_shape=jax.ShapeDtypeStruct((xs.shape[0], FFN_EXPERT), bf16),
        compiler_params=_cp("arbitrary", "arbitrary"),
        name="moe_up",
    )(blk_expert, n_used, xs, wg, wu)


def _moe_down_kernel(be_ref, nu_ref, a_ref, w_ref, o_ref):
    i = pl.program_id(1)

    @pl.when(i < nu_ref[0])
    def _():
        o_ref[...] = jnp.dot(a_ref[...], w_ref[...], preferred_element_type=f32)

    @pl.when(i >= nu_ref[0])
    def _():
        o_ref[...] = jnp.zeros_like(o_ref)


def _moe_down(hs, wd, blk_expert, n_used):
    tn = 512
    nblk = hs.shape[0] // MOE_TILE

    def row(j, i, be, nu):
        return (jnp.minimum(i, nu[0] - 1), 0)

    def wmap(j, i, be, nu):
        return (be[jnp.minimum(i, nu[0] - 1)], 0, j)

    return pl.pallas_call(
        _moe_down_kernel,
        grid_spec=pltpu.PrefetchScalarGridSpec(
            num_scalar_prefetch=2,
            grid=(D_MODEL // tn, nblk),
            in_specs=[pl.BlockSpec((MOE_TILE, FFN_EXPERT), row),
                      pl.BlockSpec((None, FFN_EXPERT, tn), wmap)],
            out_specs=pl.BlockSpec((MOE_TILE, tn), lambda j, i, be, nu: (i, j))),
        out_shape=jax.ShapeDtypeStruct((hs.shape[0], D_MODEL), f32),
        compiler_params=_cp("arbitrary", "arbitrary"),
        name="moe_down",
    )(blk_expert, n_used, hs, wd)


_COMBINE_TM = 256


def _combine_kernel(slot_ref, yb_ref, x_ref, p_ref, g2_ref, fg_ref, o_ref, buf, sem):
    i = pl.program_id(0)
    cur = i % 2

    def copy(step, b, r, k):
        s = slot_ref[(step * _COMBINE_TM + r) * TOP_K + k]
        return pltpu.make_async_copy(yb_ref.at[pl.ds(s, 1)], buf.at[b, k, pl.ds(r, 1)], sem.at[b])

    def fetch(step, b):
        def start(r, carry):
            copy(step, b, r, 0).start()
            copy(step, b, r, 1).start()
            return carry
        lax.fori_loop(0, _COMBINE_TM, start, 0)

    @pl.when(i == 0)
    def _():
        fetch(0, 0)

    @pl.when(i + 1 < pl.num_programs(0))
    def _():
        fetch(i + 1, 1 - cur)

    def wait(r, carry):
        copy(i, cur, r, 0).wait()
        copy(i, cur, r, 1).wait()
        return carry

    lax.fori_loop(0, _COMBINE_TM, wait, 0)
    p = p_ref[...]
    y = buf[cur, 0] * p[:, 0:1] + buf[cur, 1] * p[:, 1:2]
    x = x_ref[...] + g2_ref[...] * y
    o_ref[...] = x * lax.rsqrt(jnp.mean(x * x, axis=-1, keepdims=True) + EPS) * fg_ref[...]


def _combine_final(slot, yb, xa, probs, modt, final_g):
    tm = _COMBINE_TM
    return pl.pallas_call(
        _combine_kernel,
        grid_spec=pltpu.PrefetchScalarGridSpec(
            num_scalar_prefetch=1,
            grid=(NLAT // tm,),
            in_specs=[pl.BlockSpec(memory_space=pl.ANY),
                      pl.BlockSpec((tm, D_MODEL), lambda i, s: (i, 0)),
                      pl.BlockSpec((tm, 128), lambda i, s: (i, 0)),
                      pl.BlockSpec((None, 1, D_MODEL), lambda i, s: (i // (SEQ // tm), 0, 5)),
                      pl.BlockSpec((1, D_MODEL), lambda i, s: (0, 0))],
            out_specs=pl.BlockSpec((tm, D_MODEL), lambda i, s: (i, 0)),
            scratch_shapes=[pltpu.VMEM((2, TOP_K, tm, D_MODEL), f32),
                            pltpu.SemaphoreType.DMA((2,))]),
        out_shape=jax.ShapeDtypeStruct((NLAT, D_MODEL), f32),
        compiler_params=_cp("arbitrary"),
        name="moe_combine_final_norm",
    )(slot, yb, xa, probs, modt, final_g.reshape(1, D_MODEL))


def _moe_routing(top_i):
    a = NLAT * TOP_K
    e_flat = top_i.reshape(a)
    onehot = (e_flat[:, None] == jnp.arange(N_EXPERTS)[None, :]).astype(jnp.int32)
    csum = jnp.cumsum(onehot, axis=0)
    rank = jnp.sum(onehot * csum, axis=1) - 1
    counts = csum[-1]
    padded = (counts + MOE_TILE - 1) // MOE_TILE * MOE_TILE
    pad_end = jnp.cumsum(padded)
    pad_start = pad_end - padded
    slot = (pad_start[e_flat] + rank).astype(jnp.int32)
    n_rows = a + N_EXPERTS * MOE_TILE
    nblk = n_rows // MOE_TILE
    slot_tok = jnp.zeros((n_rows,), jnp.int32).at[slot].set(jnp.arange(a, dtype=jnp.int32) // TOP_K)
    blk_expert = jnp.minimum(jnp.searchsorted(pad_end, jnp.arange(nblk) * MOE_TILE, side='right'),
                             N_EXPERTS - 1).astype(jnp.int32)
    n_used = (pad_end[-1:] // MOE_TILE).astype(jnp.int32)
    return slot, slot_tok, blk_expert, n_used, n_rows


def _in_proj_weight(w, gate_b):
    o = np.cumsum((0,) + (3072, 1024, 256, 256, 512, 512, 1024, 1024, 16, 6144))
    hy, q, k, v, mq, mk, mv, mo, gt, mg = [w[:, o[i]:o[i + 1]] for i in range(10)]
    pad = jnp.zeros((D_MODEL, 128 - 2 * ML_HEADS), w.dtype)
    gates = [jnp.concatenate([gt[:, 8 * d:8 * d + 8], pad], axis=1) for d in range(2)]
    wz = jnp.concatenate([hy, q, mv, mo, mg, k, v, mq, mk] + gates
                         + [jnp.zeros((D_MODEL, Z_COLS - Z_GATE - 256), w.dtype)], axis=1).astype(bf16)
    gb = jnp.pad(gate_b.reshape(2, 1, 2 * ML_HEADS), ((0, 0), (0, 0), (0, 128 - 2 * ML_HEADS)))
    return wz, gb


def kernel(x, c, ctx, c_ctx, w_mod, b_mod, norm_mix_g, norm_ffn_g, w_in, hy_short_w, hy_short_b, hy_w1, hy_b1, hy_w2, hy_b2, hy_w3, hy_freq, hy_bias, att_sink, ml_gate_b, ml_norm_g, w_branch, w_out, ffn_wg, ffn_wu, ffn_wd, moe_router, moe_wg, moe_wu, moe_wd, final_g):
    xa = jnp.concatenate([x.reshape(NLAT, D_MODEL), ctx.reshape(NCTX, D_MODEL)], axis=0)
    c_all = jnp.concatenate([c, c_ctx[None], jnp.zeros((8 - BATCH - 1, D_MODEL), f32)], axis=0)
    mod = _modulation(c_all, w_mod, b_mod)
    rope_tabs = _rope_tables()
    dft_tabs = _dft_tables(SEQ)
    ctx_tabs = _short_dft_tables(CTX_LEN)
    out = None
    for layer in range(DEPTH):
        last = layer == DEPTH - 1
        modt = mod[layer].reshape(8, 1, 6 * D_MODEL)
        n_mix = NLAT if last else R

        u = _normmod(xa, norm_mix_g[layer], modt, 0, R)
        wz, gate_b = _in_proj_weight(w_in[layer], ml_gate_b[layer])
        z = _in_proj(u, wz)

        hy = (hy_short_w[layer], hy_short_b[layer], hy_w1[layer], hy_b1[layer], hy_w2[layer],
              hy_b2[layer], hy_w3[layer], hy_freq[layer], hy_bias[layer])
        a_rows = _hyena_branch(z, 0, SEQ, dft_tabs, *hy)

        qr, kr = _rope(z, rope_tabs)
        b_rows = _window_attention(qr, kr, z, att_sink[layer])

        hh = _mlstm(z, gate_b)
        c_rows = _mlstm_out(hh, z, ml_norm_g[layer], n_mix)

        if not last:
            a_rows = jnp.concatenate([a_rows, _hyena_branch(z, NLAT, CTX_LEN, ctx_tabs, *hy)], axis=0)
            b_rows = jnp.concatenate([b_rows, _context_attention(z, att_sink[layer])], axis=0)

        ymid = _merge(a_rows, b_rows, c_rows, w_branch[layer].astype(bf16), z, n_mix)
        xa = _mm_resid(ymid, w_out[layer].astype(bf16), xa, modt, 2, n_mix)

        f = _normmod(xa, norm_ffn_g[layer], modt, 3, n_mix, bf16 if layer % 2 == 0 else f32)
        if layer % 2 == 0:
            e = layer // 2
            hmid = _ffn_up(f, ffn_wg[e].astype(bf16), ffn_wu[e].astype(bf16))
            xa = _mm_resid(hmid, ffn_wd[e].astype(bf16), xa, modt, 5, R)
        else:
            e = layer // 2
            top_i, probs = _router(f, moe_router[e])
            slot, slot_tok, blk_expert, n_used, n_rows = _moe_routing(top_i[:, :TOP_K])
            xs = _gather_rows(f, slot_tok, n_rows, bf16)
            hs = _moe_up(xs, moe_wg[e], moe_wu[e], blk_expert, n_used)
            yb = _moe_down(hs, moe_wd[e].astype(bf16), blk_expert, n_used)
            assert last
            out = _combine_final(slot, yb, xa, probs, modt, final_g)
    return out.reshape(BATCH, SEQ, D_MODEL)
```

```python
import functools
import math

import numpy as np
import jax
import jax.numpy as jnp
from jax import lax
from jax.experimental import pallas as pl
from jax.experimental.pallas import tpu as pltpu

f32 = jnp.float32
bf16 = jnp.bfloat16

D_MODEL = 2048
BATCH = 4
SEQ = 4096
DEPTH = 2
GRID_W = 64
CTX_LEN = 256
EPS = 1e-6

HY_WIDTH = D_MODEL // 2
HY_ORDER = 2
HY_BANDS = 16
HY_EMB = 2 * HY_BANDS + 1
HY_FFN = 64
HY_MIN_DECAY = -3.0701134573253943
HY_MAX_DECAY = -15.35056728662697
HY_N2 = 128
HY_SLABS = 4
HY_UNROLL = 8

ATT_HD = 128
ATT_HEADS = 8
ATT_KV_HEADS = 2
ATT_GROUP = 4
ATT_WINDOW = 128
ATT_BLOCK = 128
ROPE_BASE = 10000.0

ML_HEADS = 4
ML_V = 256
ML_QK = 128
ML_CHUNK = 256

N_BRANCH = 3
BRANCH_W = D_MODEL // 2
FFN_DENSE = 5632
N_EXPERTS = 8
TOP_K = 2
FFN_EXPERT = 7168
MOE_TILE = 512

NCTX = BATCH * CTX_LEN
NLAT = BATCH * SEQ
R = NCTX + NLAT

Z_HY = 0
Z_Q = 3072
Z_MV = 4096
Z_MO = 5120
Z_MERGE = 6144
Z_K = 12288
Z_V = 12544
Z_MQ = 12800
Z_MK = 13312
Z_GATE = 13824
Z_COLS = 14336

VMEM_LIMIT = 56 * 1024 * 1024


def _cp(*sem, vmem=VMEM_LIMIT):
    return pltpu.CompilerParams(dimension_semantics=sem, vmem_limit_bytes=vmem)


def _modrow(i, tm):
    return jnp.where(i >= NLAT // tm, BATCH, i // (SEQ // tm))


def _mod_kernel(c_ref, w_ref, b_ref, o_ref):
    c = c_ref[...]
    a = (c * jax.nn.sigmoid(c)).astype(bf16)
    o_ref[...] = jnp.dot(a, w_ref[...].astype(bf16), preferred_element_type=f32) + b_ref[...]


def _modulation(c_all, w_mod, b_mod):
    tn = 1024
    n = 6 * D_MODEL
    return pl.pallas_call(
        _mod_kernel,
        grid=(DEPTH, n // tn),
        in_specs=[pl.BlockSpec((8, D_MODEL), lambda l, j: (0, 0)),
                  pl.BlockSpec((None, D_MODEL, tn), lambda l, j: (l, 0, j)),
                  pl.BlockSpec((None, 1, tn), lambda l, j: (l, 0, j))],
        out_specs=pl.BlockSpec((None, 8, tn), lambda l, j: (l, 0, j)),
        out_shape=jax.ShapeDtypeStruct((DEPTH, 8, n), f32),
        compiler_params=_cp("arbitrary", "arbitrary"),
        name="modulation",
    )(c_all, w_mod, b_mod.reshape(DEPTH, 1, n))


def _normmod_kernel(x_ref, g_ref, sh_ref, sc_ref, o_ref):
    x = x_ref[...]
    y = x * lax.rsqrt(jnp.mean(x * x, axis=-1, keepdims=True) + EPS) * g_ref[...]
    o_ref[...] = (y * (1.0 + sc_ref[...]) + sh_ref[...]).astype(o_ref.dtype)


def _normmod(xa, g, modt, which, nrows, out_dtype=bf16):
    tm = 512
    return pl.pallas_call(
        _normmod_kernel,
        grid=(nrows // tm,),
        in_specs=[pl.BlockSpec((tm, D_MODEL), lambda i: (i, 0)),
                  pl.BlockSpec((1, D_MODEL), lambda i: (0, 0)),
                  pl.BlockSpec((None, 1, D_MODEL), lambda i: (_modrow(i, tm), 0, which)),
                  pl.BlockSpec((None, 1, D_MODEL), lambda i: (_modrow(i, tm), 0, which + 1))],
        out_specs=pl.BlockSpec((tm, D_MODEL), lambda i: (i, 0)),
        out_shape=jax.ShapeDtypeStruct((nrows, D_MODEL), out_dtype),
        compiler_params=_cp("arbitrary"),
        name="normmod",
    )(xa, g.reshape(1, D_MODEL), modt, modt)


def _mm_kernel(a_ref, w_ref, o_ref):
    o_ref[...] = jnp.dot(a_ref[...], w_ref[...], preferred_element_type=f32).astype(o_ref.dtype)


def _in_proj(u, w):
    tm, tn = 1024, 1024
    return pl.pallas_call(
        _mm_kernel,
        grid=(Z_COLS // tn, R // tm),
        in_specs=[pl.BlockSpec((tm, D_MODEL), lambda j, i: (i, 0)),
                  pl.BlockSpec((D_MODEL, tn), lambda j, i: (0, j))],
        out_specs=pl.BlockSpec((tm, tn), lambda j, i: (i, j)),
        out_shape=jax.ShapeDtypeStruct((R, Z_COLS), f32),
        compiler_params=_cp("arbitrary", "arbitrary"),
        name="in_proj",
    )(u, w)


def _mm_resid_kernel(a_ref, w_ref, x_ref, g_ref, o_ref):
    y = jnp.dot(a_ref[...], w_ref[...], preferred_element_type=f32)
    o_ref[...] = x_ref[...] + g_ref[...] * y


def _mm_resid(a, w, xa, modt, which, nrows):
    k = a.shape[1]
    tm, tn = (1024, 1024) if k <= D_MODEL else (512, 1024)
    nj = D_MODEL // tn
    return pl.pallas_call(
        _mm_resid_kernel,
        grid=(nj, nrows // tm),
        in_specs=[pl.BlockSpec((tm, k), lambda j, i: (i, 0)),
                  pl.BlockSpec((k, tn), lambda j, i: (0, j)),
                  pl.BlockSpec((tm, tn), lambda j, i: (i, j)),
                  pl.BlockSpec((None, 1, tn), lambda j, i: (_modrow(i, tm), 0, which * nj + j))],
        out_specs=pl.BlockSpec((tm, tn), lambda j, i: (i, j)),
        out_shape=jax.ShapeDtypeStruct((nrows, D_MODEL), f32),
        compiler_params=_cp("arbitrary", "arbitrary"),
        name="mm_resid",
    )(a, w, xa, modt)


def _swiglu_kernel(a_ref, wg_ref, wu_ref, o_ref):
    a = a_ref[...]
    g = jnp.dot(a, wg_ref[...], preferred_element_type=f32)
    u = jnp.dot(a, wu_ref[...], preferred_element_type=f32)
    o_ref[...] = (g * jax.nn.sigmoid(g) * u).astype(o_ref.dtype)


def _ffn_up(f, wg, wu):
    tm, tn = 1024, 512
    n = wg.shape[1]
    return pl.pallas_call(
        _swiglu_kernel,
        grid=(n // tn, R // tm),
        in_specs=[pl.BlockSpec((tm, D_MODEL), lambda j, i: (i, 0)),
                  pl.BlockSpec((D_MODEL, tn), lambda j, i: (0, j)),
                  pl.BlockSpec((D_MODEL, tn), lambda j, i: (0, j))],
        out_specs=pl.BlockSpec((tm, tn), lambda j, i: (i, j)),
        out_shape=jax.ShapeDtypeStruct((R, n), bf16),
        compiler_params=_cp("arbitrary", "arbitrary"),
        name="ffn_up",
    )(f, wg, wu)


def _merge_kernel(a_ref, b_ref, c_ref, w_ref, g0_ref, g1_ref, g2_ref, o_ref):
    y = jax.nn.sigmoid(g0_ref[...]) * jnp.dot(a_ref[...], w_ref[0], preferred_element_type=f32)
    y += jax.nn.sigmoid(g1_ref[...]) * jnp.dot(b_ref[...], w_ref[1], preferred_element_type=f32)
    y += jax.nn.sigmoid(g2_ref[...]) * jnp.dot(c_ref[...], w_ref[2], preferred_element_type=f32)
    o_ref[...] = y.astype(o_ref.dtype)


def _merge(a, b, c, wb, z, nrows):
    tm, tn = 512, 1024
    nj = D_MODEL // tn
    act = pl.BlockSpec((tm, BRANCH_W), lambda j, i: (i, 0))

    def gate(br):
        return pl.BlockSpec((tm, tn), lambda j, i: (i, (Z_MERGE + br * D_MODEL) // tn + j))

    return pl.pallas_call(
        _merge_kernel,
        grid=(nj, nrows // tm),
        in_specs=[act, act, act,
                  pl.BlockSpec((N_BRANCH, BRANCH_W, tn), lambda j, i: (0, 0, j)),
                  gate(0), gate(1), gate(2)],
        out_specs=pl.BlockSpec((tm, tn), lambda j, i: (i, j)),
        out_shape=jax.ShapeDtypeStruct((nrows, D_MODEL), bf16),
        compiler_params=_cp("arbitrary", "arbitrary"),
        name="merge",
    )(a, b, c, wb, z, z, z)


def _short_conv_kernel(x_ref, xp_ref, xn_ref, w_ref, b_ref, o_ref, *, tiles_per_seq):
    i = pl.program_id(0)
    tm = x_ref.shape[0]
    x = x_ref[...]
    first = i % tiles_per_seq == 0
    last = i % tiles_per_seq == tiles_per_seq - 1
    prev = jnp.where(first, 0.0, xp_ref[7:8, :])
    nxt = jnp.where(last, 0.0, xn_ref[0:1, :])
    row = lax.broadcasted_iota(jnp.int32, x.shape, 0)
    up = jnp.where(row == 0, prev, pltpu.roll(x, 1, 0))
    dn = jnp.where(row == tm - 1, nxt, pltpu.roll(x, tm - 1, 0))
    o_ref[...] = w_ref[0:1, :] * up + w_ref[1:2, :] * x + w_ref[2:3, :] * dn + b_ref[...]


def _short_conv(z, w, b, row0, nrows, seq_len):
    tm, tn = 256, 1536
    t0 = row0 // tm
    nc = 3 * HY_WIDTH
    last8 = R // 8 - 1
    return pl.pallas_call(
        functools.partial(_short_conv_kernel, tiles_per_seq=seq_len // tm),
        grid=(nrows // tm, nc // tn),
        in_specs=[pl.BlockSpec((tm, tn), lambda i, j: (i + t0, j)),
                  pl.BlockSpec((8, tn), lambda i, j: (jnp.maximum((i + t0) * (tm // 8) - 1, 0), j)),
                  pl.BlockSpec((8, tn), lambda i, j: (jnp.minimum((i + t0 + 1) * (tm // 8), last8), j)),
                  pl.BlockSpec((3, tn), lambda i, j: (0, j)),
                  pl.BlockSpec((1, tn), lambda i, j: (0, j))],
        out_specs=pl.BlockSpec((tm, tn), lambda i, j: (i, j)),
        out_shape=jax.ShapeDtypeStruct((nrows, nc), f32),
        compiler_params=_cp("arbitrary", "arbitrary"),
        name="hyena_short_conv",
    )(z, z, z, w, b.reshape(1, nc))


def _hyfilt_kernel(w1_ref, b1_ref, w2_ref, b2_ref, w3_ref, fr_ref, dl_ref, o_ref, *, seq_len):
    tm = o_ref.shape[0]
    r = pl.program_id(0) * tm + lax.broadcasted_iota(jnp.int32, (tm, 1), 0)
    p = jnp.where(r < seq_len, r, 2 * seq_len - r)
    t = p.astype(f32) / seq_len
    lane = lax.broadcasted_iota(jnp.int32, (tm, 128), 1)
    band = jnp.where(lane <= HY_BANDS, lane, lane - HY_BANDS).astype(f32)
    ang = ((2.0 * math.pi) * t) * band
    feats = jnp.where(lane == 0, t,
                      jnp.where(lane <= HY_BANDS, jnp.sin(ang),
                                jnp.where(lane <= 2 * HY_BANDS, jnp.cos(ang), 0.0)))
    h = jnp.dot(feats.astype(bf16), w1_ref[...].astype(bf16), preferred_element_type=f32) + b1_ref[...]
    h = jnp.sin(fr_ref[0:1, :] * h)
    h = jnp.dot(h.astype(bf16), w2_ref[...].astype(bf16), preferred_element_type=f32) + b2_ref[...]
    h = jnp.sin(fr_ref[1:2, :] * h)
    k = jnp.dot(h.astype(bf16), w3_ref[...].astype(bf16), preferred_element_type=f32)
    k = k * jnp.exp(-t * dl_ref[...])
    o_ref[...] = jnp.where(r == seq_len, 0.0, k)


def _hyena_filter(seq_len, w1, b1, w2, b2, w3, freq):
    nc = HY_ORDER * HY_WIDTH
    tm, tn = min(512, seq_len), nc
    deltas = jnp.abs(jnp.linspace(HY_MIN_DECAY, HY_MAX_DECAY, HY_WIDTH, dtype=f32))
    dl = jnp.tile(deltas, HY_ORDER).reshape(1, nc)
    w1p = jnp.pad(w1, ((0, 128 - HY_EMB), (0, 0)))
    per_dir = nc // tn
    full = lambda i, j: (0, 0)
    return pl.pallas_call(
        functools.partial(_hyfilt_kernel, seq_len=seq_len),
        grid=(2 * seq_len // tm, per_dir),
        in_specs=[pl.BlockSpec((128, HY_FFN), full),
                  pl.BlockSpec((1, HY_FFN), full),
                  pl.BlockSpec((HY_FFN, HY_FFN), full),
                  pl.BlockSpec((1, HY_FFN), full),
                  pl.BlockSpec((HY_FFN, tn), lambda i, j: (0, jnp.where(i >= seq_len // tm, per_dir, 0) + j)),
                  pl.BlockSpec((2, HY_FFN), full),
                  pl.BlockSpec((1, tn), lambda i, j: (0, j))],
        out_specs=pl.BlockSpec((tm, tn), lambda i, j: (i, j)),
        out_shape=jax.ShapeDtypeStruct((2 * seq_len, nc), f32),
        compiler_params=_cp("arbitrary", "arbitrary"),
        name="hyena_filter",
    )(w1p, b1.reshape(1, HY_FFN), w2, b2.reshape(1, HY_FFN), w3, freq, dl)


def _pass3(m):
    hi = m.astype(bf16)
    lo = (m - hi.astype(f32)).astype(bf16)
    return jnp.concatenate([hi, hi, lo], axis=-1)


def _rhs3(x):
    hi = x.astype(bf16)
    lo = (x - hi.astype(f32)).astype(bf16)
    return jnp.concatenate([hi, lo, hi], axis=0)


def _stack_complex(ar, ai):
    return jnp.concatenate([jnp.concatenate([ar, -ai], axis=-1),
                            jnp.concatenate([ai, ar], axis=-1)], axis=-2)


def _dft_tables(seq_len):
    n = 2 * seq_len
    n1 = n // HY_N2
    half = n1 // 2
    i1 = jnp.arange(n1, dtype=jnp.int32)
    ang1 = (2.0 * math.pi / n1) * ((i1[:, None] * i1[None, :]) % n1).astype(f32)
    c1, s1 = jnp.cos(ang1), jnp.sin(ang1)
    m2 = _pass3(_stack_complex(c1[:, :half], -s1[:, :half]))
    m2f = _pass3(jnp.concatenate([c1, -s1], axis=0))
    m8 = _pass3(_stack_complex(c1[:half, :] / n, s1[:half, :] / n))
    i2 = jnp.arange(HY_N2, dtype=jnp.int32)
    k = i1[:, None, None] + n1 * i2[None, :, None]
    ang = (2.0 * math.pi / n) * ((i2[None, None, :] * k) % n).astype(f32)
    c, s = jnp.cos(ang), jnp.sin(ang)
    g4 = _pass3(_stack_complex(c, -s))
    ct, st = jnp.swapaxes(c, 1, 2), jnp.swapaxes(s, 1, 2)
    g6 = _pass3(_stack_complex(ct, st))
    return dict(m2=m2, m2f=m2f, m8=m8, g4=g4, g6=g6)


def _store_halves(ref, rows, val):
    ref[0, rows, :] = val[:, :128]
    ref[1, rows, :] = val[:, 128:]


def _load_halves(ref, rows):
    return jnp.concatenate([ref[0, rows, :], ref[1, rows, :]], axis=1)


def _hyfft_kernel(xa_ref, xb_ref, m2_ref, g4_ref, kr_ref, ki_ref, br, bi, *, n1):
    s = pl.program_id(1)

    @pl.when(s == 0)
    def _():
        def body(n2, carry):
            rows = pl.ds(n2, n1, stride=HY_N2)
            rhs = _rhs3(jnp.concatenate([xa_ref[rows, :], xb_ref[rows, :]], axis=1))
            out = jnp.dot(m2_ref[...], rhs, preferred_element_type=f32)
            _store_halves(br, rows, out[:n1])
            _store_halves(bi, rows, out[n1:])
            return carry
        lax.fori_loop(0, HY_N2, body, 0, unroll=HY_UNROLL)

    for j in range(HY_SLABS):
        rows = pl.ds(pl.multiple_of((s * HY_SLABS + j) * HY_N2, HY_N2), HY_N2)
        y = jnp.concatenate([_load_halves(br, rows), _load_halves(bi, rows)], axis=0)
        z = jnp.dot(g4_ref[j], _rhs3(y), preferred_element_type=f32)
        kr_ref[j * HY_N2:(j + 1) * HY_N2, :] = z[:HY_N2]
        ki_ref[j * HY_N2:(j + 1) * HY_N2, :] = z[HY_N2:]


def _hyena_filter_fft(kern, tabs):
    n, nc = kern.shape
    n1 = n // HY_N2
    tn = 256
    sl = HY_SLABS * HY_N2
    spec_out = pl.BlockSpec((sl, tn), lambda t, s: (s, t))
    return pl.pallas_call(
        functools.partial(_hyfft_kernel, n1=n1),
        grid=(nc // tn, n1 // HY_SLABS),
        in_specs=[pl.BlockSpec((n, 128), lambda t, s: (0, 2 * t)),
                  pl.BlockSpec((n, 128), lambda t, s: (0, 2 * t + 1)),
                  pl.BlockSpec(tabs["m2f"].shape, lambda t, s: (0, 0)),
                  pl.BlockSpec((HY_SLABS,) + tabs["g4"].shape[1:], lambda t, s: (s, 0, 0))],
        out_specs=[spec_out, spec_out],
        out_shape=[jax.ShapeDtypeStruct((n, nc), f32)] * 2,
        scratch_shapes=[pltpu.VMEM((2, n, 128), f32), pltpu.VMEM((2, n, 128), f32)],
        compiler_params=_cp("arbitrary", "arbitrary"),
        name="hyena_filter_fft",
    )(kern, kern, tabs["m2f"], tabs["g4"])


def _hyconv_kernel(x_ref, m2_ref, g4_ref, g6_ref, m8_ref, kr_ref, ki_ref, o_ref, br, bi, *, n1):
    s = pl.program_id(1)
    ns = pl.num_programs(1)
    half = n1 // 2
    cw = x_ref.shape[2]

    @pl.when(s == 0)
    def _():
        def body(n2, carry):
            rows = pl.ds(n2, half, stride=HY_N2)
            xr = jnp.concatenate([x_ref[0, rows, :], x_ref[2, rows, :]], axis=1)
            xi = jnp.concatenate([x_ref[1, rows, :], x_ref[3, rows, :]], axis=1)
            rhs = _rhs3(jnp.concatenate([xr, xi], axis=0))
            out = jnp.dot(m2_ref[...], rhs, preferred_element_type=f32)
            brows = pl.ds(n2, n1, stride=HY_N2)
            _store_halves(br, brows, out[:n1])
            _store_halves(bi, brows, out[n1:])
            return carry
        lax.fori_loop(0, HY_N2, body, 0, unroll=HY_UNROLL)

    for j in range(HY_SLABS):
        srows = pl.ds(pl.multiple_of((s * HY_SLABS + j) * HY_N2, HY_N2), HY_N2)
        y = jnp.concatenate([_load_halves(br, srows), _load_halves(bi, srows)], axis=0)
        z = jnp.dot(g4_ref[j], _rhs3(y), preferred_element_type=f32)
        zr, zi = z[:HY_N2], z[HY_N2:]
        kr = jnp.concatenate([kr_ref[j * HY_N2:(j + 1) * HY_N2, :]] * 2, axis=1)
        ki = jnp.concatenate([ki_ref[j * HY_N2:(j + 1) * HY_N2, :]] * 2, axis=1)
        w = jnp.concatenate([zr * kr - zi * ki, zr * ki + zi * kr], axis=0)
        t = jnp.dot(g6_ref[j], _rhs3(w), preferred_element_type=f32)
        _store_halves(br, srows, t[:HY_N2])
        _store_halves(bi, srows, t[HY_N2:])

    @pl.when(s == ns - 1)
    def _():
        def body(n2, carry):
            rows = pl.ds(n2, n1, stride=HY_N2)
            rhs = _rhs3(jnp.concatenate([_load_halves(br, rows), _load_halves(bi, rows)], axis=0))
            out = jnp.dot(m8_ref[...], rhs, preferred_element_type=f32)
            orows = pl.ds(n2, half, stride=HY_N2)
            o_ref[0, orows, :] = out[:half, :cw]
            o_ref[2, orows, :] = out[:half, cw:]
            o_ref[1, orows, :] = out[half:, :cw]
            o_ref[3, orows, :] = out[half:, cw:]
            return carry
        lax.fori_loop(0, HY_N2, body, 0, unroll=HY_UNROLL)


def _hyena_conv(x, col0, khr, khi, order, tabs):
    seq_len = x.shape[1]
    n = 2 * seq_len
    n1 = n // HY_N2
    cw = 128
    sl = HY_SLABS * HY_N2
    nt = HY_WIDTH // cw
    kspec = pl.BlockSpec((sl, cw), lambda t, s: (s, order * nt + t))
    const = lambda a: pl.BlockSpec(a.shape, lambda t, s: (0,) * a.ndim)
    gspec = pl.BlockSpec((HY_SLABS,) + tabs["g4"].shape[1:], lambda t, s: (s, 0, 0))
    return pl.pallas_call(
        functools.partial(_hyconv_kernel, n1=n1),
        grid=(nt, n1 // HY_SLABS),
        in_specs=[pl.BlockSpec((BATCH, seq_len, cw), lambda t, s: (0, 0, col0 // cw + t)),
                  const(tabs["m2"]), gspec, gspec, const(tabs["m8"]), kspec, kspec],
        out_specs=pl.BlockSpec((BATCH, seq_len, cw), lambda t, s: (0, 0, t)),
        out_shape=jax.ShapeDtypeStruct((BATCH, seq_len, HY_WIDTH), f32),
        scratch_shapes=[pltpu.VMEM((2, n, cw), f32), pltpu.VMEM((2, n, cw), f32)],
        compiler_params=_cp("arbitrary", "arbitrary", vmem=60 * 1024 * 1024),
        name="hyena_long_conv",
    )(x, tabs["m2"], tabs["g4"], tabs["g6"], tabs["m8"], khr, khi)


def _hygate_kernel(c_ref, y_ref, g_ref, b_ref, o_ref):
    y = y_ref[...]
    o_ref[...] = (g_ref[...] * (c_ref[...] + b_ref[...] * y)).astype(o_ref.dtype)


def _hyena_gate(conv, y, ycol0, zc, gcol0, bias, out_dtype):
    tm, tn = 512, 512
    nrows = conv.shape[0]
    return pl.pallas_call(
        _hygate_kernel,
        grid=(nrows // tm, HY_WIDTH // tn),
        in_specs=[pl.BlockSpec((tm, tn), lambda i, j: (i, j)),
                  pl.BlockSpec((tm, tn), lambda i, j: (i, ycol0 // tn + j)),
                  pl.BlockSpec((tm, tn), lambda i, j: (i, gcol0 // tn + j)),
                  pl.BlockSpec((1, tn), lambda i, j: (0, j))],
        out_specs=pl.BlockSpec((tm, tn), lambda i, j: (i, j)),
        out_shape=jax.ShapeDtypeStruct((nrows, HY_WIDTH), out_dtype),
        compiler_params=_cp("arbitrary", "arbitrary"),
        name="hyena_gate",
    )(conv, y, zc, bias.reshape(1, HY_WIDTH))


def _short_dft_tables(seq_len):
    n = 2 * seq_len
    k = jnp.arange(n, dtype=jnp.int32)
    ang = (2.0 * math.pi / n) * ((k[:, None] * k[None, :]) % n).astype(f32)
    c, s = jnp.cos(ang), jnp.sin(ang)
    mx = _pass3(_stack_complex(c[:, :seq_len], -s[:, :seq_len]))
    mf = _pass3(jnp.concatenate([c, -s], axis=0))
    mi = _pass3(_stack_complex(c[:seq_len, :] / n, s[:seq_len, :] / n))
    return dict(mx=mx, mf=mf, mi=mi)


def _short_fft_kernel(x_ref, mf_ref, kr_ref, ki_ref):
    n = x_ref.shape[0]
    z = jnp.dot(mf_ref[...], _rhs3(x_ref[...]), preferred_element_type=f32)
    kr_ref[...] = z[:n]
    ki_ref[...] = z[n:]


def _short_filter_fft(kern, tabs):
    n, nc = kern.shape
    tn = 256
    spec = pl.BlockSpec((n, tn), lambda t: (0, t))
    return pl.pallas_call(
        _short_fft_kernel,
        grid=(nc // tn,),
        in_specs=[spec, pl.BlockSpec(tabs["mf"].shape, lambda t: (0, 0))],
        out_specs=[spec, spec],
        out_shape=[jax.ShapeDtypeStruct((n, nc), f32)] * 2,
        compiler_params=_cp("arbitrary"),
        name="hyena_short_filter_fft",
    )(kern, tabs["mf"])


def _short_conv_fft_kernel(x_ref, mx_ref, mi_ref, kr_ref, ki_ref, o_ref):
    seq_len = x_ref.shape[1]
    cw = x_ref.shape[2]
    n = 2 * seq_len
    xr = jnp.concatenate([x_ref[0], x_ref[2]], axis=1)
    xi = jnp.concatenate([x_ref[1], x_ref[3]], axis=1)
    z = jnp.dot(mx_ref[...], _rhs3(jnp.concatenate([xr, xi], axis=0)), preferred_element_type=f32)
    zr, zi = z[:n], z[n:]
    kr = jnp.concatenate([kr_ref[...]] * 2, axis=1)
    ki = jnp.concatenate([ki_ref[...]] * 2, axis=1)
    w = jnp.concatenate([zr * kr - zi * ki, zr * ki + zi * kr], axis=0)
    y = jnp.dot(mi_ref[...], _rhs3(w), preferred_element_type=f32)
    o_ref[0] = y[:seq_len, :cw]
    o_ref[2] = y[:seq_len, cw:]
    o_ref[1] = y[seq_len:, :cw]
    o_ref[3] = y[seq_len:, cw:]


def _short_long_conv(x, col0, khr, khi, order, tabs):
    seq_len = x.shape[1]
    n = 2 * seq_len
    cw = 128
    nt = HY_WIDTH // cw
    kspec = pl.BlockSpec((n, cw), lambda t: (0, order * nt + t))
    const = lambda a: pl.BlockSpec(a.shape, lambda t: (0,) * a.ndim)
    return pl.pallas_call(
        _short_conv_fft_kernel,
        grid=(nt,),
        in_specs=[pl.BlockSpec((BATCH, seq_len, cw), lambda t: (0, 0, col0 // cw + t)),
                  const(tabs["mx"]), const(tabs["mi"]), kspec, kspec],
        out_specs=pl.BlockSpec((BATCH, seq_len, cw), lambda t: (0, 0, t)),
        out_shape=jax.ShapeDtypeStruct((BATCH, seq_len, HY_WIDTH), f32),
        compiler_params=_cp("arbitrary"),
        name="hyena_short_long_conv",
    )(x, tabs["mx"], tabs["mi"], khr, khi)


def _hyena_branch(z, row0, seq_len, tabs, short_w, short_b, w1, b1, w2, b2, w3, freq, bias):
    two_stage = "g4" in tabs
    conv = _hyena_conv if two_stage else _short_long_conv
    nrows = BATCH * seq_len
    zc = _short_conv(z, short_w, short_b, row0, nrows, seq_len)
    kern = _hyena_filter(seq_len, w1, b1, w2, b2, w3, freq)
    khr, khi = (_hyena_filter_fft if two_stage else _short_filter_fft)(kern, tabs)
    zc3 = zc.reshape(BATCH, seq_len, 3 * HY_WIDTH)
    c1 = conv(zc3, 0, khr, khi, 0, tabs).reshape(nrows, HY_WIDTH)
    y1 = _hyena_gate(c1, zc, 0, zc, HY_WIDTH, bias[0], f32)
    c2 = conv(y1.reshape(BATCH, seq_len, HY_WIDTH), 0, khr, khi, 1, tabs).reshape(nrows, HY_WIDTH)
    return _hyena_gate(c2, y1, 0, zc, 2 * HY_WIDTH, bias[1], bf16)


def _rope_tables():
    half = ATT_HD // 2
    nf = half // 2
    inv = ROPE_BASE ** (-jnp.arange(nf, dtype=f32) / nf)
    pos = jnp.arange(SEQ)
    rows = (pos // GRID_W).astype(f32)[:, None] * inv[None, :]
    cols = (pos % GRID_W).astype(f32)[:, None] * inv[None, :]
    zero = jnp.zeros_like(rows)
    cos = jnp.concatenate([jnp.cos(rows)] * 2 + [jnp.cos(cols)] * 2, axis=-1)
    sin_up = jnp.concatenate([-jnp.sin(rows), zero, -jnp.sin(cols), zero], axis=-1)
    sin_dn = jnp.concatenate([zero, jnp.sin(rows), zero, jnp.sin(cols)], axis=-1)
    return cos, sin_up, sin_dn


def _rope_kernel(q_ref, k_ref, cos_ref, su_ref, sd_ref, qo_ref, ko_ref):
    cos, su, sd = cos_ref[...], su_ref[...], sd_ref[...]

    def rot(x):
        return x * cos + pltpu.roll(x, 96, 1) * su + pltpu.roll(x, 32, 1) * sd

    for h in range(ATT_HEADS):
        s = slice(h * ATT_HD, (h + 1) * ATT_HD)
        qo_ref[:, s] = rot(q_ref[:, s]).astype(qo_ref.dtype)
    for h in range(ATT_KV_HEADS):
        s = slice(h * ATT_HD, (h + 1) * ATT_HD)
        ko_ref[:, s] = rot(k_ref[:, s]).astype(ko_ref.dtype)


def _rope(z, tables):
    tm = 512
    nq = ATT_HEADS * ATT_HD
    nk = ATT_KV_HEADS * ATT_HD
    tab = pl.BlockSpec((tm, ATT_HD), lambda i: (i % (SEQ // tm), 0))
    return pl.pallas_call(
        _rope_kernel,
        grid=(NLAT // tm,),
        in_specs=[pl.BlockSpec((tm, nq), lambda i: (i, Z_Q // nq)),
                  pl.BlockSpec((tm, nk), lambda i: (i, Z_K // nk)),
                  tab, tab, tab],
        out_specs=[pl.BlockSpec((tm, nq), lambda i: (i, 0)),
                   pl.BlockSpec((tm, nk), lambda i: (i, 0))],
        out_shape=[jax.ShapeDtypeStruct((NLAT, nq), bf16), jax.ShapeDtypeStruct((NLAT, nk), bf16)],
        compiler_params=_cp("arbitrary"),
        name="rope",
    )(z, z, *tables)


_ATT_SCALE = ATT_HD ** -0.5
_NEG = float(np.finfo(np.float32).min)
_NT = (((1,), (1,)), ((), ()))


def _sink_column(sink_ref, h, rows):
    rg = lax.broadcasted_iota(jnp.int32, (rows * ATT_GROUP, 1), 0) // rows
    col = jnp.full((rows * ATT_GROUP, 1), sink_ref[h * ATT_GROUP + ATT_GROUP - 1], f32)
    for g in range(ATT_GROUP - 2, -1, -1):
        col = jnp.where(rg == g, sink_ref[h * ATT_GROUP + g], col)
    return col


def _attn_kernel(sink_ref, q_ref, kp_ref, kc_ref, kn_ref, vp_ref, vc_ref, vn_ref, kx_ref, vx_ref, o_ref):
    i = pl.program_id(1)
    nb = pl.num_programs(1)
    blk = ATT_BLOCK
    r = lax.broadcasted_iota(jnp.int32, (ATT_GROUP * blk, 3 * blk), 0) % blk
    c = lax.broadcasted_iota(jnp.int32, (ATT_GROUP * blk, 3 * blk), 1)
    lo = jnp.where(i > 0, 0, blk)
    hi = jnp.where(i < nb - 1, 3 * blk, 2 * blk)
    valid = (c >= r) & (c <= r + 2 * ATT_WINDOW) & (c >= lo) & (c < hi)
    for h in range(ATT_KV_HEADS):
        hs = slice(h * ATT_HD, (h + 1) * ATT_HD)
        k_win = jnp.concatenate([kp_ref[:, hs], kc_ref[:, hs], kn_ref[:, hs]], axis=0)
        v_win = jnp.concatenate([vp_ref[:, hs], vc_ref[:, hs], vn_ref[:, hs]], axis=0).astype(bf16)
        k_ctx = kx_ref[:, hs].astype(bf16)
        v_ctx = vx_ref[:, hs].astype(bf16)
        q = jnp.concatenate([q_ref[:, (h * ATT_GROUP + g) * ATT_HD:(h * ATT_GROUP + g + 1) * ATT_HD]
                             for g in range(ATT_GROUP)], axis=0)
        s_win = lax.dot_general(q, k_win, _NT, preferred_element_type=f32) * _ATT_SCALE
        s_win = jnp.where(valid, s_win, _NEG)
        s_ctx = lax.dot_general(q, k_ctx, _NT, preferred_element_type=f32) * _ATT_SCALE
        sink = _sink_column(sink_ref, h, blk)
        m = jnp.maximum(jnp.maximum(jnp.max(s_win, axis=-1, keepdims=True),
                                    jnp.max(s_ctx, axis=-1, keepdims=True)), sink)
        p_win = jnp.exp(s_win - m)
        p_ctx = jnp.exp(s_ctx - m)
        den = (jnp.sum(p_win, axis=-1, keepdims=True) + jnp.sum(p_ctx, axis=-1, keepdims=True)
               + jnp.exp(sink - m))
        o = (jnp.dot(p_win.astype(bf16), v_win, preferred_element_type=f32)
             + jnp.dot(p_ctx.astype(bf16), v_ctx, preferred_element_type=f32)) / den
        for g in range(ATT_GROUP):
            cs = slice((h * ATT_GROUP + g) * ATT_HD, (h * ATT_GROUP + g + 1) * ATT_HD)
            o_ref[:, cs] = o[g * blk:(g + 1) * blk].astype(o_ref.dtype)


def _window_attention(qr, kr, z, sink):
    blk = ATT_BLOCK
    nb = SEQ // blk
    nkv = ATT_KV_HEADS * ATT_HD
    cx = NLAT // CTX_LEN

    def krow(off):
        return lambda b, i: (b * nb + jnp.clip(i + off, 0, nb - 1), 0)

    def vrow(off):
        return lambda b, i: (b * nb + jnp.clip(i + off, 0, nb - 1), Z_V // nkv)

    return pl.pallas_call(
        _attn_kernel,
        grid=(BATCH, nb),
        in_specs=[pl.BlockSpec(memory_space=pltpu.SMEM),
                  pl.BlockSpec((blk, ATT_HEADS * ATT_HD), lambda b, i: (b * nb + i, 0)),
                  pl.BlockSpec((blk, nkv), krow(-1)),
                  pl.BlockSpec((blk, nkv), krow(0)),
                  pl.BlockSpec((blk, nkv), krow(1)),
                  pl.BlockSpec((blk, nkv), vrow(-1)),
                  pl.BlockSpec((blk, nkv), vrow(0)),
                  pl.BlockSpec((blk, nkv), vrow(1)),
                  pl.BlockSpec((CTX_LEN, nkv), lambda b, i: (cx + b, Z_K // nkv)),
                  pl.BlockSpec((CTX_LEN, nkv), lambda b, i: (cx + b, Z_V // nkv))],
        out_specs=pl.BlockSpec((blk, ATT_HEADS * ATT_HD), lambda b, i: (b * nb + i, 0)),
        out_shape=jax.ShapeDtypeStruct((NLAT, ATT_HEADS * ATT_HD), bf16),
        compiler_params=_cp("arbitrary", "arbitrary"),
        name="window_attention",
    )(sink, qr, kr, kr, kr, z, z, z, z, z)


def _ctx_attn_kernel(sink_ref, q_ref, k_ref, v_ref, o_ref):
    for h in range(ATT_KV_HEADS):
        hs = slice(h * ATT_HD, (h + 1) * ATT_HD)
        k = k_ref[:, hs].astype(bf16)
        v = v_ref[:, hs].astype(bf16)
        q = jnp.concatenate([q_ref[:, (h * ATT_GROUP + g) * ATT_HD:(h * ATT_GROUP + g + 1) * ATT_HD]
                             for g in range(ATT_GROUP)], axis=0).astype(bf16)
        s = lax.dot_general(q, k, _NT, preferred_element_type=f32) * _ATT_SCALE
        sink = _sink_column(sink_ref, h, CTX_LEN)
        m = jnp.maximum(jnp.max(s, axis=-1, keepdims=True), sink)
        p = jnp.exp(s - m)
        den = jnp.sum(p, axis=-1, keepdims=True) + jnp.exp(sink - m)
        o = jnp.dot(p.astype(bf16), v, preferred_element_type=f32) / den
        for g in range(ATT_GROUP):
            cs = slice((h * ATT_GROUP + g) * ATT_HD, (h * ATT_GROUP + g + 1) * ATT_HD)
            o_ref[:, cs] = o[g * CTX_LEN:(g + 1) * CTX_LEN].astype(o_ref.dtype)


def _context_attention(z, sink):
    nq = ATT_HEADS * ATT_HD
    nkv = ATT_KV_HEADS * ATT_HD
    cx = NLAT // CTX_LEN
    return pl.pallas_call(
        _ctx_attn_kernel,
        grid=(BATCH,),
        in_specs=[pl.BlockSpec(memory_space=pltpu.SMEM),
                  pl.BlockSpec((CTX_LEN, nq), lambda b: (cx + b, Z_Q // nq)),
                  pl.BlockSpec((CTX_LEN, nkv), lambda b: (cx + b, Z_K // nkv)),
                  pl.BlockSpec((CTX_LEN, nkv), lambda b: (cx + b, Z_V // nkv))],
        out_specs=pl.BlockSpec((CTX_LEN, nq), lambda b: (b, 0)),
        out_shape=jax.ShapeDtypeStruct((NCTX, nq), bf16),
        compiler_params=_cp("arbitrary"),
        name="context_attention",
    )(sink, z, z, z)


_ML_SCALE = ML_QK ** -0.5


def _split3(x):
    x1 = x.astype(bf16)
    r1 = x - x1.astype(f32)
    x2 = r1.astype(bf16)
    x3 = (r1 - x2.astype(f32)).astype(bf16)
    return x1, x2, x3


def _mlstm_kernel(q_ref, k_ref, v_ref, g_ref, gb_ref, o_ref, ct_ref, n_ref, m_ref):
    d = pl.program_id(1)
    c = pl.program_id(2)
    ch = ML_CHUNK

    @pl.when(c == 0)
    def _():
        ct_ref[...] = jnp.zeros_like(ct_ref)
        n_ref[...] = jnp.zeros_like(n_ref)
        m_ref[...] = jnp.zeros_like(m_ref)

    g = g_ref[...] + gb_ref[...]
    row = lax.broadcasted_iota(jnp.int32, (ch, ch), 0)
    col = lax.broadcasted_iota(jnp.int32, (ch, ch), 1)
    tri = (row - col) * (1 - 2 * d) >= 0
    tri_b = tri.astype(f32).astype(bf16)
    lf = jax.nn.log_sigmoid(g)
    l1, l2, l3 = _split3(lf)
    bcol = (jnp.dot(tri_b, l1, preferred_element_type=f32)
            + jnp.dot(tri_b, l2, preferred_element_type=f32)
            + jnp.dot(tri_b, l3, preferred_element_type=f32))
    bend = jnp.where(d == 0, bcol[ch - 1:ch, :], bcol[0:1, :])
    g_t = g.T
    b_t = bcol.T

    for h in range(ML_HEADS):
        qf = q_ref[:, h * ML_QK:(h + 1) * ML_QK] * _ML_SCALE
        kf = k_ref[:, h * ML_QK:(h + 1) * ML_QK]
        vf = v_ref[:, h * ML_V:(h + 1) * ML_V]
        q = qf.astype(bf16)
        k = kf.astype(bf16)
        li_c = g[:, h:h + 1]
        b_c = bcol[:, ML_HEADS + h:ML_HEADS + h + 1]
        li_r = g_t[h:h + 1, :]
        b_r = b_t[ML_HEADS + h:ML_HEADS + h + 1, :]
        m_prev = m_ref[h][:, 0:1]
        dmat = jnp.where(tri, b_c - b_r + li_r, -jnp.inf)
        inter = b_c + m_prev
        m_t = jnp.maximum(inter, jnp.max(dmat, axis=-1, keepdims=True))
        w_intra = jnp.exp(dmat - m_t)
        w_inter = jnp.exp(inter - m_t)
        s = lax.dot_general(q, k, _NT, preferred_element_type=f32) * w_intra
        qc = jnp.dot(q, ct_ref[h].astype(bf16), preferred_element_type=f32)
        num = jnp.dot(s.astype(bf16), vf.astype(bf16), preferred_element_type=f32) + w_inter * qc
        den = (jnp.sum(s, axis=-1, keepdims=True)
               + w_inter * jnp.sum(qf * n_ref[h], axis=-1, keepdims=True))
        o_ref[:, h * ML_V:(h + 1) * ML_V] = num / jnp.maximum(jnp.abs(den), jnp.exp(-m_t))

        b_e = bend[:, ML_HEADS + h:ML_HEADS + h + 1]
        g_c = b_e - b_c + li_c
        m_new = jnp.maximum(b_e + m_prev, jnp.max(g_c, axis=0, keepdims=True))
        w_s = jnp.exp(g_c - m_new)
        w_c = jnp.exp(b_e + m_prev - m_new)
        vw = (vf * w_s).astype(bf16)
        ct_ref[h] = w_c * ct_ref[h] + jnp.dot(kf.T.astype(bf16), vw, preferred_element_type=f32)
        n_ref[h] = w_c * n_ref[h] + jnp.sum(kf * w_s, axis=0, keepdims=True)
        m_ref[h] = jnp.broadcast_to(m_new, (1, 128))


def _mlstm(z, gate_b):
    ch = ML_CHUNK
    ncl = SEQ // ch
    nsteps = ncl + CTX_LEN // ch
    assert CTX_LEN == ch

    def rt(b, d, c):
        lat = b * ncl + jnp.where(d == 0, c - 1, ncl - c)
        return jnp.where(c == 0, NLAT // ch + b, lat)

    nq = ML_HEADS * ML_QK
    nv = ML_HEADS * ML_V
    return pl.pallas_call(
        _mlstm_kernel,
        grid=(BATCH, 2, nsteps),
        in_specs=[pl.BlockSpec((ch, nq), lambda b, d, c: (rt(b, d, c), Z_MQ // nq)),
                  pl.BlockSpec((ch, nq), lambda b, d, c: (rt(b, d, c), Z_MK // nq)),
                  pl.BlockSpec((ch, nv), lambda b, d, c: (rt(b, d, c), Z_MV // nv)),
                  pl.BlockSpec((ch, 128), lambda b, d, c: (rt(b, d, c), Z_GATE // 128 + d)),
                  pl.BlockSpec((None, 1, 128), lambda b, d, c: (d, 0, 0))],
        out_specs=pl.BlockSpec((None, ch, nv), lambda b, d, c: (d, rt(b, d, c), 0)),
        out_shape=jax.ShapeDtypeStruct((2, R, nv), f32),
        scratch_shapes=[pltpu.VMEM((ML_HEADS, ML_QK, ML_V), f32),
                        pltpu.VMEM((ML_HEADS, 1, ML_QK), f32),
                        pltpu.VMEM((ML_HEADS, 1, 128), f32)],
        compiler_params=_cp("arbitrary", "arbitrary", "arbitrary"),
        name="mlstm",
    )(z, z, z, z, gate_b)


def _mlstm_out_kernel(h_ref, zo_ref, g_ref, o_ref):
    for h in range(ML_HEADS):
        s = slice(h * ML_V, (h + 1) * ML_V)
        x = h_ref[0, :, s] + h_ref[1, :, s]
        xn = x * lax.rsqrt(jnp.mean(x * x, axis=-1, keepdims=True) + EPS) * g_ref[:, s]
        o_ref[:, s] = (xn * jax.nn.sigmoid(zo_ref[:, s])).astype(o_ref.dtype)


def _mlstm_out(hh, z, g, nrows):
    tm = 512
    nv = ML_HEADS * ML_V
    return pl.pallas_call(
        _mlstm_out_kernel,
        grid=(nrows // tm,),
        in_specs=[pl.BlockSpec((2, tm, nv), lambda i: (0, i, 0)),
                  pl.BlockSpec((tm, nv), lambda i: (i, Z_MO // nv)),
                  pl.BlockSpec((1, nv), lambda i: (0, 0))],
        out_specs=pl.BlockSpec((tm, nv), lambda i: (i, 0)),
        out_shape=jax.ShapeDtypeStruct((nrows, nv), bf16),
        compiler_params=_cp("arbitrary"),
        name="mlstm_out",
    )(hh, z, g.reshape(1, nv))


def _router_kernel(f_ref, w_ref, idx_ref, p_ref):
    logits = jnp.dot(f_ref[...].astype(bf16), w_ref[...], preferred_element_type=f32)
    lane = lax.broadcasted_iota(jnp.int32, logits.shape, 1).astype(f32)
    logits = jnp.where(lane < N_EXPERTS, logits, -jnp.inf)
    v1 = jnp.max(logits, axis=-1, keepdims=True)
    i1 = jnp.min(jnp.where(logits == v1, lane, 128.0), axis=-1, keepdims=True)
    rest = jnp.where(lane == i1, -jnp.inf, logits)
    v2 = jnp.max(rest, axis=-1, keepdims=True)
    i2 = jnp.min(jnp.where(rest == v2, lane, 128.0), axis=-1, keepdims=True)
    e = jnp.exp(v2 - v1)
    p1 = 1.0 / (1.0 + e)
    p2 = e / (1.0 + e)
    idx_ref[...] = jnp.where(lane == 0, i1, jnp.where(lane == 1, i2, 0.0)).astype(jnp.int32)
    p_ref[...] = jnp.where(lane == 0, p1, jnp.where(lane == 1, p2, 0.0))


def _router(f, w_router):
    tm = 512
    w = jnp.pad(w_router, ((0, 0), (0, 128 - N_EXPERTS))).astype(bf16)
    return pl.pallas_call(
        _router_kernel,
        grid=(NLAT // tm,),
        in_specs=[pl.BlockSpec((tm, D_MODEL), lambda i: (i, 0)),
                  pl.BlockSpec((D_MODEL, 128), lambda i: (0, 0))],
        out_specs=[pl.BlockSpec((tm, 128), lambda i: (i, 0)),
                   pl.BlockSpec((tm, 128), lambda i: (i, 0))],
        out_shape=[jax.ShapeDtypeStruct((NLAT, 128), jnp.int32),
                   jax.ShapeDtypeStruct((NLAT, 128), f32)],
        compiler_params=_cp("arbitrary"),
        name="router",
    )(f, w)


def _gather_kernel(tok_ref, src_ref, o_ref, buf, sem):
    base = pl.program_id(0) * MOE_TILE

    def copy(r):
        return pltpu.make_async_copy(src_ref.at[pl.ds(tok_ref[base + r], 1)], buf.at[pl.ds(r, 1)], sem)

    def start(r, carry):
        copy(r).start()
        return carry

    def wait(r, carry):
        copy(r).wait()
        return carry

    lax.fori_loop(0, MOE_TILE, start, 0, unroll=4)
    lax.fori_loop(0, MOE_TILE, wait, 0)
    o_ref[...] = buf[...].astype(o_ref.dtype)


def _gather_rows(src, rows, n_out, out_dtype):
    width = src.shape[1]
    return pl.pallas_call(
        _gather_kernel,
        grid_spec=pltpu.PrefetchScalarGridSpec(
            num_scalar_prefetch=1,
            grid=(n_out // MOE_TILE,),
            in_specs=[pl.BlockSpec(memory_space=pl.ANY)],
            out_specs=pl.BlockSpec((MOE_TILE, width), lambda i, tok: (i, 0)),
            scratch_shapes=[pltpu.VMEM((MOE_TILE, width), src.dtype), pltpu.SemaphoreType.DMA(())]),
        out_shape=jax.ShapeDtypeStruct((n_out, width), out_dtype),
        compiler_params=_cp("arbitrary"),
        name="gather_rows",
    )(rows, src)


def _moe_up_kernel(be_ref, nu_ref, a_ref, wg_ref, wu_ref, o_ref, wg_bf, wu_bf):
    i = pl.program_id(1)
    used = i < nu_ref[0]
    fresh = jnp.logical_or(i == 0, be_ref[i] != be_ref[jnp.maximum(i - 1, 0)])

    @pl.when(jnp.logical_and(used, fresh))
    def _():
        wg_bf[...] = wg_ref[...].astype(bf16)
        wu_bf[...] = wu_ref[...].astype(bf16)

    @pl.when(used)
    def _():
        a = a_ref[...]
        g = jnp.dot(a, wg_bf[...], preferred_element_type=f32)
        u = jnp.dot(a, wu_bf[...], preferred_element_type=f32)
        o_ref[...] = (g * jax.nn.sigmoid(g) * u).astype(o_ref.dtype)

    @pl.when(i >= nu_ref[0])
    def _():
        o_ref[...] = jnp.zeros_like(o_ref)


def _moe_up(xs, wg, wu, blk_expert, n_used):
    tn = 1024
    nblk = xs.shape[0] // MOE_TILE

    def row(j, i, be, nu):
        return (jnp.minimum(i, nu[0] - 1), 0)

    def wmap(j, i, be, nu):
        return (be[jnp.minimum(i, nu[0] - 1)], 0, j)

    return pl.pallas_call(
        _moe_up_kernel,
        grid_spec=pltpu.PrefetchScalarGridSpec(
            num_scalar_prefetch=2,
            grid=(FFN_EXPERT // tn, nblk),
            in_specs=[pl.BlockSpec((MOE_TILE, D_MODEL), row),
                      pl.BlockSpec((None, D_MODEL, tn), wmap),
                      pl.BlockSpec((None, D_MODEL, tn), wmap)],
            out_specs=pl.BlockSpec((MOE_TILE, tn), lambda j, i, be, nu: (i, j)),
            scratch_shapes=[pltpu.VMEM((D_MODEL, tn), bf16), pltpu.VMEM((D_MODEL, tn), bf16)]),
        out_shape=jax.ShapeDtypeStruct((xs.shape[0], FFN_EXPERT), bf16),
        compiler_params=_cp("arbitrary", "arbitrary"),
        name="moe_up",
    )(blk_expert, n_used, xs, wg, wu)


def _moe_down_kernel(be_ref, nu_ref, a_ref, w_ref, o_ref):
    i = pl.program_id(1)

    @pl.when(i < nu_ref[0])
    def _():
        o_ref[...] = jnp.dot(a_ref[...], w_ref[...], preferred_element_type=f32)

    @pl.when(i >= nu_ref[0])
    def _():
        o_ref[...] = jnp.zeros_like(o_ref)


def _moe_down(hs, wd, blk_expert, n_used):
    tn = 512
    nblk = hs.shape[0] // MOE_TILE

    def row(j, i, be, nu):
        return (jnp.minimum(i, nu[0] - 1), 0)

    def wmap(j, i, be, nu):
        return (be[jnp.minimum(i, nu[0] - 1)], 0, j)

    return pl.pallas_call(
        _moe_down_kernel,
        grid_spec=pltpu.PrefetchScalarGridSpec(
            num_scalar_prefetch=2,
            grid=(D_MODEL // tn, nblk),
            in_specs=[pl.BlockSpec((MOE_TILE, FFN_EXPERT), row),
                      pl.BlockSpec((None, FFN_EXPERT, tn), wmap)],
            out_specs=pl.BlockSpec((MOE_TILE, tn), lambda j, i, be, nu: (i, j))),
        out_shape=jax.ShapeDtypeStruct((hs.shape[0], D_MODEL), f32),
        compiler_params=_cp("arbitrary", "arbitrary"),
        name="moe_down",
    )(blk_expert, n_used, hs, wd)


_COMBINE_TM = 256


def _combine_kernel(slot_ref, yb_ref, x_ref, p_ref, g2_ref, fg_ref, o_ref, buf, sem):
    i = pl.program_id(0)
    cur = i % 2

    def copy(step, b, r, k):
        s = slot_ref[(step * _COMBINE_TM + r) * TOP_K + k]
        return pltpu.make_async_copy(yb_ref.at[pl.ds(s, 1)], buf.at[b, k, pl.ds(r, 1)], sem.at[b])

    def fetch(step, b):
        def start(r, carry):
            copy(step, b, r, 0).start()
            copy(step, b, r, 1).start()
            return carry
        lax.fori_loop(0, _COMBINE_TM, start, 0, unroll=2)

    @pl.when(i == 0)
    def _():
        fetch(0, 0)

    @pl.when(i + 1 < pl.num_programs(0))
    def _():
        fetch(i + 1, 1 - cur)

    def wait(r, carry):
        copy(i, cur, r, 0).wait()
        copy(i, cur, r, 1).wait()
        return carry

    lax.fori_loop(0, _COMBINE_TM, wait, 0)
    p = p_ref[...]
    y = buf[cur, 0] * p[:, 0:1] + buf[cur, 1] * p[:, 1:2]
    x = x_ref[...] + g2_ref[...] * y
    o_ref[...] = x * lax.rsqrt(jnp.mean(x * x, axis=-1, keepdims=True) + EPS) * fg_ref[...]


def _combine_final(slot, yb, xa, probs, modt, final_g):
    tm = _COMBINE_TM
    return pl.pallas_call(
        _combine_kernel,
        grid_spec=pltpu.PrefetchScalarGridSpec(
            num_scalar_prefetch=1,
            grid=(NLAT // tm,),
            in_specs=[pl.BlockSpec(memory_space=pl.ANY),
                      pl.BlockSpec((tm, D_MODEL), lambda i, s: (i, 0)),
                      pl.BlockSpec((tm, 128), lambda i, s: (i, 0)),
                      pl.BlockSpec((None, 1, D_MODEL), lambda i, s: (i // (SEQ // tm), 0, 5)),
                      pl.BlockSpec((1, D_MODEL), lambda i, s: (0, 0))],
            out_specs=pl.BlockSpec((tm, D_MODEL), lambda i, s: (i, 0)),
            scratch_shapes=[pltpu.VMEM((2, TOP_K, tm, D_MODEL), f32),
                            pltpu.SemaphoreType.DMA((2,))]),
        out_shape=jax.ShapeDtypeStruct((NLAT, D_MODEL), f32),
        compiler_params=_cp("arbitrary"),
        name="moe_combine_final_norm",
    )(slot, yb, xa, probs, modt, final_g.reshape(1, D_MODEL))


def _moe_routing(top_i):
    a = NLAT * TOP_K
    e_flat = top_i.reshape(a)
    onehot = (e_flat[:, None] == jnp.arange(N_EXPERTS)[None, :]).astype(jnp.int32)
    csum = jnp.cumsum(onehot, axis=0)
    rank = jnp.sum(onehot * csum, axis=1) - 1
    counts = csum[-1]
    padded = (counts + MOE_TILE - 1) // MOE_TILE * MOE_TILE
    pad_end = jnp.cumsum(padded)
    pad_start = pad_end - padded
    slot = (pad_start[e_flat] + rank).astype(jnp.int32)
    n_rows = a + N_EXPERTS * MOE_TILE
    nblk = n_rows // MOE_TILE
    slot_tok = jnp.zeros((n_rows,), jnp.int32).at[slot].set(jnp.arange(a, dtype=jnp.int32) // TOP_K)
    blk_expert = jnp.minimum(jnp.searchsorted(pad_end, jnp.arange(nblk) * MOE_TILE, side='right'),
                             N_EXPERTS - 1).astype(jnp.int32)
    n_used = (pad_end[-1:] // MOE_TILE).astype(jnp.int32)
    return slot, slot_tok, blk_expert, n_used, n_rows


def _in_proj_weight(w, gate_b):
    o = np.cumsum((0,) + (3072, 1024, 256, 256, 512, 512, 1024, 1024, 16, 6144))
    hy, q, k, v, mq, mk, mv, mo, gt, mg = [w[:, o[i]:o[i + 1]] for i in range(10)]
    pad = jnp.zeros((D_MODEL, 128 - 2 * ML_HEADS), w.dtype)
    gates = [jnp.concatenate([gt[:, 8 * d:8 * d + 8], pad], axis=1) for d in range(2)]
    wz = jnp.concatenate([hy, q, mv, mo, mg, k, v, mq, mk] + gates
                         + [jnp.zeros((D_MODEL, Z_COLS - Z_GATE - 256), w.dtype)], axis=1).astype(bf16)
    gb = jnp.pad(gate_b.reshape(2, 1, 2 * ML_HEADS), ((0, 0), (0, 0), (0, 128 - 2 * ML_HEADS)))
    return wz, gb


def kernel(x, c, ctx, c_ctx, w_mod, b_mod, norm_mix_g, norm_ffn_g, w_in, hy_short_w, hy_short_b, hy_w1, hy_b1, hy_w2, hy_b2, hy_w3, hy_freq, hy_bias, att_sink, ml_gate_b, ml_norm_g, w_branch, w_out, ffn_wg, ffn_wu, ffn_wd, moe_router, moe_wg, moe_wu, moe_wd, final_g):
    xa = jnp.concatenate([x.reshape(NLAT, D_MODEL), ctx.reshape(NCTX, D_MODEL)], axis=0)
    c_all = jnp.concatenate([c, c_ctx[None], jnp.zeros((8 - BATCH - 1, D_MODEL), f32)], axis=0)
    mod = _modulation(c_all, w_mod, b_mod)
    rope_tabs = _rope_tables()
    dft_tabs = _dft_tables(SEQ)
    ctx_tabs = _short_dft_tables(CTX_LEN)
    out = None
    for layer in range(DEPTH):
        last = layer == DEPTH - 1
        modt = mod[layer].reshape(8, 1, 6 * D_MODEL)
        n_mix = NLAT if last else R

        u = _normmod(xa, norm_mix_g[layer], modt, 0, R)
        wz, gate_b = _in_proj_weight(w_in[layer], ml_gate_b[layer])
        z = _in_proj(u, wz)

        hy = (hy_short_w[layer], hy_short_b[layer], hy_w1[layer], hy_b1[layer], hy_w2[layer],
              hy_b2[layer], hy_w3[layer], hy_freq[layer], hy_bias[layer])
        a_rows = _hyena_branch(z, 0, SEQ, dft_tabs, *hy)

        qr, kr = _rope(z, rope_tabs)
        b_rows = _window_attention(qr, kr, z, att_sink[layer])

        hh = _mlstm(z, gate_b)
        c_rows = _mlstm_out(hh, z, ml_norm_g[layer], n_mix)

        if not last:
            a_rows = jnp.concatenate([a_rows, _hyena_branch(z, NLAT, CTX_LEN, ctx_tabs, *hy)], axis=0)
            b_rows = jnp.concatenate([b_rows, _context_attention(z, att_sink[layer])], axis=0)

        ymid = _merge(a_rows, b_rows, c_rows, w_branch[layer].astype(bf16), z, n_mix)
        xa = _mm_resid(ymid, w_out[layer].astype(bf16), xa, modt, 2, n_mix)

        f = _normmod(xa, norm_ffn_g[layer], modt, 3, n_mix, bf16 if layer % 2 == 0 else f32)
        if layer % 2 == 0:
            e = layer // 2
            hmid = _ffn_up(f, ffn_wg[e].astype(bf16), ffn_wu[e].astype(bf16))
            xa = _mm_resid(hmid, ffn_wd[e].astype(bf16), xa, modt, 5, R)
        else:
            e = layer // 2
            top_i, probs = _router(f, moe_router[e])
            slot, slot_tok, blk_expert, n_used, n_rows = _moe_routing(top_i[:, :TOP_K])
            xs = _gather_rows(f, slot_tok, n_rows, bf16)
            hs = _moe_up(xs, moe_wg[e], moe_wu[e], blk_expert, n_used)
            yb = _moe_down(hs, moe_wd[e].astype(bf16), blk_expert, n_used)
            assert last
            out = _combine_final(slot, yb, xa, probs, modt, final_g)
    return out.reshape(BATCH, SEQ, D_MODEL)
```

```python
import functools
import math

import numpy as np
import jax
import jax.numpy as jnp
from jax import lax
from jax.experimental import pallas as pl
from jax.experimental.pallas import tpu as pltpu

f32 = jnp.float32
bf16 = jnp.bfloat16

D_MODEL = 2048
BATCH = 4
SEQ = 4096
DEPTH = 2
GRID_W = 64
CTX_LEN = 256
EPS = 1e-6

HY_WIDTH = D_MODEL // 2
HY_ORDER = 2
HY_BANDS = 16
HY_EMB = 2 * HY_BANDS + 1
HY_FFN = 64
HY_MIN_DECAY = -3.0701134573253943
HY_MAX_DECAY = -15.35056728662697
HY_N2 = 128
HY_SLABS = 4
HY_UNROLL = 8

ATT_HD = 128
ATT_HEADS = 8
ATT_KV_HEADS = 2
ATT_GROUP = 4
ATT_WINDOW = 128
ATT_BLOCK = 128
ROPE_BASE = 10000.0

ML_HEADS = 4
ML_V = 256
ML_QK = 128
ML_CHUNK = 256

N_BRANCH = 3
BRANCH_W = D_MODEL // 2
FFN_DENSE = 5632
N_EXPERTS = 8
TOP_K = 2
FFN_EXPERT = 7168
MOE_TILE = 512

NCTX = BATCH * CTX_LEN
NLAT = BATCH * SEQ
R = NCTX + NLAT

Z_HY = 0
Z_Q = 3072
Z_MV = 4096
Z_MO = 5120
Z_MERGE = 6144
Z_K = 12288
Z_V = 12544
Z_MQ = 12800
Z_MK = 13312
Z_GATE = 13824
Z_COLS = 14336

VMEM_LIMIT = 56 * 1024 * 1024


def _cp(*sem, vmem=VMEM_LIMIT):
    return pltpu.CompilerParams(dimension_semantics=sem, vmem_limit_bytes=vmem)


def _modrow(i, tm):
    return jnp.where(i >= NLAT // tm, BATCH, i // (SEQ // tm))


def _mod_kernel(c_ref, w_ref, b_ref, o_ref):
    c = c_ref[...]
    a = (c * jax.nn.sigmoid(c)).astype(bf16)
    o_ref[...] = jnp.dot(a, w_ref[...].astype(bf16), preferred_element_type=f32) + b_ref[...]


def _modulation(c_all, w_mod, b_mod):
    tn = 1024
    n = 6 * D_MODEL
    return pl.pallas_call(
        _mod_kernel,
        grid=(DEPTH, n // tn),
        in_specs=[pl.BlockSpec((8, D_MODEL), lambda l, j: (0, 0)),
                  pl.BlockSpec((None, D_MODEL, tn), lambda l, j: (l, 0, j)),
                  pl.BlockSpec((None, 1, tn), lambda l, j: (l, 0, j))],
        out_specs=pl.BlockSpec((None, 8, tn), lambda l, j: (l, 0, j)),
        out_shape=jax.ShapeDtypeStruct((DEPTH, 8, n), f32),
        compiler_params=_cp("arbitrary", "arbitrary"),
        name="modulation",
    )(c_all, w_mod, b_mod.reshape(DEPTH, 1, n))


def _normmod_kernel(x_ref, g_ref, sh_ref, sc_ref, o_ref):
    x = x_ref[...]
    y = x * lax.rsqrt(jnp.mean(x * x, axis=-1, keepdims=True) + EPS) * g_ref[...]
    o_ref[...] = (y * (1.0 + sc_ref[...]) + sh_ref[...]).astype(o_ref.dtype)


def _normmod(xa, g, modt, which, nrows, out_dtype=bf16):
    tm = 512
    return pl.pallas_call(
        _normmod_kernel,
        grid=(nrows // tm,),
        in_specs=[pl.BlockSpec((tm, D_MODEL), lambda i: (i, 0)),
                  pl.BlockSpec((1, D_MODEL), lambda i: (0, 0)),
                  pl.BlockSpec((None, 1, D_MODEL), lambda i: (_modrow(i, tm), 0, which)),
                  pl.BlockSpec((None, 1, D_MODEL), lambda i: (_modrow(i, tm), 0, which + 1))],
        out_specs=pl.BlockSpec((tm, D_MODEL), lambda i: (i, 0)),
        out_shape=jax.ShapeDtypeStruct((nrows, D_MODEL), out_dtype),
        compiler_params=_cp("arbitrary"),
        name="normmod",
    )(xa, g.reshape(1, D_MODEL), modt, modt)


def _mm_kernel(a_ref, w_ref, o_ref):
    o_ref[...] = jnp.dot(a_ref[...], w_ref[...], preferred_element_type=f32).astype(o_ref.dtype)


def _in_proj(u, w):
    tm, tn = 1024, 1024
    return pl.pallas_call(
        _mm_kernel,
        grid=(Z_COLS // tn, R // tm),
        in_specs=[pl.BlockSpec((tm, D_MODEL), lambda j, i: (i, 0)),
                  pl.BlockSpec((D_MODEL, tn), lambda j, i: (0, j))],
        out_specs=pl.BlockSpec((tm, tn), lambda j, i: (i, j)),
        out_shape=jax.ShapeDtypeStruct((R, Z_COLS), f32),
        compiler_params=_cp("arbitrary", "arbitrary"),
        name="in_proj",
    )(u, w)


def _mm_resid_kernel(a_ref, w_ref, x_ref, g_ref, o_ref):
    y = jnp.dot(a_ref[...], w_ref[...], preferred_element_type=f32)
    o_ref[...] = x_ref[...] + g_ref[...] * y


def _mm_resid(a, w, xa, modt, which, nrows):
    k = a.shape[1]
    tm, tn = (1024, 1024) if k <= D_MODEL else (512, 1024)
    nj = D_MODEL // tn
    return pl.pallas_call(
        _mm_resid_kernel,
        grid=(nj, nrows // tm),
        in_specs=[pl.BlockSpec((tm, k), lambda j, i: (i, 0)),
                  pl.BlockSpec((k, tn), lambda j, i: (0, j)),
                  pl.BlockSpec((tm, tn), lambda j, i: (i, j)),
                  pl.BlockSpec((None, 1, tn), lambda j, i: (_modrow(i, tm), 0, which * nj + j))],
        out_specs=pl.BlockSpec((tm, tn), lambda j, i: (i, j)),
        out_shape=jax.ShapeDtypeStruct((nrows, D_MODEL), f32),
        compiler_params=_cp("arbitrary", "arbitrary"),
        name="mm_resid",
    )(a, w, xa, modt)


def _swiglu_kernel(a_ref, wg_ref, wu_ref, o_ref):
    a = a_ref[...]
    g = jnp.dot(a, wg_ref[...], preferred_element_type=f32)
    u = jnp.dot(a, wu_ref[...], preferred_element_type=f32)
    o_ref[...] = (g * jax.nn.sigmoid(g) * u).astype(o_ref.dtype)


def _ffn_up(f, wg, wu):
    tm, tn = 1024, 512
    n = wg.shape[1]
    return pl.pallas_call(
        _swiglu_kernel,
        grid=(n // tn, R // tm),
        in_specs=[pl.BlockSpec((tm, D_MODEL), lambda j, i: (i, 0)),
                  pl.BlockSpec((D_MODEL, tn), lambda j, i: (0, j)),
                  pl.BlockSpec((D_MODEL, tn), lambda j, i: (0, j))],
        out_specs=pl.BlockSpec((tm, tn), lambda j, i: (i, j)),
        out_shape=jax.ShapeDtypeStruct((R, n), bf16),
        compiler_params=_cp("arbitrary", "arbitrary"),
        name="ffn_up",
    )(f, wg, wu)


def _merge_kernel(a_ref, b_ref, c_ref, w_ref, g0_ref, g1_ref, g2_ref, o_ref):
    y = jax.nn.sigmoid(g0_ref[...]) * jnp.dot(a_ref[...], w_ref[0], preferred_element_type=f32)
    y += jax.nn.sigmoid(g1_ref[...]) * jnp.dot(b_ref[...], w_ref[1], preferred_element_type=f32)
    y += jax.nn.sigmoid(g2_ref[...]) * jnp.dot(c_ref[...], w_ref[2], preferred_element_type=f32)
    o_ref[...] = y.astype(o_ref.dtype)


def _merge(a, b, c, wb, z, nrows):
    tm, tn = 512, 1024
    nj = D_MODEL // tn
    act = pl.BlockSpec((tm, BRANCH_W), lambda j, i: (i, 0))

    def gate(br):
        return pl.BlockSpec((tm, tn), lambda j, i: (i, (Z_MERGE + br * D_MODEL) // tn + j))

    return pl.pallas_call(
        _merge_kernel,
        grid=(nj, nrows // tm),
        in_specs=[act, act, act,
                  pl.BlockSpec((N_BRANCH, BRANCH_W, tn), lambda j, i: (0, 0, j)),
                  gate(0), gate(1), gate(2)],
        out_specs=pl.BlockSpec((tm, tn), lambda j, i: (i, j)),
        out_shape=jax.ShapeDtypeStruct((nrows, D_MODEL), bf16),
        compiler_params=_cp("arbitrary", "arbitrary"),
        name="merge",
    )(a, b, c, wb, z, z, z)


def _short_conv_kernel(x_ref, xp_ref, xn_ref, w_ref, b_ref, o_ref, *, tiles_per_seq):
    i = pl.program_id(0)
    tm = x_ref.shape[0]
    x = x_ref[...]
    first = i % tiles_per_seq == 0
    last = i % tiles_per_seq == tiles_per_seq - 1
    prev = jnp.where(first, 0.0, xp_ref[7:8, :])
    nxt = jnp.where(last, 0.0, xn_ref[0:1, :])
    row = lax.broadcasted_iota(jnp.int32, x.shape, 0)
    up = jnp.where(row == 0, prev, pltpu.roll(x, 1, 0))
    dn = jnp.where(row == tm - 1, nxt, pltpu.roll(x, tm - 1, 0))
    o_ref[...] = w_ref[0:1, :] * up + w_ref[1:2, :] * x + w_ref[2:3, :] * dn + b_ref[...]


def _short_conv(z, w, b, row0, nrows, seq_len):
    tm, tn = 256, 1536
    t0 = row0 // tm
    nc = 3 * HY_WIDTH
    last8 = R // 8 - 1
    return pl.pallas_call(
        functools.partial(_short_conv_kernel, tiles_per_seq=seq_len // tm),
        grid=(nrows // tm, nc // tn),
        in_specs=[pl.BlockSpec((tm, tn), lambda i, j: (i + t0, j)),
                  pl.BlockSpec((8, tn), lambda i, j: (jnp.maximum((i + t0) * (tm // 8) - 1, 0), j)),
                  pl.BlockSpec((8, tn), lambda i, j: (jnp.minimum((i + t0 + 1) * (tm // 8), last8), j)),
                  pl.BlockSpec((3, tn), lambda i, j: (0, j)),
                  pl.BlockSpec((1, tn), lambda i, j: (0, j))],
        out_specs=pl.BlockSpec((tm, tn), lambda i, j: (i, j)),
        out_shape=jax.ShapeDtypeStruct((nrows, nc), f32),
        compiler_params=_cp("arbitrary", "arbitrary"),
        name="hyena_short_conv",
    )(z, z, z, w, b.reshape(1, nc))


def _hyfilt_kernel(w1_ref, b1_ref, w2_ref, b2_ref, w3_ref, fr_ref, dl_ref, o_ref, *, seq_len):
    tm = o_ref.shape[0]
    r = pl.program_id(0) * tm + lax.broadcasted_iota(jnp.int32, (tm, 1), 0)
    p = jnp.where(r < seq_len, r, 2 * seq_len - r)
    t = p.astype(f32) / seq_len
    lane = lax.broadcasted_iota(jnp.int32, (tm, 128), 1)
    band = jnp.where(lane <= HY_BANDS, lane, lane - HY_BANDS).astype(f32)
    ang = ((2.0 * math.pi) * t) * band
    feats = jnp.where(lane == 0, t,
                      jnp.where(lane <= HY_BANDS, jnp.sin(ang),
                                jnp.where(lane <= 2 * HY_BANDS, jnp.cos(ang), 0.0)))
    h = jnp.dot(feats.astype(bf16), w1_ref[...].astype(bf16), preferred_element_type=f32) + b1_ref[...]
    h = jnp.sin(fr_ref[0:1, :] * h)
    h = jnp.dot(h.astype(bf16), w2_ref[...].astype(bf16), preferred_element_type=f32) + b2_ref[...]
    h = jnp.sin(fr_ref[1:2, :] * h)
    k = jnp.dot(h.astype(bf16), w3_ref[...].astype(bf16), preferred_element_type=f32)
    k = k * jnp.exp(-t * dl_ref[...])
    o_ref[...] = jnp.where(r == seq_len, 0.0, k)


def _hyena_filter(seq_len, w1, b1, w2, b2, w3, freq):
    nc = HY_ORDER * HY_WIDTH
    tm, tn = min(512, seq_len), nc
    deltas = jnp.abs(jnp.linspace(HY_MIN_DECAY, HY_MAX_DECAY, HY_WIDTH, dtype=f32))
    dl = jnp.tile(deltas, HY_ORDER).reshape(1, nc)
    w1p = jnp.pad(w1, ((0, 128 - HY_EMB), (0, 0)))
    per_dir = nc // tn
    full = lambda i, j: (0, 0)
    return pl.pallas_call(
        functools.partial(_hyfilt_kernel, seq_len=seq_len),
        grid=(2 * seq_len // tm, per_dir),
        in_specs=[pl.BlockSpec((128, HY_FFN), full),
                  pl.BlockSpec((1, HY_FFN), full),
                  pl.BlockSpec((HY_FFN, HY_FFN), full),
                  pl.BlockSpec((1, HY_FFN), full),
                  pl.BlockSpec((HY_FFN, tn), lambda i, j: (0, jnp.where(i >= seq_len // tm, per_dir, 0) + j)),
                  pl.BlockSpec((2, HY_FFN), full),
                  pl.BlockSpec((1, tn), lambda i, j: (0, j))],
        out_specs=pl.BlockSpec((tm, tn), lambda i, j: (i, j)),
        out_shape=jax.ShapeDtypeStruct((2 * seq_len, nc), f32),
        compiler_params=_cp("arbitrary", "arbitrary"),
        name="hyena_filter",
    )(w1p, b1.reshape(1, HY_FFN), w2, b2.reshape(1, HY_FFN), w3, freq, dl)


def _pass3(m):
    hi = m.astype(bf16)
    lo = (m - hi.astype(f32)).astype(bf16)
    return jnp.concatenate([hi, hi, lo], axis=-1)


def _rhs3(x):
    hi = x.astype(bf16)
    lo = (x - hi.astype(f32)).astype(bf16)
    return jnp.concatenate([hi, lo, hi], axis=0)


def _stack_complex(ar, ai):
    return jnp.concatenate([jnp.concatenate([ar, -ai], axis=-1),
                            jnp.concatenate([ai, ar], axis=-1)], axis=-2)


def _dft_tables(seq_len):
    n = 2 * seq_len
    n1 = n // HY_N2
    half = n1 // 2
    i1 = jnp.arange(n1, dtype=jnp.int32)
    ang1 = (2.0 * math.pi / n1) * ((i1[:, None] * i1[None, :]) % n1).astype(f32)
    c1, s1 = jnp.cos(ang1), jnp.sin(ang1)
    m2 = _pass3(_stack_complex(c1[:, :half], -s1[:, :half]))
    m2f = _pass3(jnp.concatenate([c1, -s1], axis=0))
    m8 = _pass3(_stack_complex(c1[:half, :] / n, s1[:half, :] / n))
    i2 = jnp.arange(HY_N2, dtype=jnp.int32)
    k = i1[:, None, None] + n1 * i2[None, :, None]
    ang = (2.0 * math.pi / n) * ((i2[None, None, :] * k) % n).astype(f32)
    c, s = jnp.cos(ang), jnp.sin(ang)
    g4 = _pass3(_stack_complex(c, -s))
    ct, st = jnp.swapaxes(c, 1, 2), jnp.swapaxes(s, 1, 2)
    g6 = _pass3(_stack_complex(ct, st))
    return dict(m2=m2, m2f=m2f, m8=m8, g4=g4, g6=g6)


def _store_halves(ref, rows, val):
    ref[0, rows, :] = val[:, :128]
    ref[1, rows, :] = val[:, 128:]


def _load_halves(ref, rows):
    return jnp.concatenate([ref[0, rows, :], ref[1, rows, :]], axis=1)


def _hyfft_kernel(xa_ref, xb_ref, m2_ref, g4_ref, kr_ref, ki_ref, br, bi, *, n1):
    s = pl.program_id(1)

    @pl.when(s == 0)
    def _():
        def body(n2, carry):
            rows = pl.ds(n2, n1, stride=HY_N2)
            rhs = _rhs3(jnp.concatenate([xa_ref[rows, :], xb_ref[rows, :]], axis=1))
            out = jnp.dot(m2_ref[...], rhs, preferred_element_type=f32)
            _store_halves(br, rows, out[:n1])
            _store_halves(bi, rows, out[n1:])
            return carry
        lax.fori_loop(0, HY_N2, body, 0, unroll=HY_UNROLL)

    for j in range(HY_SLABS):
        rows = pl.ds(pl.multiple_of((s * HY_SLABS + j) * HY_N2, HY_N2), HY_N2)
        y = jnp.concatenate([_load_halves(br, rows), _load_halves(bi, rows)], axis=0)
        z = jnp.dot(g4_ref[j], _rhs3(y), preferred_element_type=f32)
        kr_ref[j * HY_N2:(j + 1) * HY_N2, :] = z[:HY_N2]
        ki_ref[j * HY_N2:(j + 1) * HY_N2, :] = z[HY_N2:]


def _hyena_filter_fft(kern, tabs):
    n, nc = kern.shape
    n1 = n // HY_N2
    tn = 256
    sl = HY_SLABS * HY_N2
    spec_out = pl.BlockSpec((sl, tn), lambda t, s: (s, t))
    return pl.pallas_call(
        functools.partial(_hyfft_kernel, n1=n1),
        grid=(nc // tn, n1 // HY_SLABS),
        in_specs=[pl.BlockSpec((n, 128), lambda t, s: (0, 2 * t)),
                  pl.BlockSpec((n, 128), lambda t, s: (0, 2 * t + 1)),
                  pl.BlockSpec(tabs["m2f"].shape, lambda t, s: (0, 0)),
                  pl.BlockSpec((HY_SLABS,) + tabs["g4"].shape[1:], lambda t, s: (s, 0, 0))],
        out_specs=[spec_out, spec_out],
        out_shape=[jax.ShapeDtypeStruct((n, nc), f32)] * 2,
        scratch_shapes=[pltpu.VMEM((2, n, 128), f32), pltpu.VMEM((2, n, 128), f32)],
        compiler_params=_cp("arbitrary", "arbitrary"),
        name="hyena_filter_fft",
    )(kern, kern, tabs["m2f"], tabs["g4"])


def _hyconv_kernel(x_ref, m2_ref, g4_ref, g6_ref, m8_ref, kr_ref, ki_ref, o_ref, br, bi, *, n1):
    s = pl.program_id(1)
    ns = pl.num_programs(1)
    half = n1 // 2
    cw = x_ref.shape[2]

    @pl.when(s == 0)
    def _():
        def body(n2, carry):
            rows = pl.ds(n2, half, stride=HY_N2)
            xr = jnp.concatenate([x_ref[0, rows, :], x_ref[2, rows, :]], axis=1)
            xi = jnp.concatenate([x_ref[1, rows, :], x_ref[3, rows, :]], axis=1)
            rhs = _rhs3(jnp.concatenate([xr, xi], axis=0))
            out = jnp.dot(m2_ref[...], rhs, preferred_element_type=f32)
            brows = pl.ds(n2, n1, stride=HY_N2)
            _store_halves(br, brows, out[:n1])
            _store_halves(bi, brows, out[n1:])
            return carry
        lax.fori_loop(0, HY_N2, body, 0, unroll=HY_UNROLL)

    for j in range(HY_SLABS):
        srows = pl.ds(pl.multiple_of((s * HY_SLABS + j) * HY_N2, HY_N2), HY_N2)
        y = jnp.concatenate([_load_halves(br, srows), _load_halves(bi, srows)], axis=0)
        z = jnp.dot(g4_ref[j], _rhs3(y), preferred_element_type=f32)
        zr, zi = z[:HY_N2], z[HY_N2:]
        kr = jnp.concatenate([kr_ref[j * HY_N2:(j + 1) * HY_N2, :]] * 2, axis=1)
        ki = jnp.concatenate([ki_ref[j * HY_N2:(j + 1) * HY_N2, :]] * 2, axis=1)
        w = jnp.concatenate([zr * kr - zi * ki, zr * ki + zi * kr], axis=0)
        t = jnp.dot(g6_ref[j], _rhs3(w), preferred_element_type=f32)
        _store_halves(br, srows, t[:HY_N2])
        _store_halves(bi, srows, t[HY_N2:])

    @pl.when(s == ns - 1)
    def _():
        def body(n2, carry):
            rows = pl.ds(n2, n1, stride=HY_N2)
            rhs = _rhs3(jnp.concatenate([_load_halves(br, rows), _load_halves(bi, rows)], axis=0))
            out = jnp.dot(m8_ref[...], rhs, preferred_element_type=f32)
            orows = pl.ds(n2, half, stride=HY_N2)
            o_ref[0, orows, :] = out[:half, :cw]
            o_ref[2, orows, :] = out[:half, cw:]
            o_ref[1, orows, :] = out[half:, :cw]
            o_ref[3, orows, :] = out[half:, cw:]
            return carry
        lax.fori_loop(0, HY_N2, body, 0, unroll=HY_UNROLL)


def _hyena_conv(x, col0, khr, khi, order, tabs):
    seq_len = x.shape[1]
    n = 2 * seq_len
    n1 = n // HY_N2
    cw = 128
    sl = HY_SLABS * HY_N2
    nt = HY_WIDTH // cw
    kspec = pl.BlockSpec((sl, cw), lambda t, s: (s, order * nt + t))
    const = lambda a: pl.BlockSpec(a.shape, lambda t, s: (0,) * a.ndim)
    gspec = pl.BlockSpec((HY_SLABS,) + tabs["g4"].shape[1:], lambda t, s: (s, 0, 0))
    return pl.pallas_call(
        functools.partial(_hyconv_kernel, n1=n1),
        grid=(nt, n1 // HY_SLABS),
        in_specs=[pl.BlockSpec((BATCH, seq_len, cw), lambda t, s: (0, 0, col0 // cw + t)),
                  const(tabs["m2"]), gspec, gspec, const(tabs["m8"]), kspec, kspec],
        out_specs=pl.BlockSpec((BATCH, seq_len, cw), lambda t, s: (0, 0, t)),
        out_shape=jax.ShapeDtypeStruct((BATCH, seq_len, HY_WIDTH), f32),
        scratch_shapes=[pltpu.VMEM((2, n, cw), f32), pltpu.VMEM((2, n, cw), f32)],
        compiler_params=_cp("arbitrary", "arbitrary", vmem=60 * 1024 * 1024),
        name="hyena_long_conv",
    )(x, tabs["m2"], tabs["g4"], tabs["g6"], tabs["m8"], khr, khi)


def _hygate_kernel(c_ref, y_ref, g_ref, b_ref, o_ref):
    y = y_ref[...]
    o_ref[...] = (g_ref[...] * (c_ref[...] + b_ref[...] * y)).astype(o_ref.dtype)


def _hyena_gate(conv, y, ycol0, zc, gcol0, bias, out_dtype):
    tm, tn = 512, 512
    nrows = conv.shape[0]
    return pl.pallas_call(
        _hygate_kernel,
        grid=(nrows // tm, HY_WIDTH // tn),
        in_specs=[pl.BlockSpec((tm, tn), lambda i, j: (i, j)),
                  pl.BlockSpec((tm, tn), lambda i, j: (i, ycol0 // tn + j)),
                  pl.BlockSpec((tm, tn), lambda i, j: (i, gcol0 // tn + j)),
                  pl.BlockSpec((1, tn), lambda i, j: (0, j))],
        out_specs=pl.BlockSpec((tm, tn), lambda i, j: (i, j)),
        out_shape=jax.ShapeDtypeStruct((nrows, HY_WIDTH), out_dtype),
        compiler_params=_cp("arbitrary", "arbitrary"),
        name="hyena_gate",
    )(conv, y, zc, bias.reshape(1, HY_WIDTH))


def _short_dft_tables(seq_len):
    n = 2 * seq_len
    k = jnp.arange(n, dtype=jnp.int32)
    ang = (2.0 * math.pi / n) * ((k[:, None] * k[None, :]) % n).astype(f32)
    c, s = jnp.cos(ang), jnp.sin(ang)
    mx = _pass3(_stack_complex(c[:, :seq_len], -s[:, :seq_len]))
    mf = _pass3(jnp.concatenate([c, -s], axis=0))
    mi = _pass3(_stack_complex(c[:seq_len, :] / n, s[:seq_len, :] / n))
    return dict(mx=mx, mf=mf, mi=mi)


def _short_fft_kernel(x_ref, mf_ref, kr_ref, ki_ref):
    n = x_ref.shape[0]
    z = jnp.dot(mf_ref[...], _rhs3(x_ref[...]), preferred_element_type=f32)
    kr_ref[...] = z[:n]
    ki_ref[...] = z[n:]


def _short_filter_fft(kern, tabs):
    n, nc = kern.shape
    tn = 256
    spec = pl.BlockSpec((n, tn), lambda t: (0, t))
    return pl.pallas_call(
        _short_fft_kernel,
        grid=(nc // tn,),
        in_specs=[spec, pl.BlockSpec(tabs["mf"].shape, lambda t: (0, 0))],
        out_specs=[spec, spec],
        out_shape=[jax.ShapeDtypeStruct((n, nc), f32)] * 2,
        compiler_params=_cp("arbitrary"),
        name="hyena_short_filter_fft",
    )(kern, tabs["mf"])


def _short_conv_fft_kernel(x_ref, mx_ref, mi_ref, kr_ref, ki_ref, o_ref):
    seq_len = x_ref.shape[1]
    cw = x_ref.shape[2]
    n = 2 * seq_len
    xr = jnp.concatenate([x_ref[0], x_ref[2]], axis=1)
    xi = jnp.concatenate([x_ref[1], x_ref[3]], axis=1)
    z = jnp.dot(mx_ref[...], _rhs3(jnp.concatenate([xr, xi], axis=0)), preferred_element_type=f32)
    zr, zi = z[:n], z[n:]
    kr = jnp.concatenate([kr_ref[...]] * 2, axis=1)
    ki = jnp.concatenate([ki_ref[...]] * 2, axis=1)
    w = jnp.concatenate([zr * kr - zi * ki, zr * ki + zi * kr], axis=0)
    y = jnp.dot(mi_ref[...], _rhs3(w), preferred_element_type=f32)
    o_ref[0] = y[:seq_len, :cw]
    o_ref[2] = y[:seq_len, cw:]
    o_ref[1] = y[seq_len:, :cw]
    o_ref[3] = y[seq_len:, cw:]


def _short_long_conv(x, col0, khr, khi, order, tabs):
    seq_len = x.shape[1]
    n = 2 * seq_len
    cw = 128
    nt = HY_WIDTH // cw
    kspec = pl.BlockSpec((n, cw), lambda t: (0, order * nt + t))
    const = lambda a: pl.BlockSpec(a.shape, lambda t: (0,) * a.ndim)
    return pl.pallas_call(
        _short_conv_fft_kernel,
        grid=(nt,),
        in_specs=[pl.BlockSpec((BATCH, seq_len, cw), lambda t: (0, 0, col0 // cw + t)),
                  const(tabs["mx"]), const(tabs["mi"]), kspec, kspec],
        out_specs=pl.BlockSpec((BATCH, seq_len, cw), lambda t: (0, 0, t)),
        out_shape=jax.ShapeDtypeStruct((BATCH, seq_len, HY_WIDTH), f32),
        compiler_params=_cp("arbitrary"),
        name="hyena_short_long_conv",
    )(x, tabs["mx"], tabs["mi"], khr, khi)


def _hyena_branch(z, row0, seq_len, tabs, short_w, short_b, w1, b1, w2, b2, w3, freq, bias):
    two_stage = "g4" in tabs
    conv = _hyena_conv if two_stage else _short_long_conv
    nrows = BATCH * seq_len
    zc = _short_conv(z, short_w, short_b, row0, nrows, seq_len)
    kern = _hyena_filter(seq_len, w1, b1, w2, b2, w3, freq)
    khr, khi = (_hyena_filter_fft if two_stage else _short_filter_fft)(kern, tabs)
    zc3 = zc.reshape(BATCH, seq_len, 3 * HY_WIDTH)
    c1 = conv(zc3, 0, khr, khi, 0, tabs).reshape(nrows, HY_WIDTH)
    y1 = _hyena_gate(c1, zc, 0, zc, HY_WIDTH, bias[0], f32)
    c2 = conv(y1.reshape(BATCH, seq_len, HY_WIDTH), 0, khr, khi, 1, tabs).reshape(nrows, HY_WIDTH)
    return _hyena_gate(c2, y1, 0, zc, 2 * HY_WIDTH, bias[1], bf16)


def _rope_tables():
    half = ATT_HD // 2
    nf = half // 2
    inv = ROPE_BASE ** (-jnp.arange(nf, dtype=f32) / nf)
    pos = jnp.arange(SEQ)
    rows = (pos // GRID_W).astype(f32)[:, None] * inv[None, :]
    cols = (pos % GRID_W).astype(f32)[:, None] * inv[None, :]
    zero = jnp.zeros_like(rows)
    cos = jnp.concatenate([jnp.cos(rows)] * 2 + [jnp.cos(cols)] * 2, axis=-1)
    sin_up = jnp.concatenate([-jnp.sin(rows), zero, -jnp.sin(cols), zero], axis=-1)
    sin_dn = jnp.concatenate([zero, jnp.sin(rows), zero, jnp.sin(cols)], axis=-1)
    return cos, sin_up, sin_dn


def _rope_kernel(q_ref, k_ref, cos_ref, su_ref, sd_ref, qo_ref, ko_ref):
    cos, su, sd = cos_ref[...], su_ref[...], sd_ref[...]

    def rot(x):
        return x * cos + pltpu.roll(x, 96, 1) * su + pltpu.roll(x, 32, 1) * sd

    for h in range(ATT_HEADS):
        s = slice(h * ATT_HD, (h + 1) * ATT_HD)
        qo_ref[:, s] = rot(q_ref[:, s]).astype(qo_ref.dtype)
    for h in range(ATT_KV_HEADS):
        s = slice(h * ATT_HD, (h + 1) * ATT_HD)
        ko_ref[:, s] = rot(k_ref[:, s]).astype(ko_ref.dtype)


def _rope(z, tables):
    tm = 512
    nq = ATT_HEADS * ATT_HD
    nk = ATT_KV_HEADS * ATT_HD
    tab = pl.BlockSpec((tm, ATT_HD), lambda i: (i % (SEQ // tm), 0))
    return pl.pallas_call(
        _rope_kernel,
        grid=(NLAT // tm,),
        in_specs=[pl.BlockSpec((tm, nq), lambda i: (i, Z_Q // nq)),
                  pl.BlockSpec((tm, nk), lambda i: (i, Z_K // nk)),
                  tab, tab, tab],
        out_specs=[pl.BlockSpec((tm, nq), lambda i: (i, 0)),
                   pl.BlockSpec((tm, nk), lambda i: (i, 0))],
        out_shape=[jax.ShapeDtypeStruct((NLAT, nq), bf16), jax.ShapeDtypeStruct((NLAT, nk), bf16)],
        compiler_params=_cp("arbitrary"),
        name="rope",
    )(z, z, *tables)


_ATT_SCALE = ATT_HD ** -0.5
_NEG = float(np.finfo(np.float32).min)
_NT = (((1,), (1,)), ((), ()))


def _sink_column(sink_ref, h, rows):
    rg = lax.broadcasted_iota(jnp.int32, (rows * ATT_GROUP, 1), 0) // rows
    col = jnp.full((rows * ATT_GROUP, 1), sink_ref[h * ATT_GROUP + ATT_GROUP - 1], f32)
    for g in range(ATT_GROUP - 2, -1, -1):
        col = jnp.where(rg == g, sink_ref[h * ATT_GROUP + g], col)
    return col


def _attn_kernel(sink_ref, q_ref, kp_ref, kc_ref, kn_ref, vp_ref, vc_ref, vn_ref, kx_ref, vx_ref, o_ref):
    i = pl.program_id(1)
    nb = pl.num_programs(1)
    blk = ATT_BLOCK
    r = lax.broadcasted_iota(jnp.int32, (ATT_GROUP * blk, 3 * blk), 0) % blk
    c = lax.broadcasted_iota(jnp.int32, (ATT_GROUP * blk, 3 * blk), 1)
    lo = jnp.where(i > 0, 0, blk)
    hi = jnp.where(i < nb - 1, 3 * blk, 2 * blk)
    valid = (c >= r) & (c <= r + 2 * ATT_WINDOW) & (c >= lo) & (c < hi)
    for h in range(ATT_KV_HEADS):
        hs = slice(h * ATT_HD, (h + 1) * ATT_HD)
        k_win = jnp.concatenate([kp_ref[:, hs], kc_ref[:, hs], kn_ref[:, hs]], axis=0)
        v_win = jnp.concatenate([vp_ref[:, hs], vc_ref[:, hs], vn_ref[:, hs]], axis=0).astype(bf16)
        k_ctx = kx_ref[:, hs].astype(bf16)
        v_ctx = vx_ref[:, hs].astype(bf16)
        q = jnp.concatenate([q_ref[:, (h * ATT_GROUP + g) * ATT_HD:(h * ATT_GROUP + g + 1) * ATT_HD]
                             for g in range(ATT_GROUP)], axis=0)
        s_win = lax.dot_general(q, k_win, _NT, preferred_element_type=f32) * _ATT_SCALE
        s_win = jnp.where(valid, s_win, _NEG)
        s_ctx = lax.dot_general(q, k_ctx, _NT, preferred_element_type=f32) * _ATT_SCALE
        sink = _sink_column(sink_ref, h, blk)
        m = jnp.maximum(jnp.maximum(jnp.max(s_win, axis=-1, keepdims=True),
                                    jnp.max(s_ctx, axis=-1, keepdims=True)), sink)
        p_win = jnp.exp(s_win - m)
        p_ctx = jnp.exp(s_ctx - m)
        den = (jnp.sum(p_win, axis=-1, keepdims=True) + jnp.sum(p_ctx, axis=-1, keepdims=True)
               + jnp.exp(sink - m))
        o = (jnp.dot(p_win.astype(bf16), v_win, preferred_element_type=f32)
             + jnp.dot(p_ctx.astype(bf16), v_ctx, preferred_element_type=f32)) / den
        for g in range(ATT_GROUP):
            cs = slice((h * ATT_GROUP + g) * ATT_HD, (h * ATT_GROUP + g + 1) * ATT_HD)
            o_ref[:, cs] = o[g * blk:(g + 1) * blk].astype(o_ref.dtype)


def _window_attention(qr, kr, z, sink):
    blk = ATT_BLOCK
    nb = SEQ // blk
    nkv = ATT_KV_HEADS * ATT_HD
    cx = NLAT // CTX_LEN

    def krow(off):
        return lambda b, i: (b * nb + jnp.clip(i + off, 0, nb - 1), 0)

    def vrow(off):
        return lambda b, i: (b * nb + jnp.clip(i + off, 0, nb - 1), Z_V // nkv)

    return pl.pallas_call(
        _attn_kernel,
        grid=(BATCH, nb),
        in_specs=[pl.BlockSpec(memory_space=pltpu.SMEM),
                  pl.BlockSpec((blk, ATT_HEADS * ATT_HD), lambda b, i: (b * nb + i, 0)),
                  pl.BlockSpec((blk, nkv), krow(-1)),
                  pl.BlockSpec((blk, nkv), krow(0)),
                  pl.BlockSpec((blk, nkv), krow(1)),
                  pl.BlockSpec((blk, nkv), vrow(-1)),
                  pl.BlockSpec((blk, nkv), vrow(0)),
                  pl.BlockSpec((blk, nkv), vrow(1)),
                  pl.BlockSpec((CTX_LEN, nkv), lambda b, i: (cx + b, Z_K // nkv)),
                  pl.BlockSpec((CTX_LEN, nkv), lambda b, i: (cx + b, Z_V // nkv))],
        out_specs=pl.BlockSpec((blk, ATT_HEADS * ATT_HD), lambda b, i: (b * nb + i, 0)),
        out_shape=jax.ShapeDtypeStruct((NLAT, ATT_HEADS * ATT_HD), bf16),
        compiler_params=_cp("arbitrary", "arbitrary"),
        name="window_attention",
    )(sink, qr, kr, kr, kr, z, z, z, z, z)


def _ctx_attn_kernel(sink_ref, q_ref, k_ref, v_ref, o_ref):
    for h in range(ATT_KV_HEADS):
        hs = slice(h * ATT_HD, (h + 1) * ATT_HD)
        k = k_ref[:, hs].astype(bf16)
        v = v_ref[:, hs].astype(bf16)
        q = jnp.concatenate([q_ref[:, (h * ATT_GROUP + g) * ATT_HD:(h * ATT_GROUP + g + 1) * ATT_HD]
                             for g in range(ATT_GROUP)], axis=0).astype(bf16)
        s = lax.dot_general(q, k, _NT, preferred_element_type=f32) * _ATT_SCALE
        sink = _sink_column(sink_ref, h, CTX_LEN)
        m = jnp.maximum(jnp.max(s, axis=-1, keepdims=True), sink)
        p = jnp.exp(s - m)
        den = jnp.sum(p, axis=-1, keepdims=True) + jnp.exp(sink - m)
        o = jnp.dot(p.astype(bf16), v, preferred_element_type=f32) / den
        for g in range(ATT_GROUP):
            cs = slice((h * ATT_GROUP + g) * ATT_HD, (h * ATT_GROUP + g + 1) * ATT_HD)
            o_ref[:, cs] = o[g * CTX_LEN:(g + 1) * CTX_LEN].astype(o_ref.dtype)


def _context_attention(z, sink):
    nq = ATT_HEADS * ATT_HD
    nkv = ATT_KV_HEADS * ATT_HD
    cx = NLAT // CTX_LEN
    return pl.pallas_call(
        _ctx_attn_kernel,
        grid=(BATCH,),
        in_specs=[pl.BlockSpec(memory_space=pltpu.SMEM),
                  pl.BlockSpec((CTX_LEN, nq), lambda b: (cx + b, Z_Q // nq)),
                  pl.BlockSpec((CTX_LEN, nkv), lambda b: (cx + b, Z_K // nkv)),
                  pl.BlockSpec((CTX_LEN, nkv), lambda b: (cx + b, Z_V // nkv))],
        out_specs=pl.BlockSpec((CTX_LEN, nq), lambda b: (b, 0)),
        out_shape=jax.ShapeDtypeStruct((NCTX, nq), bf16),
        compiler_params=_cp("arbitrary"),
        name="context_attention",
    )(sink, z, z, z)


_ML_SCALE = ML_QK ** -0.5


def _split3(x):
    x1 = x.astype(bf16)
    r1 = x - x1.astype(f32)
    x2 = r1.astype(bf16)
    x3 = (r1 - x2.astype(f32)).astype(bf16)
    return x1, x2, x3


def _mlstm_kernel(q_ref, k_ref, v_ref, g_ref, gb_ref, o_ref, ct_ref, n_ref, m_ref):
    d = pl.program_id(1)
    c = pl.program_id(2)
    ch = ML_CHUNK

    @pl.when(c == 0)
    def _():
        ct_ref[...] = jnp.zeros_like(ct_ref)
        n_ref[...] = jnp.zeros_like(n_ref)
        m_ref[...] = jnp.zeros_like(m_ref)

    g = g_ref[...] + gb_ref[...]
    row = lax.broadcasted_iota(jnp.int32, (ch, ch), 0)
    col = lax.broadcasted_iota(jnp.int32, (ch, ch), 1)
    tri = (row - col) * (1 - 2 * d) >= 0
    tri_b = tri.astype(f32).astype(bf16)
    lf = jax.nn.log_sigmoid(g)
    l1, l2, l3 = _split3(lf)
    bcol = (jnp.dot(tri_b, l1, preferred_element_type=f32)
            + jnp.dot(tri_b, l2, preferred_element_type=f32)
            + jnp.dot(tri_b, l3, preferred_element_type=f32))
    bend = jnp.where(d == 0, bcol[ch - 1:ch, :], bcol[0:1, :])
    g_t = g.T
    b_t = bcol.T

    for h in range(ML_HEADS):
        qf = q_ref[:, h * ML_QK:(h + 1) * ML_QK] * _ML_SCALE
        kf = k_ref[:, h * ML_QK:(h + 1) * ML_QK]
        vf = v_ref[:, h * ML_V:(h + 1) * ML_V]
        q = qf.astype(bf16)
        k = kf.astype(bf16)
        li_c = g[:, h:h + 1]
        b_c = bcol[:, ML_HEADS + h:ML_HEADS + h + 1]
        li_r = g_t[h:h + 1, :]
        b_r = b_t[ML_HEADS + h:ML_HEADS + h + 1, :]
        m_prev = m_ref[h][:, 0:1]
        dmat = jnp.where(tri, b_c - b_r + li_r, -jnp.inf)
        inter = b_c + m_prev
        m_t = jnp.maximum(inter, jnp.max(dmat, axis=-1, keepdims=True))
        w_intra = jnp.exp(dmat - m_t)
        w_inter = jnp.exp(inter - m_t)
        s = lax.dot_general(q, k, _NT, preferred_element_type=f32) * w_intra
        qc = jnp.dot(q, ct_ref[h].astype(bf16), preferred_element_type=f32)
        num = jnp.dot(s.astype(bf16), vf.astype(bf16), preferred_element_type=f32) + w_inter * qc
        den = (jnp.sum(s, axis=-1, keepdims=True)
               + w_inter * jnp.sum(qf * n_ref[h], axis=-1, keepdims=True))
        o_ref[:, h * ML_V:(h + 1) * ML_V] = num / jnp.maximum(jnp.abs(den), jnp.exp(-m_t))

        b_e = bend[:, ML_HEADS + h:ML_HEADS + h + 1]
        g_c = b_e - b_c + li_c
        m_new = jnp.maximum(b_e + m_prev, jnp.max(g_c, axis=0, keepdims=True))
        w_s = jnp.exp(g_c - m_new)
        w_c = jnp.exp(b_e + m_prev - m_new)
        vw = (vf * w_s).astype(bf16)
        ct_ref[h] = w_c * ct_ref[h] + jnp.dot(kf.T.astype(bf16), vw, preferred_element_type=f32)
        n_ref[h] = w_c * n_ref[h] + jnp.sum(kf * w_s, axis=0, keepdims=True)
        m_ref[h] = jnp.broadcast_to(m_new, (1, 128))


def _mlstm(z, gate_b):
    ch = ML_CHUNK
    ncl = SEQ // ch
    nsteps = ncl + CTX_LEN // ch
    assert CTX_LEN == ch

    def rt(b, d, c):
        lat = b * ncl + jnp.where(d == 0, c - 1, ncl - c)
        return jnp.where(c == 0, NLAT // ch + b, lat)

    nq = ML_HEADS * ML_QK
    nv = ML_HEADS * ML_V
    return pl.pallas_call(
        _mlstm_kernel,
        grid=(BATCH, 2, nsteps),
        in_specs=[pl.BlockSpec((ch, nq), lambda b, d, c: (rt(b, d, c), Z_MQ // nq)),
                  pl.BlockSpec((ch, nq), lambda b, d, c: (rt(b, d, c), Z_MK // nq)),
                  pl.BlockSpec((ch, nv), lambda b, d, c: (rt(b, d, c), Z_MV // nv)),
                  pl.BlockSpec((ch, 128), lambda b, d, c: (rt(b, d, c), Z_GATE // 128 + d)),
                  pl.BlockSpec((None, 1, 128), lambda b, d, c: (d, 0, 0))],
        out_specs=pl.BlockSpec((None, ch, nv), lambda b, d, c: (d, rt(b, d, c), 0)),
        out_shape=jax.ShapeDtypeStruct((2, R, nv), f32),
        scratch_shapes=[pltpu.VMEM((ML_HEADS, ML_QK, ML_V), f32),
                        pltpu.VMEM((ML_HEADS, 1, ML_QK), f32),
                        pltpu.VMEM((ML_HEADS, 1, 128), f32)],
        compiler_params=_cp("arbitrary", "arbitrary", "arbitrary"),
        name="mlstm",
    )(z, z, z, z, gate_b)


def _mlstm_out_kernel(h_ref, zo_ref, g_ref, o_ref):
    for h in range(ML_HEADS):
        s = slice(h * ML_V, (h + 1) * ML_V)
        x = h_ref[0, :, s] + h_ref[1, :, s]
        xn = x * lax.rsqrt(jnp.mean(x * x, axis=-1, keepdims=True) + EPS) * g_ref[:, s]
        o_ref[:, s] = (xn * jax.nn.sigmoid(zo_ref[:, s])).astype(o_ref.dtype)


def _mlstm_out(hh, z, g, nrows):
    tm = 512
    nv = ML_HEADS * ML_V
    return pl.pallas_call(
        _mlstm_out_kernel,
        grid=(nrows // tm,),
        in_specs=[pl.BlockSpec((2, tm, nv), lambda i: (0, i, 0)),
                  pl.BlockSpec((tm, nv), lambda i: (i, Z_MO // nv)),
                  pl.BlockSpec((1, nv), lambda i: (0, 0))],
        out_specs=pl.BlockSpec((tm, nv), lambda i: (i, 0)),
        out_shape=jax.ShapeDtypeStruct((nrows, nv), bf16),
        compiler_params=_cp("arbitrary"),
        name="mlstm_out",
    )(hh, z, g.reshape(1, nv))


def _router_kernel(f_ref, w_ref, idx_ref, p_ref):
    logits = jnp.dot(f_ref[...].astype(bf16), w_ref[...], preferred_element_type=f32)
    lane = lax.broadcasted_iota(jnp.int32, logits.shape, 1).astype(f32)
    logits = jnp.where(lane < N_EXPERTS, logits, -jnp.inf)
    v1 = jnp.max(logits, axis=-1, keepdims=True)
    i1 = jnp.min(jnp.where(logits == v1, lane, 128.0), axis=-1, keepdims=True)
    rest = jnp.where(lane == i1, -jnp.inf, logits)
    v2 = jnp.max(rest, axis=-1, keepdims=True)
    i2 = jnp.min(jnp.where(rest == v2, lane, 128.0), axis=-1, keepdims=True)
    e = jnp.exp(v2 - v1)
    p1 = 1.0 / (1.0 + e)
    p2 = e / (1.0 + e)
    idx_ref[...] = jnp.where(lane == 0, i1, jnp.where(lane == 1, i2, 0.0)).astype(jnp.int32)
    p_ref[...] = jnp.where(lane == 0, p1, jnp.where(lane == 1, p2, 0.0))


def _router(f, w_router):
    tm = 512
    w = jnp.pad(w_router, ((0, 0), (0, 128 - N_EXPERTS))).astype(bf16)
    return pl.pallas_call(
        _router_kernel,
        grid=(NLAT // tm,),
        in_specs=[pl.BlockSpec((tm, D_MODEL), lambda i: (i, 0)),
                  pl.BlockSpec((D_MODEL, 128), lambda i: (0, 0))],
        out_specs=[pl.BlockSpec((tm, 128), lambda i: (i, 0)),
                   pl.BlockSpec((tm, 128), lambda i: (i, 0))],
        out_shape=[jax.ShapeDtypeStruct((NLAT, 128), jnp.int32),
                   jax.ShapeDtypeStruct((NLAT, 128), f32)],
        compiler_params=_cp("arbitrary"),
        name="router",
    )(f, w)


def _gather_kernel(tok_ref, src_ref, o_ref, buf, sem):
    base = pl.program_id(0) * MOE_TILE

    def copy(r):
        return pltpu.make_async_copy(src_ref.at[pl.ds(tok_ref[base + r], 1)], buf.at[pl.ds(r, 1)], sem)

    def start(r, carry):
        copy(r).start()
        return carry

    def wait(r, carry):
        copy(r).wait()
        return carry

    lax.fori_loop(0, MOE_TILE, start, 0, unroll=8)
    lax.fori_loop(0, MOE_TILE, wait, 0, unroll=8)
    o_ref[...] = buf[...].astype(o_ref.dtype)


def _gather_rows(src, rows, n_out, out_dtype):
    width = src.shape[1]
    return pl.pallas_call(
        _gather_kernel,
        grid_spec=pltpu.PrefetchScalarGridSpec(
            num_scalar_prefetch=1,
            grid=(n_out // MOE_TILE,),
            in_specs=[pl.BlockSpec(memory_space=pl.ANY)],
            out_specs=pl.BlockSpec((MOE_TILE, width), lambda i, tok: (i, 0)),
            scratch_shapes=[pltpu.VMEM((MOE_TILE, width), src.dtype), pltpu.SemaphoreType.DMA(())]),
        out_shape=jax.ShapeDtypeStruct((n_out, width), out_dtype),
        compiler_params=_cp("arbitrary"),
        name="gather_rows",
    )(rows, src)


def _moe_up_kernel(be_ref, nu_ref, a_ref, wg_ref, wu_ref, o_ref, wg_bf, wu_bf):
    i = pl.program_id(1)
    used = i < nu_ref[0]
    fresh = jnp.logical_or(i == 0, be_ref[i] != be_ref[jnp.maximum(i - 1, 0)])

    @pl.when(jnp.logical_and(used, fresh))
    def _():
        wg_bf[...] = wg_ref[...].astype(bf16)
        wu_bf[...] = wu_ref[...].astype(bf16)

    @pl.when(used)
    def _():
        a = a_ref[...]
        g = jnp.dot(a, wg_bf[...], preferred_element_type=f32)
        u = jnp.dot(a, wu_bf[...], preferred_element_type=f32)
        o_ref[...] = (g * jax.nn.sigmoid(g) * u).astype(o_ref.dtype)

    @pl.when(i >= nu_ref[0])
    def _():
        o_ref[...] = jnp.zeros_like(o_ref)


def _moe_up(xs, wg, wu, blk_expert, n_used):
    tn = 1024
    nblk = xs.shape[0] // MOE_TILE

    def row(j, i, be, nu):
        return (jnp.minimum(i, nu[0] - 1), 0)

    def wmap(j, i, be, nu):
        return (be[jnp.minimum(i, nu[0] - 1)], 0, j)

    return pl.pallas_call(
        _moe_up_kernel,
        grid_spec=pltpu.PrefetchScalarGridSpec(
            num_scalar_prefetch=2,
            grid=(FFN_EXPERT // tn, nblk),
            in_specs=[pl.BlockSpec((MOE_TILE, D_MODEL), row),
                      pl.BlockSpec((None, D_MODEL, tn), wmap),
                      pl.BlockSpec((None, D_MODEL, tn), wmap)],
            out_specs=pl.BlockSpec((MOE_TILE, tn), lambda j, i, be, nu: (i, j)),
            scratch_shapes=[pltpu.VMEM((D_MODEL, tn), bf16), pltpu.VMEM((D_MODEL, tn), bf16)]),
        out_shape=jax.ShapeDtypeStruct((xs.shape[0], FFN_EXPERT), bf16),
        compiler_params=_cp("arbitrary", "arbitrary"),
        name="moe_up",
    )(blk_expert, n_used, xs, wg, wu)


def _moe_down_kernel(be_ref, nu_ref, a_ref, w_ref, o_ref):
    i = pl.program_id(1)

    @pl.when(i < nu_ref[0])
    def _():
        o_ref[...] = jnp.dot(a_ref[...], w_ref[...], preferred_element_type=f32)

    @pl.when(i >= nu_ref[0])
    def _():
        o_ref[...] = jnp.zeros_like(o_ref)


def _moe_down(hs, wd, blk_expert, n_used):
    tn = 1024
    nblk = hs.shape[0] // MOE_TILE

    def row(j, i, be, nu):
        return (jnp.minimum(i, nu[0] - 1), 0)

    def wmap(j, i, be, nu):
        return (be[jnp.minimum(i, nu[0] - 1)], 0, j)

    return pl.pallas_call(
        _moe_down_kernel,
        grid_spec=pltpu.PrefetchScalarGridSpec(
            num_scalar_prefetch=2,
            grid=(D_MODEL // tn, nblk),
            in_specs=[pl.BlockSpec((MOE_TILE, FFN_EXPERT), row),
                      pl.BlockSpec((None, FFN_EXPERT, tn), wmap)],
            out_specs=pl.BlockSpec((MOE_TILE, tn), lambda j, i, be, nu: (i, j))),
        out_shape=jax.ShapeDtypeStruct((hs.shape[0], D_MODEL), f32),
        compiler_params=_cp("arbitrary", "arbitrary"),
        name="moe_down",
    )(blk_expert, n_used, hs, wd)


_COMBINE_TM = 256


def _combine_kernel(slot_ref, yb_ref, x_ref, p_ref, g2_ref, fg_ref, o_ref, buf, sem):
    i = pl.program_id(0)
    cur = i % 2

    def copy(step, b, r, k):
        s = slot_ref[(step * _COMBINE_TM + r) * TOP_K + k]
        return pltpu.make_async_copy(yb_ref.at[pl.ds(s, 1)], buf.at[b, k, pl.ds(r, 1)], sem.at[b])

    def fetch(step, b):
        def start(r, carry):
            copy(step, b, r, 0).start()
            copy(step, b, r, 1).start()
            return carry
        lax.fori_loop(0, _COMBINE_TM, start, 0, unroll=4)

    @pl.when(i == 0)
    def _():
        fetch(0, 0)

    @pl.when(i + 1 < pl.num_programs(0))
    def _():
        fetch(i + 1, 1 - cur)

    def wait(r, carry):
        copy(i, cur, r, 0).wait()
        copy(i, cur, r, 1).wait()
        return carry

    lax.fori_loop(0, _COMBINE_TM, wait, 0, unroll=8)
    p = p_ref[...]
    y = buf[cur, 0] * p[:, 0:1] + buf[cur, 1] * p[:, 1:2]
    x = x_ref[...] + g2_ref[...] * y
    o_ref[...] = x * lax.rsqrt(jnp.mean(x * x, axis=-1, keepdims=True) + EPS) * fg_ref[...]


def _combine_final(slot, yb, xa, probs, modt, final_g):
    tm = _COMBINE_TM
    return pl.pallas_call(
        _combine_kernel,
        grid_spec=pltpu.PrefetchScalarGridSpec(
            num_scalar_prefetch=1,
            grid=(NLAT // tm,),
            in_specs=[pl.BlockSpec(memory_space=pl.ANY),
                      pl.BlockSpec((tm, D_MODEL), lambda i, s: (i, 0)),
                      pl.BlockSpec((tm, 128), lambda i, s: (i, 0)),
                      pl.BlockSpec((None, 1, D_MODEL), lambda i, s: (i // (SEQ // tm), 0, 5)),
                      pl.BlockSpec((1, D_MODEL), lambda i, s: (0, 0))],
            out_specs=pl.BlockSpec((tm, D_MODEL), lambda i, s: (i, 0)),
            scratch_shapes=[pltpu.VMEM((2, TOP_K, tm, D_MODEL), f32),
                            pltpu.SemaphoreType.DMA((2,))]),
        out_shape=jax.ShapeDtypeStruct((NLAT, D_MODEL), f32),
        compiler_params=_cp("arbitrary"),
        name="moe_combine_final_norm",
    )(slot, yb, xa, probs, modt, final_g.reshape(1, D_MODEL))


def _moe_routing(top_i):
    a = NLAT * TOP_K
    e_flat = top_i.reshape(a)
    onehot = (e_flat[:, None] == jnp.arange(N_EXPERTS)[None, :]).astype(jnp.int32)
    csum = jnp.cumsum(onehot, axis=0)
    rank = jnp.sum(onehot * csum, axis=1) - 1
    counts = csum[-1]
    padded = (counts + MOE_TILE - 1) // MOE_TILE * MOE_TILE
    pad_end = jnp.cumsum(padded)
    pad_start = pad_end - padded
    slot = (pad_start[e_flat] + rank).astype(jnp.int32)
    n_rows = a + N_EXPERTS * MOE_TILE
    nblk = n_rows // MOE_TILE
    slot_tok = jnp.zeros((n_rows,), jnp.int32).at[slot].set(jnp.arange(a, dtype=jnp.int32) // TOP_K)
    blk_expert = jnp.minimum(jnp.searchsorted(pad_end, jnp.arange(nblk) * MOE_TILE, side='right'),
                             N_EXPERTS - 1).astype(jnp.int32)
    n_used = (pad_end[-1:] // MOE_TILE).astype(jnp.int32)
    return slot, slot_tok, blk_expert, n_used, n_rows


def _in_proj_weight(w, gate_b):
    o = np.cumsum((0,) + (3072, 1024, 256, 256, 512, 512, 1024, 1024, 16, 6144))
    hy, q, k, v, mq, mk, mv, mo, gt, mg = [w[:, o[i]:o[i + 1]] for i in range(10)]
    pad = jnp.zeros((D_MODEL, 128 - 2 * ML_HEADS), w.dtype)
    gates = [jnp.concatenate([gt[:, 8 * d:8 * d + 8], pad], axis=1) for d in range(2)]
    wz = jnp.concatenate([hy, q, mv, mo, mg, k, v, mq, mk] + gates
                         + [jnp.zeros((D_MODEL, Z_COLS - Z_GATE - 256), w.dtype)], axis=1).astype(bf16)
    gb = jnp.pad(gate_b.reshape(2, 1, 2 * ML_HEADS), ((0, 0), (0, 0), (0, 128 - 2 * ML_HEADS)))
    return wz, gb


def kernel(x, c, ctx, c_ctx, w_mod, b_mod, norm_mix_g, norm_ffn_g, w_in, hy_short_w, hy_short_b, hy_w1, hy_b1, hy_w2, hy_b2, hy_w3, hy_freq, hy_bias, att_sink, ml_gate_b, ml_norm_g, w_branch, w_out, ffn_wg, ffn_wu, ffn_wd, moe_router, moe_wg, moe_wu, moe_wd, final_g):
    xa = jnp.concatenate([x.reshape(NLAT, D_MODEL), ctx.reshape(NCTX, D_MODEL)], axis=0)
    c_all = jnp.concatenate([c, c_ctx[None], jnp.zeros((8 - BATCH - 1, D_MODEL), f32)], axis=0)
    mod = _modulation(c_all, w_mod, b_mod)
    rope_tabs = _rope_tables()
    dft_tabs = _dft_tables(SEQ)
    ctx_tabs = _short_dft_tables(CTX_LEN)
    out = None
    for layer in range(DEPTH):
        last = layer == DEPTH - 1
        modt = mod[layer].reshape(8, 1, 6 * D_MODEL)
        n_mix = NLAT if last else R

        u = _normmod(xa, norm_mix_g[layer], modt, 0, R)
        wz, gate_b = _in_proj_weight(w_in[layer], ml_gate_b[layer])
        z = _in_proj(u, wz)

        hy = (hy_short_w[layer], hy_short_b[layer], hy_w1[layer], hy_b1[layer], hy_w2[layer],
              hy_b2[layer], hy_w3[layer], hy_freq[layer], hy_bias[layer])
        a_rows = _hyena_branch(z, 0, SEQ, dft_tabs, *hy)

        qr, kr = _rope(z, rope_tabs)
        b_rows = _window_attention(qr, kr, z, att_sink[layer])

        hh = _mlstm(z, gate_b)
        c_rows = _mlstm_out(hh, z, ml_norm_g[layer], n_mix)

        if not last:
            a_rows = jnp.concatenate([a_rows, _hyena_branch(z, NLAT, CTX_LEN, ctx_tabs, *hy)], axis=0)
            b_rows = jnp.concatenate([b_rows, _context_attention(z, att_sink[layer])], axis=0)

        ymid = _merge(a_rows, b_rows, c_rows, w_branch[layer].astype(bf16), z, n_mix)
        xa = _mm_resid(ymid, w_out[layer].astype(bf16), xa, modt, 2, n_mix)

        f = _normmod(xa, norm_ffn_g[layer], modt, 3, n_mix, bf16 if layer % 2 == 0 else f32)
        if layer % 2 == 0:
            e = layer // 2
            hmid = _ffn_up(f, ffn_wg[e].astype(bf16), ffn_wu[e].astype(bf16))
            xa = _mm_resid(hmid, ffn_wd[e].astype(bf16), xa, modt, 5, R)
        else:
            e = layer // 2
            top_i, probs = _router(f, moe_router[e])
            slot, slot_tok, blk_expert, n_used, n_rows = _moe_routing(top_i[:, :TOP_K])
            xs = _gather_rows(f, slot_tok, n_rows, bf16)
            hs = _moe_up(xs, moe_wg[e], moe_wu[e], blk_expert, n_used)
            yb = _moe_down(hs, moe_wd[e].astype(bf16), blk_expert, n_used)
            assert last
            out = _combine_final(slot, yb, xa, probs, modt, final_g)
    return out.reshape(BATCH, SEQ, D_MODEL)
```

```python
import functools
import math

import numpy as np
import jax
import jax.numpy as jnp
from jax import lax
from jax.experimental import pallas as pl
from jax.experimental.pallas import tpu as pltpu

f32 = jnp.float32
bf16 = jnp.bfloat16

D_MODEL = 2048
BATCH = 4
SEQ = 4096
DEPTH = 2
GRID_W = 64
CTX_LEN = 256
EPS = 1e-6

HY_WIDTH = D_MODEL // 2
HY_ORDER = 2
HY_BANDS = 16
HY_EMB = 2 * HY_BANDS + 1
HY_FFN = 64
HY_MIN_DECAY = -3.0701134573253943
HY_MAX_DECAY = -15.35056728662697
HY_N2 = 128
HY_SLABS = 8
HY_UNROLL = 8

ATT_HD = 128
ATT_HEADS = 8
ATT_KV_HEADS = 2
ATT_GROUP = 4
ATT_WINDOW = 128
ATT_BLOCK = 128
ROPE_BASE = 10000.0

ML_HEADS = 4
ML_V = 256
ML_QK = 128
ML_CHUNK = 256

N_BRANCH = 3
BRANCH_W = D_MODEL // 2
FFN_DENSE = 5632
N_EXPERTS = 8
TOP_K = 2
FFN_EXPERT = 7168
MOE_TILE = 512

NCTX = BATCH * CTX_LEN
NLAT = BATCH * SEQ
R = NCTX + NLAT

Z_HY = 0
Z_Q = 3072
Z_MV = 4096
Z_MO = 5120
Z_MERGE = 6144
Z_K = 12288
Z_V = 12544
Z_MQ = 12800
Z_MK = 13312
Z_GATE = 13824
Z_COLS = 14080

VMEM_LIMIT = 56 * 1024 * 1024


def _cp(*sem, vmem=VMEM_LIMIT):
    return pltpu.CompilerParams(dimension_semantics=sem, vmem_limit_bytes=vmem)


def _modrow(i, tm):
    return jnp.where(i >= NLAT // tm, BATCH, i // (SEQ // tm))


def _mod_kernel(c_ref, w_ref, b_ref, o_ref):
    c = c_ref[...]
    a = (c * jax.nn.sigmoid(c)).astype(bf16)
    o_ref[...] = jnp.dot(a, w_ref[...].astype(bf16), preferred_element_type=f32) + b_ref[...]


def _modulation(c_all, w_mod, b_mod):
    tn = 1024
    n = 6 * D_MODEL
    return pl.pallas_call(
        _mod_kernel,
        grid=(DEPTH, n // tn),
        in_specs=[pl.BlockSpec((8, D_MODEL), lambda l, j: (0, 0)),
                  pl.BlockSpec((None, D_MODEL, tn), lambda l, j: (l, 0, j)),
                  pl.BlockSpec((None, 1, tn), lambda l, j: (l, 0, j))],
        out_specs=pl.BlockSpec((None, 8, tn), lambda l, j: (l, 0, j)),
        out_shape=jax.ShapeDtypeStruct((DEPTH, 8, n), f32),
        compiler_params=_cp("arbitrary", "arbitrary"),
        name="modulation",
    )(c_all, w_mod, b_mod.reshape(DEPTH, 1, n))


def _normmod_kernel(x_ref, g_ref, sh_ref, sc_ref, o_ref):
    x = x_ref[...]
    y = x * lax.rsqrt(jnp.mean(x * x, axis=-1, keepdims=True) + EPS) * g_ref[...]
    o_ref[...] = (y * (1.0 + sc_ref[...]) + sh_ref[...]).astype(o_ref.dtype)


def _normmod(xa, g, modt, which, nrows, out_dtype=bf16):
    tm = 512
    return pl.pallas_call(
        _normmod_kernel,
        grid=(nrows // tm,),
        in_specs=[pl.BlockSpec((tm, D_MODEL), lambda i: (i, 0)),
                  pl.BlockSpec((1, D_MODEL), lambda i: (0, 0)),
                  pl.BlockSpec((None, 1, D_MODEL), lambda i: (_modrow(i, tm), 0, which)),
                  pl.BlockSpec((None, 1, D_MODEL), lambda i: (_modrow(i, tm), 0, which + 1))],
        out_specs=pl.BlockSpec((tm, D_MODEL), lambda i: (i, 0)),
        out_shape=jax.ShapeDtypeStruct((nrows, D_MODEL), out_dtype),
        compiler_params=_cp("arbitrary"),
        name="normmod",
    )(xa, g.reshape(1, D_MODEL), modt, modt)


def _mm_kernel(a_ref, w_ref, o_ref):
    o_ref[...] = jnp.dot(a_ref[...], w_ref[...], preferred_element_type=f32).astype(o_ref.dtype)


def _in_proj(u, w):
    tm, tn = 1024, 1280
    return pl.pallas_call(
        _mm_kernel,
        grid=(Z_COLS // tn, R // tm),
        in_specs=[pl.BlockSpec((tm, D_MODEL), lambda j, i: (i, 0)),
                  pl.BlockSpec((D_MODEL, tn), lambda j, i: (0, j))],
        out_specs=pl.BlockSpec((tm, tn), lambda j, i: (i, j)),
        out_shape=jax.ShapeDtypeStruct((R, Z_COLS), f32),
        compiler_params=_cp("arbitrary", "arbitrary"),
        name="in_proj",
    )(u, w)


def _mm_resid_kernel(a_ref, w_ref, x_ref, g_ref, o_ref):
    y = jnp.dot(a_ref[...], w_ref[...], preferred_element_type=f32)
    o_ref[...] = x_ref[...] + g_ref[...] * y


def _mm_resid(a, w, xa, modt, which, nrows):
    k = a.shape[1]
    tm, tn = (1024, 1024) if k <= D_MODEL else (512, 1024)
    nj = D_MODEL // tn
    return pl.pallas_call(
        _mm_resid_kernel,
        grid=(nj, nrows // tm),
        in_specs=[pl.BlockSpec((tm, k), lambda j, i: (i, 0)),
                  pl.BlockSpec((k, tn), lambda j, i: (0, j)),
                  pl.BlockSpec((tm, tn), lambda j, i: (i, j)),
                  pl.BlockSpec((None, 1, tn), lambda j, i: (_modrow(i, tm), 0, which * nj + j))],
        out_specs=pl.BlockSpec((tm, tn), lambda j, i: (i, j)),
        out_shape=jax.ShapeDtypeStruct((nrows, D_MODEL), f32),
        compiler_params=_cp("arbitrary", "arbitrary"),
        name="mm_resid",
    )(a, w, xa, modt)


def _swiglu_kernel(a_ref, wg_ref, wu_ref, o_ref):
    a = a_ref[...]
    g = jnp.dot(a, wg_ref[...], preferred_element_type=f32)
    u = jnp.dot(a, wu_ref[...], preferred_element_type=f32)
    o_ref[...] = (g * jax.nn.sigmoid(g) * u).astype(o_ref.dtype)


def _ffn_up(f, wg, wu):
    tm, tn = 1024, 512
    n = wg.shape[1]
    return pl.pallas_call(
        _swiglu_kernel,
        grid=(n // tn, R // tm),
        in_specs=[pl.BlockSpec((tm, D_MODEL), lambda j, i: (i, 0)),
                  pl.BlockSpec((D_MODEL, tn), lambda j, i: (0, j)),
                  pl.BlockSpec((D_MODEL, tn), lambda j, i: (0, j))],
        out_specs=pl.BlockSpec((tm, tn), lambda j, i: (i, j)),
        out_shape=jax.ShapeDtypeStruct((R, n), bf16),
        compiler_params=_cp("arbitrary", "arbitrary"),
        name="ffn_up",
    )(f, wg, wu)


def _merge_kernel(a_ref, b_ref, c_ref, w_ref, g0_ref, g1_ref, g2_ref, o_ref):
    y = jax.nn.sigmoid(g0_ref[...]) * jnp.dot(a_ref[...], w_ref[0], preferred_element_type=f32)
    y += jax.nn.sigmoid(g1_ref[...]) * jnp.dot(b_ref[...], w_ref[1], preferred_element_type=f32)
    y += jax.nn.sigmoid(g2_ref[...]) * jnp.dot(c_ref[...], w_ref[2], preferred_element_type=f32)
    o_ref[...] = y.astype(o_ref.dtype)


def _merge(a, b, c, wb, z, nrows):
    tm, tn = 512, 1024
    nj = D_MODEL // tn
    act = pl.BlockSpec((tm, BRANCH_W), lambda j, i: (i, 0))

    def gate(br):
        return pl.BlockSpec((tm, tn), lambda j, i: (i, (Z_MERGE + br * D_MODEL) // tn + j))

    return pl.pallas_call(
        _merge_kernel,
        grid=(nj, nrows // tm),
        in_specs=[act, act, act,
                  pl.BlockSpec((N_BRANCH, BRANCH_W, tn), lambda j, i: (0, 0, j)),
                  gate(0), gate(1), gate(2)],
        out_specs=pl.BlockSpec((tm, tn), lambda j, i: (i, j)),
        out_shape=jax.ShapeDtypeStruct((nrows, D_MODEL), bf16),
        compiler_params=_cp("arbitrary", "arbitrary"),
        name="merge",
    )(a, b, c, wb, z, z, z)


def _short_conv_kernel(x_ref, xp_ref, xn_ref, w_ref, b_ref, o_ref, *, tiles_per_seq):
    i = pl.program_id(0)
    tm = x_ref.shape[0]
    x = x_ref[...]
    first = i % tiles_per_seq == 0
    last = i % tiles_per_seq == tiles_per_seq - 1
    prev = jnp.where(first, 0.0, xp_ref[7:8, :])
    nxt = jnp.where(last, 0.0, xn_ref[0:1, :])
    row = lax.broadcasted_iota(jnp.int32, x.shape, 0)
    up = jnp.where(row == 0, prev, pltpu.roll(x, 1, 0))
    dn = jnp.where(row == tm - 1, nxt, pltpu.roll(x, tm - 1, 0))
    o_ref[...] = w_ref[0:1, :] * up + w_ref[1:2, :] * x + w_ref[2:3, :] * dn + b_ref[...]


def _short_conv(z, w, b, row0, nrows, seq_len):
    tm, tn = 256, 1536
    t0 = row0 // tm
    nc = 3 * HY_WIDTH
    last8 = R // 8 - 1
    return pl.pallas_call(
        functools.partial(_short_conv_kernel, tiles_per_seq=seq_len // tm),
        grid=(nrows // tm, nc // tn),
        in_specs=[pl.BlockSpec((tm, tn), lambda i, j: (i + t0, j)),
                  pl.BlockSpec((8, tn), lambda i, j: (jnp.maximum((i + t0) * (tm // 8) - 1, 0), j)),
                  pl.BlockSpec((8, tn), lambda i, j: (jnp.minimum((i + t0 + 1) * (tm // 8), last8), j)),
                  pl.BlockSpec((3, tn), lambda i, j: (0, j)),
                  pl.BlockSpec((1, tn), lambda i, j: (0, j))],
        out_specs=pl.BlockSpec((tm, tn), lambda i, j: (i, j)),
        out_shape=jax.ShapeDtypeStruct((nrows, nc), f32),
        compiler_params=_cp("arbitrary", "arbitrary"),
        name="hyena_short_conv",
    )(z, z, z, w, b.reshape(1, nc))


def _hyfilt_kernel(w1_ref, b1_ref, w2_ref, b2_ref, w3_ref, fr_ref, dl_ref, o_ref, *, seq_len):
    tm = o_ref.shape[0]
    r = pl.program_id(0) * tm + lax.broadcasted_iota(jnp.int32, (tm, 1), 0)
    p = jnp.where(r < seq_len, r, 2 * seq_len - r)
    t = p.astype(f32) / seq_len
    lane = lax.broadcasted_iota(jnp.int32, (tm, 128), 1)
    band = jnp.where(lane <= HY_BANDS, lane, lane - HY_BANDS).astype(f32)
    ang = ((2.0 * math.pi) * t) * band
    feats = jnp.where(lane == 0, t,
                      jnp.where(lane <= HY_BANDS, jnp.sin(ang),
                                jnp.where(lane <= 2 * HY_BANDS, jnp.cos(ang), 0.0)))
    h = jnp.dot(feats.astype(bf16), w1_ref[...].astype(bf16), preferred_element_type=f32) + b1_ref[...]
    h = jnp.sin(fr_ref[0:1, :] * h)
    h = jnp.dot(h.astype(bf16), w2_ref[...].astype(bf16), preferred_element_type=f32) + b2_ref[...]
    h = jnp.sin(fr_ref[1:2, :] * h)
    k = jnp.dot(h.astype(bf16), w3_ref[...].astype(bf16), preferred_element_type=f32)
    k = k * jnp.exp(-t * dl_ref[...])
    o_ref[...] = jnp.where(r == seq_len, 0.0, k)


def _hyena_filter(seq_len, w1, b1, w2, b2, w3, freq):
    nc = HY_ORDER * HY_WIDTH
    tm, tn = min(512, seq_len), nc
    deltas = jnp.abs(jnp.linspace(HY_MIN_DECAY, HY_MAX_DECAY, HY_WIDTH, dtype=f32))
    dl = jnp.tile(deltas, HY_ORDER).reshape(1, nc)
    w1p = jnp.pad(w1, ((0, 128 - HY_EMB), (0, 0)))
    per_dir = nc // tn
    full = lambda i, j: (0, 0)
    return pl.pallas_call(
        functools.partial(_hyfilt_kernel, seq_len=seq_len),
        grid=(2 * seq_len // tm, per_dir),
        in_specs=[pl.BlockSpec((128, HY_FFN), full),
                  pl.BlockSpec((1, HY_FFN), full),
                  pl.BlockSpec((HY_FFN, HY_FFN), full),
                  pl.BlockSpec((1, HY_FFN), full),
                  pl.BlockSpec((HY_FFN, tn), lambda i, j: (0, jnp.where(i >= seq_len // tm, per_dir, 0) + j)),
                  pl.BlockSpec((2, HY_FFN), full),
                  pl.BlockSpec((1, tn), lambda i, j: (0, j))],
        out_specs=pl.BlockSpec((tm, tn), lambda i, j: (i, j)),
        out_shape=jax.ShapeDtypeStruct((2 * seq_len, nc), f32),
        compiler_params=_cp("arbitrary", "arbitrary"),
        name="hyena_filter",
    )(w1p, b1.reshape(1, HY_FFN), w2, b2.reshape(1, HY_FFN), w3, freq, dl)


def _pass3(m):
    hi = m.astype(bf16)
    lo = (m - hi.astype(f32)).astype(bf16)
    return jnp.concatenate([hi, hi, lo], axis=-1)


def _rhs3(x):
    hi = x.astype(bf16)
    lo = (x - hi.astype(f32)).astype(bf16)
    return jnp.concatenate([hi, lo, hi], axis=0)


def _stack_complex(ar, ai):
    return jnp.concatenate([jnp.concatenate([ar, -ai], axis=-1),
                            jnp.concatenate([ai, ar], axis=-1)], axis=-2)


def _dft_tables(seq_len):
    n = 2 * seq_len
    n1 = n // HY_N2
    half = n1 // 2
    i1 = jnp.arange(n1, dtype=jnp.int32)
    ang1 = (2.0 * math.pi / n1) * ((i1[:, None] * i1[None, :]) % n1).astype(f32)
    c1, s1 = jnp.cos(ang1), jnp.sin(ang1)
    m2 = _pass3(_stack_complex(c1[:, :half], -s1[:, :half]))
    m2f = _pass3(jnp.concatenate([c1, -s1], axis=0))
    m8 = _pass3(_stack_complex(c1[:half, :] / n, s1[:half, :] / n))
    i2 = jnp.arange(HY_N2, dtype=jnp.int32)
    k = i1[:, None, None] + n1 * i2[None, :, None]
    ang = (2.0 * math.pi / n) * ((i2[None, None, :] * k) % n).astype(f32)
    c, s = jnp.cos(ang), jnp.sin(ang)
    g4 = _pass3(_stack_complex(c, -s))
    ct, st = jnp.swapaxes(c, 1, 2), jnp.swapaxes(s, 1, 2)
    g6 = _pass3(_stack_complex(ct, st))
    return dict(m2=m2, m2f=m2f, m8=m8, g4=g4, g6=g6)


def _store_halves(ref, rows, val):
    ref[0, rows, :] = val[:, :128]
    ref[1, rows, :] = val[:, 128:]


def _load_halves(ref, rows):
    return jnp.concatenate([ref[0, rows, :], ref[1, rows, :]], axis=1)


def _hyfft_kernel(xa_ref, xb_ref, m2_ref, g4_ref, kr_ref, ki_ref, br, bi, *, n1):
    s = pl.program_id(1)

    @pl.when(s == 0)
    def _():
        def body(n2, carry):
            rows = pl.ds(n2, n1, stride=HY_N2)
            rhs = _rhs3(jnp.concatenate([xa_ref[rows, :], xb_ref[rows, :]], axis=1))
            out = jnp.dot(m2_ref[...], rhs, preferred_element_type=f32)
            _store_halves(br, rows, out[:n1])
            _store_halves(bi, rows, out[n1:])
            return carry
        lax.fori_loop(0, HY_N2, body, 0, unroll=HY_UNROLL)

    for j in range(HY_SLABS):
        rows = pl.ds(pl.multiple_of((s * HY_SLABS + j) * HY_N2, HY_N2), HY_N2)
        y = jnp.concatenate([_load_halves(br, rows), _load_halves(bi, rows)], axis=0)
        z = jnp.dot(g4_ref[j], _rhs3(y), preferred_element_type=f32)
        kr_ref[j * HY_N2:(j + 1) * HY_N2, :] = z[:HY_N2]
        ki_ref[j * HY_N2:(j + 1) * HY_N2, :] = z[HY_N2:]


def _hyena_filter_fft(kern, tabs):
    n, nc = kern.shape
    n1 = n // HY_N2
    tn = 256
    sl = HY_SLABS * HY_N2
    spec_out = pl.BlockSpec((sl, tn), lambda t, s: (s, t))
    return pl.pallas_call(
        functools.partial(_hyfft_kernel, n1=n1),
        grid=(nc // tn, n1 // HY_SLABS),
        in_specs=[pl.BlockSpec((n, 128), lambda t, s: (0, 2 * t)),
                  pl.BlockSpec((n, 128), lambda t, s: (0, 2 * t + 1)),
                  pl.BlockSpec(tabs["m2f"].shape, lambda t, s: (0, 0)),
                  pl.BlockSpec((HY_SLABS,) + tabs["g4"].shape[1:], lambda t, s: (s, 0, 0))],
        out_specs=[spec_out, spec_out],
        out_shape=[jax.ShapeDtypeStruct((n, nc), f32)] * 2,
        scratch_shapes=[pltpu.VMEM((2, n, 128), f32), pltpu.VMEM((2, n, 128), f32)],
        compiler_params=_cp("arbitrary", "arbitrary"),
        name="hyena_filter_fft",
    )(kern, kern, tabs["m2f"], tabs["g4"])


def _hyconv_kernel(x_ref, m2_ref, g4_ref, g6_ref, m8_ref, kr_ref, ki_ref, o_ref, br, bi, *, n1):
    s = pl.program_id(1)
    ns = pl.num_programs(1)
    half = n1 // 2
    cw = x_ref.shape[2]

    @pl.when(s == 0)
    def _():
        def body(n2, carry):
            rows = pl.ds(n2, half, stride=HY_N2)
            xr = jnp.concatenate([x_ref[0, rows, :], x_ref[2, rows, :]], axis=1)
            xi = jnp.concatenate([x_ref[1, rows, :], x_ref[3, rows, :]], axis=1)
            rhs = _rhs3(jnp.concatenate([xr, xi], axis=0))
            out = jnp.dot(m2_ref[...], rhs, preferred_element_type=f32)
            brows = pl.ds(n2, n1, stride=HY_N2)
            _store_halves(br, brows, out[:n1])
            _store_halves(bi, brows, out[n1:])
            return carry
        lax.fori_loop(0, HY_N2, body, 0, unroll=HY_UNROLL)

    for j in range(HY_SLABS):
        srows = pl.ds(pl.multiple_of((s * HY_SLABS + j) * HY_N2, HY_N2), HY_N2)
        y = jnp.concatenate([_load_halves(br, srows), _load_halves(bi, srows)], axis=0)
        z = jnp.dot(g4_ref[j], _rhs3(y), preferred_element_type=f32)
        zr, zi = z[:HY_N2], z[HY_N2:]
        kr = jnp.concatenate([kr_ref[j * HY_N2:(j + 1) * HY_N2, :]] * 2, axis=1)
        ki = jnp.concatenate([ki_ref[j * HY_N2:(j + 1) * HY_N2, :]] * 2, axis=1)
        w = jnp.concatenate([zr * kr - zi * ki, zr * ki + zi * kr], axis=0)
        t = jnp.dot(g6_ref[j], _rhs3(w), preferred_element_type=f32)
        _store_halves(br, srows, t[:HY_N2])
        _store_halves(bi, srows, t[HY_N2:])

    @pl.when(s == ns - 1)
    def _():
        def body(n2, carry):
            rows = pl.ds(n2, n1, stride=HY_N2)
            rhs = _rhs3(jnp.concatenate([_load_halves(br, rows), _load_halves(bi, rows)], axis=0))
            out = jnp.dot(m8_ref[...], rhs, preferred_element_type=f32)
            orows = pl.ds(n2, half, stride=HY_N2)
            o_ref[0, orows, :] = out[:half, :cw]
            o_ref[2, orows, :] = out[:half, cw:]
            o_ref[1, orows, :] = out[half:, :cw]
            o_ref[3, orows, :] = out[half:, cw:]
            return carry
        lax.fori_loop(0, HY_N2, body, 0, unroll=HY_UNROLL)


def _hyena_conv(x, col0, khr, khi, order, tabs):
    seq_len = x.shape[1]
    n = 2 * seq_len
    n1 = n // HY_N2
    cw = 128
    sl = HY_SLABS * HY_N2
    nt = HY_WIDTH // cw
    kspec = pl.BlockSpec((sl, cw), lambda t, s: (s, order * nt + t))
    const = lambda a: pl.BlockSpec(a.shape, lambda t, s: (0,) * a.ndim)
    gspec = pl.BlockSpec((HY_SLABS,) + tabs["g4"].shape[1:], lambda t, s: (s, 0, 0))
    return pl.pallas_call(
        functools.partial(_hyconv_kernel, n1=n1),
        grid=(nt, n1 // HY_SLABS),
        in_specs=[pl.BlockSpec((BATCH, seq_len, cw), lambda t, s: (0, 0, col0 // cw + t)),
                  const(tabs["m2"]), gspec, gspec, const(tabs["m8"]), kspec, kspec],
        out_specs=pl.BlockSpec((BATCH, seq_len, cw), lambda t, s: (0, 0, t), pipeline_mode=pl.Buffered(1)),
        out_shape=jax.ShapeDtypeStruct((BATCH, seq_len, HY_WIDTH), f32),
        scratch_shapes=[pltpu.VMEM((2, n, cw), f32), pltpu.VMEM((2, n, cw), f32)],
        compiler_params=_cp("arbitrary", "arbitrary", vmem=60 * 1024 * 1024),
        name="hyena_long_conv",
    )(x, tabs["m2"], tabs["g4"], tabs["g6"], tabs["m8"], khr, khi)


def _hygate_kernel(c_ref, y_ref, g_ref, b_ref, o_ref):
    y = y_ref[...]
    o_ref[...] = (g_ref[...] * (c_ref[...] + b_ref[...] * y)).astype(o_ref.dtype)


def _hyena_gate(conv, y, ycol0, zc, gcol0, bias, out_dtype):
    tm, tn = 512, 512
    nrows = conv.shape[0]
    return pl.pallas_call(
        _hygate_kernel,
        grid=(nrows // tm, HY_WIDTH // tn),
        in_specs=[pl.BlockSpec((tm, tn), lambda i, j: (i, j)),
                  pl.BlockSpec((tm, tn), lambda i, j: (i, ycol0 // tn + j)),
                  pl.BlockSpec((tm, tn), lambda i, j: (i, gcol0 // tn + j)),
                  pl.BlockSpec((1, tn), lambda i, j: (0, j))],
        out_specs=pl.BlockSpec((tm, tn), lambda i, j: (i, j)),
        out_shape=jax.ShapeDtypeStruct((nrows, HY_WIDTH), out_dtype),
        compiler_params=_cp("arbitrary", "arbitrary"),
        name="hyena_gate",
    )(conv, y, zc, bias.reshape(1, HY_WIDTH))


def _short_dft_tables(seq_len):
    n = 2 * seq_len
    k = jnp.arange(n, dtype=jnp.int32)
    ang = (2.0 * math.pi / n) * ((k[:, None] * k[None, :]) % n).astype(f32)
    c, s = jnp.cos(ang), jnp.sin(ang)
    mx = _pass3(_stack_complex(c[:, :seq_len], -s[:, :seq_len]))
    mf = _pass3(jnp.concatenate([c, -s], axis=0))
    mi = _pass3(_stack_complex(c[:seq_len, :] / n, s[:seq_len, :] / n))
    return dict(mx=mx, mf=mf, mi=mi)


def _short_fft_kernel(x_ref, mf_ref, kr_ref, ki_ref):
    n = x_ref.shape[0]
    z = jnp.dot(mf_ref[...], _rhs3(x_ref[...]), preferred_element_type=f32)
    kr_ref[...] = z[:n]
    ki_ref[...] = z[n:]


def _short_filter_fft(kern, tabs):
    n, nc = kern.shape
    tn = 256
    spec = pl.BlockSpec((n, tn), lambda t: (0, t))
    return pl.pallas_call(
        _short_fft_kernel,
        grid=(nc // tn,),
        in_specs=[spec, pl.BlockSpec(tabs["mf"].shape, lambda t: (0, 0))],
        out_specs=[spec, spec],
        out_shape=[jax.ShapeDtypeStruct((n, nc), f32)] * 2,
        compiler_params=_cp("arbitrary"),
        name="hyena_short_filter_fft",
    )(kern, tabs["mf"])


def _short_conv_fft_kernel(x_ref, mx_ref, mi_ref, kr_ref, ki_ref, o_ref):
    seq_len = x_ref.shape[1]
    cw = x_ref.shape[2]
    n = 2 * seq_len
    xr = jnp.concatenate([x_ref[0], x_ref[2]], axis=1)
    xi = jnp.concatenate([x_ref[1], x_ref[3]], axis=1)
    z = jnp.dot(mx_ref[...], _rhs3(jnp.concatenate([xr, xi], axis=0)), preferred_element_type=f32)
    zr, zi = z[:n], z[n:]
    kr = jnp.concatenate([kr_ref[...]] * 2, axis=1)
    ki = jnp.concatenate([ki_ref[...]] * 2, axis=1)
    w = jnp.concatenate([zr * kr - zi * ki, zr * ki + zi * kr], axis=0)
    y = jnp.dot(mi_ref[...], _rhs3(w), preferred_element_type=f32)
    o_ref[0] = y[:seq_len, :cw]
    o_ref[2] = y[:seq_len, cw:]
    o_ref[1] = y[seq_len:, :cw]
    o_ref[3] = y[seq_len:, cw:]


def _short_long_conv(x, col0, khr, khi, order, tabs):
    seq_len = x.shape[1]
    n = 2 * seq_len
    cw = 128
    nt = HY_WIDTH // cw
    kspec = pl.BlockSpec((n, cw), lambda t: (0, order * nt + t))
    const = lambda a: pl.BlockSpec(a.shape, lambda t: (0,) * a.ndim)
    return pl.pallas_call(
        _short_conv_fft_kernel,
        grid=(nt,),
        in_specs=[pl.BlockSpec((BATCH, seq_len, cw), lambda t: (0, 0, col0 // cw + t)),
                  const(tabs["mx"]), const(tabs["mi"]), kspec, kspec],
        out_specs=pl.BlockSpec((BATCH, seq_len, cw), lambda t: (0, 0, t)),
        out_shape=jax.ShapeDtypeStruct((BATCH, seq_len, HY_WIDTH), f32),
        compiler_params=_cp("arbitrary"),
        name="hyena_short_long_conv",
    )(x, tabs["mx"], tabs["mi"], khr, khi)


def _hyena_branch(z, row0, seq_len, tabs, short_w, short_b, w1, b1, w2, b2, w3, freq, bias):
    two_stage = "g4" in tabs
    conv = _hyena_conv if two_stage else _short_long_conv
    nrows = BATCH * seq_len
    zc = _short_conv(z, short_w, short_b, row0, nrows, seq_len)
    kern = _hyena_filter(seq_len, w1, b1, w2, b2, w3, freq)
    khr, khi = (_hyena_filter_fft if two_stage else _short_filter_fft)(kern, tabs)
    zc3 = zc.reshape(BATCH, seq_len, 3 * HY_WIDTH)
    c1 = conv(zc3, 0, khr, khi, 0, tabs).reshape(nrows, HY_WIDTH)
    y1 = _hyena_gate(c1, zc, 0, zc, HY_WIDTH, bias[0], f32)
    c2 = conv(y1.reshape(BATCH, seq_len, HY_WIDTH), 0, khr, khi, 1, tabs).reshape(nrows, HY_WIDTH)
    return _hyena_gate(c2, y1, 0, zc, 2 * HY_WIDTH, bias[1], bf16)


def _rope_tables():
    half = ATT_HD // 2
    nf = half // 2
    inv = ROPE_BASE ** (-jnp.arange(nf, dtype=f32) / nf)
    pos = jnp.arange(SEQ)
    rows = (pos // GRID_W).astype(f32)[:, None] * inv[None, :]
    cols = (pos % GRID_W).astype(f32)[:, None] * inv[None, :]
    zero = jnp.zeros_like(rows)
    cos = jnp.concatenate([jnp.cos(rows)] * 2 + [jnp.cos(cols)] * 2, axis=-1)
    sin_up = jnp.concatenate([-jnp.sin(rows), zero, -jnp.sin(cols), zero], axis=-1)
    sin_dn = jnp.concatenate([zero, jnp.sin(rows), zero, jnp.sin(cols)], axis=-1)
    return cos, sin_up, sin_dn


def _rope_kernel(q_ref, k_ref, cos_ref, su_ref, sd_ref, qo_ref, ko_ref):
    cos, su, sd = cos_ref[...], su_ref[...], sd_ref[...]

    def rot(x):
        return x * cos + pltpu.roll(x, 96, 1) * su + pltpu.roll(x, 32, 1) * sd

    for h in range(ATT_HEADS):
        s = slice(h * ATT_HD, (h + 1) * ATT_HD)
        qo_ref[:, s] = rot(q_ref[:, s]).astype(qo_ref.dtype)
    for h in range(ATT_KV_HEADS):
        s = slice(h * ATT_HD, (h + 1) * ATT_HD)
        ko_ref[:, s] = rot(k_ref[:, s]).astype(ko_ref.dtype)


def _rope(z, tables):
    tm = 512
    nq = ATT_HEADS * ATT_HD
    nk = ATT_KV_HEADS * ATT_HD
    tab = pl.BlockSpec((tm, ATT_HD), lambda i: (i % (SEQ // tm), 0))
    return pl.pallas_call(
        _rope_kernel,
        grid=(NLAT // tm,),
        in_specs=[pl.BlockSpec((tm, nq), lambda i: (i, Z_Q // nq)),
                  pl.BlockSpec((tm, nk), lambda i: (i, Z_K // nk)),
                  tab, tab, tab],
        out_specs=[pl.BlockSpec((tm, nq), lambda i: (i, 0)),
                   pl.BlockSpec((tm, nk), lambda i: (i, 0))],
        out_shape=[jax.ShapeDtypeStruct((NLAT, nq), bf16), jax.ShapeDtypeStruct((NLAT, nk), bf16)],
        compiler_params=_cp("arbitrary"),
        name="rope",
    )(z, z, *tables)


_ATT_SCALE = ATT_HD ** -0.5
_NEG = float(np.finfo(np.float32).min)
_NT = (((1,), (1,)), ((), ()))


def _sink_column(sink_ref, h, rows):
    rg = lax.broadcasted_iota(jnp.int32, (rows * ATT_GROUP, 1), 0) // rows
    col = jnp.full((rows * ATT_GROUP, 1), sink_ref[h * ATT_GROUP + ATT_GROUP - 1], f32)
    for g in range(ATT_GROUP - 2, -1, -1):
        col = jnp.where(rg == g, sink_ref[h * ATT_GROUP + g], col)
    return col


def _attn_kernel(sink_ref, q_ref, kp_ref, kc_ref, kn_ref, vp_ref, vc_ref, vn_ref, kx_ref, vx_ref, o_ref):
    i = pl.program_id(1)
    nb = pl.num_programs(1)
    blk = ATT_BLOCK
    r = lax.broadcasted_iota(jnp.int32, (ATT_GROUP * blk, 3 * blk), 0) % blk
    c = lax.broadcasted_iota(jnp.int32, (ATT_GROUP * blk, 3 * blk), 1)
    lo = jnp.where(i > 0, 0, blk)
    hi = jnp.where(i < nb - 1, 3 * blk, 2 * blk)
    valid = (c >= r) & (c <= r + 2 * ATT_WINDOW) & (c >= lo) & (c < hi)
    for h in range(ATT_KV_HEADS):
        hs = slice(h * ATT_HD, (h + 1) * ATT_HD)
        k_win = jnp.concatenate([kp_ref[:, hs], kc_ref[:, hs], kn_ref[:, hs]], axis=0)
        v_win = jnp.concatenate([vp_ref[:, hs], vc_ref[:, hs], vn_ref[:, hs]], axis=0).astype(bf16)
        k_ctx = kx_ref[:, hs].astype(bf16)
        v_ctx = vx_ref[:, hs].astype(bf16)
        q = jnp.concatenate([q_ref[:, (h * ATT_GROUP + g) * ATT_HD:(h * ATT_GROUP + g + 1) * ATT_HD]
                             for g in range(ATT_GROUP)], axis=0)
        s_win = lax.dot_general(q, k_win, _NT, preferred_element_type=f32) * _ATT_SCALE
        s_win = jnp.where(valid, s_win, _NEG)
        s_ctx = lax.dot_general(q, k_ctx, _NT, preferred_element_type=f32) * _ATT_SCALE
        sink = _sink_column(sink_ref, h, blk)
        m = jnp.maximum(jnp.maximum(jnp.max(s_win, axis=-1, keepdims=True),
                                    jnp.max(s_ctx, axis=-1, keepdims=True)), sink)
        p_win = jnp.exp(s_win - m)
        p_ctx = jnp.exp(s_ctx - m)
        den = (jnp.sum(p_win, axis=-1, keepdims=True) + jnp.sum(p_ctx, axis=-1, keepdims=True)
               + jnp.exp(sink - m))
        o = (jnp.dot(p_win.astype(bf16), v_win, preferred_element_type=f32)
             + jnp.dot(p_ctx.astype(bf16), v_ctx, preferred_element_type=f32)) / den
        for g in range(ATT_GROUP):
            cs = slice((h * ATT_GROUP + g) * ATT_HD, (h * ATT_GROUP + g + 1) * ATT_HD)
            o_ref[:, cs] = o[g * blk:(g + 1) * blk].astype(o_ref.dtype)


def _window_attention(qr, kr, z, sink):
    blk = ATT_BLOCK
    nb = SEQ // blk
    nkv = ATT_KV_HEADS * ATT_HD
    cx = NLAT // CTX_LEN

    def krow(off):
        return lambda b, i: (b * nb + jnp.clip(i + off, 0, nb - 1), 0)

    def vrow(off):
        return lambda b, i: (b * nb + jnp.clip(i + off, 0, nb - 1), Z_V // nkv)

    return pl.pallas_call(
        _attn_kernel,
        grid=(BATCH, nb),
        in_specs=[pl.BlockSpec(memory_space=pltpu.SMEM),
                  pl.BlockSpec((blk, ATT_HEADS * ATT_HD), lambda b, i: (b * nb + i, 0)),
                  pl.BlockSpec((blk, nkv), krow(-1)),
                  pl.BlockSpec((blk, nkv), krow(0)),
                  pl.BlockSpec((blk, nkv), krow(1)),
                  pl.BlockSpec((blk, nkv), vrow(-1)),
                  pl.BlockSpec((blk, nkv), vrow(0)),
                  pl.BlockSpec((blk, nkv), vrow(1)),
                  pl.BlockSpec((CTX_LEN, nkv), lambda b, i: (cx + b, Z_K // nkv)),
                  pl.BlockSpec((CTX_LEN, nkv), lambda b, i: (cx + b, Z_V // nkv))],
        out_specs=pl.BlockSpec((blk, ATT_HEADS * ATT_HD), lambda b, i: (b * nb + i, 0)),
        out_shape=jax.ShapeDtypeStruct((NLAT, ATT_HEADS * ATT_HD), bf16),
        compiler_params=_cp("arbitrary", "arbitrary"),
        name="window_attention",
    )(sink, qr, kr, kr, kr, z, z, z, z, z)


def _ctx_attn_kernel(sink_ref, q_ref, k_ref, v_ref, o_ref):
    for h in range(ATT_KV_HEADS):
        hs = slice(h * ATT_HD, (h + 1) * ATT_HD)
        k = k_ref[:, hs].astype(bf16)
        v = v_ref[:, hs].astype(bf16)
        q = jnp.concatenate([q_ref[:, (h * ATT_GROUP + g) * ATT_HD:(h * ATT_GROUP + g + 1) * ATT_HD]
                             for g in range(ATT_GROUP)], axis=0).astype(bf16)
        s = lax.dot_general(q, k, _NT, preferred_element_type=f32) * _ATT_SCALE
        sink = _sink_column(sink_ref, h, CTX_LEN)
        m = jnp.maximum(jnp.max(s, axis=-1, keepdims=True), sink)
        p = jnp.exp(s - m)
        den = jnp.sum(p, axis=-1, keepdims=True) + jnp.exp(sink - m)
        o = jnp.dot(p.astype(bf16), v, preferred_element_type=f32) / den
        for g in range(ATT_GROUP):
            cs = slice((h * ATT_GROUP + g) * ATT_HD, (h * ATT_GROUP + g + 1) * ATT_HD)
            o_ref[:, cs] = o[g * CTX_LEN:(g + 1) * CTX_LEN].astype(o_ref.dtype)


def _context_attention(z, sink):
    nq = ATT_HEADS * ATT_HD
    nkv = ATT_KV_HEADS * ATT_HD
    cx = NLAT // CTX_LEN
    return pl.pallas_call(
        _ctx_attn_kernel,
        grid=(BATCH,),
        in_specs=[pl.BlockSpec(memory_space=pltpu.SMEM),
                  pl.BlockSpec((CTX_LEN, nq), lambda b: (cx + b, Z_Q // nq)),
                  pl.BlockSpec((CTX_LEN, nkv), lambda b: (cx + b, Z_K // nkv)),
                  pl.BlockSpec((CTX_LEN, nkv), lambda b: (cx + b, Z_V // nkv))],
        out_specs=pl.BlockSpec((CTX_LEN, nq), lambda b: (b, 0)),
        out_shape=jax.ShapeDtypeStruct((NCTX, nq), bf16),
        compiler_params=_cp("arbitrary"),
        name="context_attention",
    )(sink, z, z, z)


_ML_SCALE = ML_QK ** -0.5


def _split3(x):
    x1 = x.astype(bf16)
    r1 = x - x1.astype(f32)
    x2 = r1.astype(bf16)
    x3 = (r1 - x2.astype(f32)).astype(bf16)
    return x1, x2, x3


def _mlstm_kernel(q_ref, k_ref, v_ref, g_ref, gb_ref, o_ref, ct_ref, n_ref, m_ref):
    d = pl.program_id(1)
    c = pl.program_id(2)
    ch = ML_CHUNK

    @pl.when(c == 0)
    def _():
        ct_ref[...] = jnp.zeros_like(ct_ref)
        n_ref[...] = jnp.zeros_like(n_ref)
        m_ref[...] = jnp.zeros_like(m_ref)

    g = g_ref[...] + gb_ref[...]
    row = lax.broadcasted_iota(jnp.int32, (ch, ch), 0)
    col = lax.broadcasted_iota(jnp.int32, (ch, ch), 1)
    tri = (row - col) * (1 - 2 * d) >= 0
    tri_b = tri.astype(f32).astype(bf16)
    lf = jax.nn.log_sigmoid(g)
    l1, l2, l3 = _split3(lf)
    bcol = (jnp.dot(tri_b, l1, preferred_element_type=f32)
            + jnp.dot(tri_b, l2, preferred_element_type=f32)
            + jnp.dot(tri_b, l3, preferred_element_type=f32))
    bend = jnp.where(d == 0, bcol[ch - 1:ch, :], bcol[0:1, :])
    g_t = g.T
    b_t = bcol.T

    for h in range(ML_HEADS):
        qf = q_ref[:, h * ML_QK:(h + 1) * ML_QK] * _ML_SCALE
        kf = k_ref[:, h * ML_QK:(h + 1) * ML_QK]
        vf = v_ref[:, h * ML_V:(h + 1) * ML_V]
        q = qf.astype(bf16)
        k = kf.astype(bf16)
        li_c = g[:, h:h + 1]
        b_c = bcol[:, ML_HEADS + h:ML_HEADS + h + 1]
        li_r = g_t[h:h + 1, :]
        b_r = b_t[ML_HEADS + h:ML_HEADS + h + 1, :]
        m_prev = m_ref[h][:, 0:1]
        dmat = jnp.where(tri, b_c - b_r + li_r, -jnp.inf)
        inter = b_c + m_prev
        m_t = jnp.maximum(inter, jnp.max(dmat, axis=-1, keepdims=True))
        w_intra = jnp.exp(dmat - m_t)
        w_inter = jnp.exp(inter - m_t)
        s = lax.dot_general(q, k, _NT, preferred_element_type=f32) * w_intra
        qc = jnp.dot(q, ct_ref[h].astype(bf16), preferred_element_type=f32)
        num = jnp.dot(s.astype(bf16), vf.astype(bf16), preferred_element_type=f32) + w_inter * qc
        den = (jnp.sum(s, axis=-1, keepdims=True)
               + w_inter * jnp.sum(qf * n_ref[h], axis=-1, keepdims=True))
        o_ref[:, h * ML_V:(h + 1) * ML_V] = num / jnp.maximum(jnp.abs(den), jnp.exp(-m_t))

        b_e = bend[:, ML_HEADS + h:ML_HEADS + h + 1]
        g_c = b_e - b_c + li_c
        m_new = jnp.maximum(b_e + m_prev, jnp.max(g_c, axis=0, keepdims=True))
        w_s = jnp.exp(g_c - m_new)
        w_c = jnp.exp(b_e + m_prev - m_new)
        vw = (vf * w_s).astype(bf16)
        ct_ref[h] = w_c * ct_ref[h] + jnp.dot(kf.T.astype(bf16), vw, preferred_element_type=f32)
        n_ref[h] = w_c * n_ref[h] + jnp.sum(kf * w_s, axis=0, keepdims=True)
        m_ref[h] = jnp.broadcast_to(m_new, (1, 128))


def _mlstm(z, gate_b):
    ch = ML_CHUNK
    ncl = SEQ // ch
    nsteps = ncl + CTX_LEN // ch
    assert CTX_LEN == ch

    def rt(b, d, c):
        lat = b * ncl + jnp.where(d == 0, c - 1, ncl - c)
        return jnp.where(c == 0, NLAT // ch + b, lat)

    nq = ML_HEADS * ML_QK
    nv = ML_HEADS * ML_V
    return pl.pallas_call(
        _mlstm_kernel,
        grid=(BATCH, 2, nsteps),
        in_specs=[pl.BlockSpec((ch, nq), lambda b, d, c: (rt(b, d, c), Z_MQ // nq)),
                  pl.BlockSpec((ch, nq), lambda b, d, c: (rt(b, d, c), Z_MK // nq)),
                  pl.BlockSpec((ch, nv), lambda b, d, c: (rt(b, d, c), Z_MV // nv)),
                  pl.BlockSpec((ch, 128), lambda b, d, c: (rt(b, d, c), Z_GATE // 128 + d)),
                  pl.BlockSpec((None, 1, 128), lambda b, d, c: (d, 0, 0))],
        out_specs=pl.BlockSpec((None, ch, nv), lambda b, d, c: (d, rt(b, d, c), 0)),
        out_shape=jax.ShapeDtypeStruct((2, R, nv), f32),
        scratch_shapes=[pltpu.VMEM((ML_HEADS, ML_QK, ML_V), f32),
                        pltpu.VMEM((ML_HEADS, 1, ML_QK), f32),
                        pltpu.VMEM((ML_HEADS, 1, 128), f32)],
        compiler_params=_cp("arbitrary", "arbitrary", "arbitrary"),
        name="mlstm",
    )(z, z, z, z, gate_b)


def _mlstm_out_kernel(h_ref, zo_ref, g_ref, o_ref):
    for h in range(ML_HEADS):
        s = slice(h * ML_V, (h + 1) * ML_V)
        x = h_ref[0, :, s] + h_ref[1, :, s]
        xn = x * lax.rsqrt(jnp.mean(x * x, axis=-1, keepdims=True) + EPS) * g_ref[:, s]
        o_ref[:, s] = (xn * jax.nn.sigmoid(zo_ref[:, s])).astype(o_ref.dtype)


def _mlstm_out(hh, z, g, nrows):
    tm = 512
    nv = ML_HEADS * ML_V
    return pl.pallas_call(
        _mlstm_out_kernel,
        grid=(nrows // tm,),
        in_specs=[pl.BlockSpec((2, tm, nv), lambda i: (0, i, 0)),
                  pl.BlockSpec((tm, nv), lambda i: (i, Z_MO // nv)),
                  pl.BlockSpec((1, nv), lambda i: (0, 0))],
        out_specs=pl.BlockSpec((tm, nv), lambda i: (i, 0)),
        out_shape=jax.ShapeDtypeStruct((nrows, nv), bf16),
        compiler_params=_cp("arbitrary"),
        name="mlstm_out",
    )(hh, z, g.reshape(1, nv))


def _router_kernel(f_ref, w_ref, idx_ref, p_ref):
    logits = jnp.dot(f_ref[...].astype(bf16), w_ref[...], preferred_element_type=f32)
    lane = lax.broadcasted_iota(jnp.int32, logits.shape, 1).astype(f32)
    logits = jnp.where(lane < N_EXPERTS, logits, -jnp.inf)
    v1 = jnp.max(logits, axis=-1, keepdims=True)
    i1 = jnp.min(jnp.where(logits == v1, lane, 128.0), axis=-1, keepdims=True)
    rest = jnp.where(lane == i1, -jnp.inf, logits)
    v2 = jnp.max(rest, axis=-1, keepdims=True)
    i2 = jnp.min(jnp.where(rest == v2, lane, 128.0), axis=-1, keepdims=True)
    e = jnp.exp(v2 - v1)
    p1 = 1.0 / (1.0 + e)
    p2 = e / (1.0 + e)
    idx_ref[...] = jnp.where(lane == 0, i1, jnp.where(lane == 1, i2, 0.0)).astype(jnp.int32)
    p_ref[...] = jnp.where(lane == 0, p1, jnp.where(lane == 1, p2, 0.0))


def _router(f, w_router):
    tm = 512
    w = jnp.pad(w_router, ((0, 0), (0, 128 - N_EXPERTS))).astype(bf16)
    return pl.pallas_call(
        _router_kernel,
        grid=(NLAT // tm,),
        in_specs=[pl.BlockSpec((tm, D_MODEL), lambda i: (i, 0)),
                  pl.BlockSpec((D_MODEL, 128), lambda i: (0, 0))],
        out_specs=[pl.BlockSpec((tm, 128), lambda i: (i, 0)),
                   pl.BlockSpec((tm, 128), lambda i: (i, 0))],
        out_shape=[jax.ShapeDtypeStruct((NLAT, 128), jnp.int32),
                   jax.ShapeDtypeStruct((NLAT, 128), f32)],
        compiler_params=_cp("arbitrary"),
        name="router",
    )(f, w)


def _gather_kernel(tok_ref, src_ref, o_ref, buf, sem):
    base = pl.program_id(0) * MOE_TILE

    def copy(r):
        return pltpu.make_async_copy(src_ref.at[pl.ds(tok_ref[base + r], 1)], buf.at[pl.ds(r, 1)], sem)

    def start(r, carry):
        copy(r).start()
        return carry

    def wait(r, carry):
        copy(r).wait()
        return carry

    lax.fori_loop(0, MOE_TILE, start, 0, unroll=8)
    lax.fori_loop(0, MOE_TILE, wait, 0, unroll=8)
    o_ref[...] = buf[...].astype(o_ref.dtype)


def _gather_rows(src, rows, n_out, out_dtype):
    width = src.shape[1]
    return pl.pallas_call(
        _gather_kernel,
        grid_spec=pltpu.PrefetchScalarGridSpec(
            num_scalar_prefetch=1,
            grid=(n_out // MOE_TILE,),
            in_specs=[pl.BlockSpec(memory_space=pl.ANY)],
            out_specs=pl.BlockSpec((MOE_TILE, width), lambda i, tok: (i, 0)),
            scratch_shapes=[pltpu.VMEM((MOE_TILE, width), src.dtype), pltpu.SemaphoreType.DMA(())]),
        out_shape=jax.ShapeDtypeStruct((n_out, width), out_dtype),
        compiler_params=_cp("arbitrary"),
        name="gather_rows",
    )(rows, src)


def _moe_up_kernel(be_ref, nu_ref, a_ref, wg_ref, wu_ref, o_ref, wg_bf, wu_bf):
    i = pl.program_id(1)
    used = i < nu_ref[0]
    fresh = jnp.logical_or(i == 0, be_ref[i] != be_ref[jnp.maximum(i - 1, 0)])

    @pl.when(jnp.logical_and(used, fresh))
    def _():
        wg_bf[...] = wg_ref[...].astype(bf16)
        wu_bf[...] = wu_ref[...].astype(bf16)

    @pl.when(used)
    def _():
        a = a_ref[...]
        g = jnp.dot(a, wg_bf[...], preferred_element_type=f32)
        u = jnp.dot(a, wu_bf[...], preferred_element_type=f32)
        o_ref[...] = (g * jax.nn.sigmoid(g) * u).astype(o_ref.dtype)

    @pl.when(i >= nu_ref[0])
    def _():
        o_ref[...] = jnp.zeros_like(o_ref)


def _moe_up(xs, wg, wu, blk_expert, n_used):
    tn = 1024
    nblk = xs.shape[0] // MOE_TILE

    def row(j, i, be, nu):
        return (jnp.minimum(i, nu[0] - 1), 0)

    def wmap(j, i, be, nu):
        return (be[jnp.minimum(i, nu[0] - 1)], 0, j)

    return pl.pallas_call(
        _moe_up_kernel,
        grid_spec=pltpu.PrefetchScalarGridSpec(
            num_scalar_prefetch=2,
            grid=(FFN_EXPERT // tn, nblk),
            in_specs=[pl.BlockSpec((MOE_TILE, D_MODEL), row),
                      pl.BlockSpec((None, D_MODEL, tn), wmap),
                      pl.BlockSpec((None, D_MODEL, tn), wmap)],
            out_specs=pl.BlockSpec((MOE_TILE, tn), lambda j, i, be, nu: (i, j)),
            scratch_shapes=[pltpu.VMEM((D_MODEL, tn), bf16), pltpu.VMEM((D_MODEL, tn), bf16)]),
        out_shape=jax.ShapeDtypeStruct((xs.shape[0], FFN_EXPERT), bf16),
        compiler_params=_cp("arbitrary", "arbitrary"),
        name="moe_up",
    )(blk_expert, n_used, xs, wg, wu)


def _moe_down_kernel(be_ref, nu_ref, a_ref, w_ref, o_ref):
    i = pl.program_id(1)

    @pl.when(i < nu_ref[0])
    def _():
        o_ref[...] = jnp.dot(a_ref[...], w_ref[...], preferred_element_type=f32)

    @pl.when(i >= nu_ref[0])
    def _():
        o_ref[...] = jnp.zeros_like(o_ref)


def _moe_down(hs, wd, blk_expert, n_used):
    tn = 1024
    nblk = hs.shape[0] // MOE_TILE

    def row(j, i, be, nu):
        return (jnp.minimum(i, nu[0] - 1), 0)

    def wmap(j, i, be, nu):
        return (be[jnp.minimum(i, nu[0] - 1)], 0, j)

    return pl.pallas_call(
        _moe_down_kernel,
        grid_spec=pltpu.PrefetchScalarGridSpec(
            num_scalar_prefetch=2,
            grid=(D_MODEL // tn, nblk),
            in_specs=[pl.BlockSpec((MOE_TILE, FFN_EXPERT), row),
                      pl.BlockSpec((None, FFN_EXPERT, tn), wmap)],
            out_specs=pl.BlockSpec((MOE_TILE, tn), lambda j, i, be, nu: (i, j))),
        out_shape=jax.ShapeDtypeStruct((hs.shape[0], D_MODEL), f32),
        compiler_params=_cp("arbitrary", "arbitrary"),
        name="moe_down",
    )(blk_expert, n_used, hs, wd)


_COMBINE_TM = 256


def _combine_kernel(slot_ref, yb_ref, x_ref, p_ref, g2_ref, fg_ref, o_ref, buf, sem):
    i = pl.program_id(0)
    cur = i % 2

    def copy(step, b, r, k):
        s = slot_ref[(step * _COMBINE_TM + r) * TOP_K + k]
        return pltpu.make_async_copy(yb_ref.at[pl.ds(s, 1)], buf.at[b, k, pl.ds(r, 1)], sem.at[b])

    def fetch(step, b):
        def start(r, carry):
            copy(step, b, r, 0).start()
            copy(step, b, r, 1).start()
            return carry
        lax.fori_loop(0, _COMBINE_TM, start, 0, unroll=4)

    @pl.when(i == 0)
    def _():
        fetch(0, 0)

    @pl.when(i + 1 < pl.num_programs(0))
    def _():
        fetch(i + 1, 1 - cur)

    def wait(r, carry):
        copy(i, cur, r, 0).wait()
        copy(i, cur, r, 1).wait()
        return carry

    lax.fori_loop(0, _COMBINE_TM, wait, 0, unroll=8)
    p = p_ref[...]
    y = buf[cur, 0] * p[:, 0:1] + buf[cur, 1] * p[:, 1:2]
    x = x_ref[...] + g2_ref[...] * y
    o_ref[...] = x * lax.rsqrt(jnp.mean(x * x, axis=-1, keepdims=True) + EPS) * fg_ref[...]


def _combine_final(slot, yb, xa, probs, modt, final_g):
    tm = _COMBINE_TM
    return pl.pallas_call(
        _combine_kernel,
        grid_spec=pltpu.PrefetchScalarGridSpec(
            num_scalar_prefetch=1,
            grid=(NLAT // tm,),
            in_specs=[pl.BlockSpec(memory_space=pl.ANY),
                      pl.BlockSpec((tm, D_MODEL), lambda i, s: (i, 0)),
                      pl.BlockSpec((tm, 128), lambda i, s: (i, 0)),
                      pl.BlockSpec((None, 1, D_MODEL), lambda i, s: (i // (SEQ // tm), 0, 5)),
                      pl.BlockSpec((1, D_MODEL), lambda i, s: (0, 0))],
            out_specs=pl.BlockSpec((tm, D_MODEL), lambda i, s: (i, 0)),
            scratch_shapes=[pltpu.VMEM((2, TOP_K, tm, D_MODEL), f32),
                            pltpu.SemaphoreType.DMA((2,))]),
        out_shape=jax.ShapeDtypeStruct((NLAT, D_MODEL), f32),
        compiler_params=_cp("arbitrary"),
        name="moe_combine_final_norm",
    )(slot, yb, xa, probs, modt, final_g.reshape(1, D_MODEL))


def _moe_routing(top_i):
    a = NLAT * TOP_K
    e_flat = top_i.reshape(a)
    onehot = (e_flat[:, None] == jnp.arange(N_EXPERTS)[None, :]).astype(jnp.int32)
    csum = jnp.cumsum(onehot, axis=0)
    rank = jnp.sum(onehot * csum, axis=1) - 1
    counts = csum[-1]
    padded = (counts + MOE_TILE - 1) // MOE_TILE * MOE_TILE
    pad_end = jnp.cumsum(padded)
    pad_start = pad_end - padded
    slot = (pad_start[e_flat] + rank).astype(jnp.int32)
    n_rows = a + N_EXPERTS * MOE_TILE
    nblk = n_rows // MOE_TILE
    slot_tok = jnp.zeros((n_rows,), jnp.int32).at[slot].set(jnp.arange(a, dtype=jnp.int32) // TOP_K)
    blk_expert = jnp.minimum(jnp.searchsorted(pad_end, jnp.arange(nblk) * MOE_TILE, side='right'),
                             N_EXPERTS - 1).astype(jnp.int32)
    n_used = (pad_end[-1:] // MOE_TILE).astype(jnp.int32)
    return slot, slot_tok, blk_expert, n_used, n_rows


def _in_proj_weight(w, gate_b):
    o = np.cumsum((0,) + (3072, 1024, 256, 256, 512, 512, 1024, 1024, 16, 6144))
    hy, q, k, v, mq, mk, mv, mo, gt, mg = [w[:, o[i]:o[i + 1]] for i in range(10)]
    pad = jnp.zeros((D_MODEL, 128 - 2 * ML_HEADS), w.dtype)
    gates = [jnp.concatenate([gt[:, 8 * d:8 * d + 8], pad], axis=1) for d in range(2)]
    wz = jnp.concatenate([hy, q, mv, mo, mg, k, v, mq, mk] + gates
                         + [jnp.zeros((D_MODEL, Z_COLS - Z_GATE - 256), w.dtype)], axis=1).astype(bf16)
    gb = jnp.pad(gate_b.reshape(2, 1, 2 * ML_HEADS), ((0, 0), (0, 0), (0, 128 - 2 * ML_HEADS)))
    return wz, gb


def kernel(x, c, ctx, c_ctx, w_mod, b_mod, norm_mix_g, norm_ffn_g, w_in, hy_short_w, hy_short_b, hy_w1, hy_b1, hy_w2, hy_b2, hy_w3, hy_freq, hy_bias, att_sink, ml_gate_b, ml_norm_g, w_branch, w_out, ffn_wg, ffn_wu, ffn_wd, moe_router, moe_wg, moe_wu, moe_wd, final_g):
    xa = jnp.concatenate([x.reshape(NLAT, D_MODEL), ctx.reshape(NCTX, D_MODEL)], axis=0)
    c_all = jnp.concatenate([c, c_ctx[None], jnp.zeros((8 - BATCH - 1, D_MODEL), f32)], axis=0)
    mod = _modulation(c_all, w_mod, b_mod)
    rope_tabs = _rope_tables()
    dft_tabs = _dft_tables(SEQ)
    ctx_tabs = _short_dft_tables(CTX_LEN)
    out = None
    for layer in range(DEPTH):
        last = layer == DEPTH - 1
        modt = mod[layer].reshape(8, 1, 6 * D_MODEL)
        n_mix = NLAT if last else R

        u = _normmod(xa, norm_mix_g[layer], modt, 0, R)
        wz, gate_b = _in_proj_weight(w_in[layer], ml_gate_b[layer])
        z = _in_proj(u, wz)

        hy = (hy_short_w[layer], hy_short_b[layer], hy_w1[layer], hy_b1[layer], hy_w2[layer],
              hy_b2[layer], hy_w3[layer], hy_freq[layer], hy_bias[layer])
        a_rows = _hyena_branch(z, 0, SEQ, dft_tabs, *hy)

        qr, kr = _rope(z, rope_tabs)
        b_rows = _window_attention(qr, kr, z, att_sink[layer])

        hh = _mlstm(z, gate_b)
        c_rows = _mlstm_out(hh, z, ml_norm_g[layer], n_mix)

        if not last:
            a_rows = jnp.concatenate([a_rows, _hyena_branch(z, NLAT, CTX_LEN, ctx_tabs, *hy)], axis=0)
            b_rows = jnp.concatenate([b_rows, _context_attention(z, att_sink[layer])], axis=0)

        ymid = _merge(a_rows, b_rows, c_rows, w_branch[layer].astype(bf16), z, n_mix)
        xa = _mm_resid(ymid, w_out[layer].astype(bf16), xa, modt, 2, n_mix)

        f = _normmod(xa, norm_ffn_g[layer], modt, 3, n_mix, bf16 if layer % 2 == 0 else f32)
        if layer % 2 == 0:
            e = layer // 2
            hmid = _ffn_up(f, ffn_wg[e].astype(bf16), ffn_wu[e].astype(bf16))
            xa = _mm_resid(hmid, ffn_wd[e].astype(bf16), xa, modt, 5, R)
        else:
            e = layer // 2
            top_i, probs = _router(f, moe_router[e])
            slot, slot_tok, blk_expert, n_used, n_rows = _moe_routing(top_i[:, :TOP_K])
            xs = _gather_rows(f, slot_tok, n_rows, bf16)
            hs = _moe_up(xs, moe_wg[e], moe_wu[e], blk_expert, n_used)
            yb = _moe_down(hs, moe_wd[e].astype(bf16), blk_expert, n_used)
            assert last
            out = _combine_final(slot, yb, xa, probs, modt, final_g)
    return out.reshape(BATCH, SEQ, D_MODEL)
```

```python
import functools
import math

import numpy as np
import jax
import jax.numpy as jnp
from jax import lax
from jax.experimental import pallas as pl
from jax.experimental.pallas import tpu as pltpu

f32 = jnp.float32
bf16 = jnp.bfloat16

D_MODEL = 2048
BATCH = 4
SEQ = 4096
DEPTH = 2
GRID_W = 64
CTX_LEN = 256
EPS = 1e-6

HY_WIDTH = D_MODEL // 2
HY_ORDER = 2
HY_BANDS = 16
HY_EMB = 2 * HY_BANDS + 1
HY_FFN = 64
HY_MIN_DECAY = -3.0701134573253943
HY_MAX_DECAY = -15.35056728662697
HY_N2 = 128
HY_SLABS = 8
HY_UNROLL = 8

ATT_HD = 128
ATT_HEADS = 8
ATT_KV_HEADS = 2
ATT_GROUP = 4
ATT_WINDOW = 128
ATT_BLOCK = 128
ROPE_BASE = 10000.0

ML_HEADS = 4
ML_V = 256
ML_QK = 128
ML_CHUNK = 256

N_BRANCH = 3
BRANCH_W = D_MODEL // 2
FFN_DENSE = 5632
N_EXPERTS = 8
TOP_K = 2
FFN_EXPERT = 7168
MOE_TILE = 512

NCTX = BATCH * CTX_LEN
NLAT = BATCH * SEQ
R = NCTX + NLAT

Z_HY = 0
Z_Q = 3072
Z_MV = 4096
Z_MO = 5120
Z_MERGE = 6144
Z_K = 12288
Z_V = 12544
Z_MQ = 12800
Z_MK = 13312
Z_GATE = 13824
Z_COLS = 14080

VMEM_LIMIT = 56 * 1024 * 1024


def _cp(*sem, vmem=VMEM_LIMIT):
    return pltpu.CompilerParams(dimension_semantics=sem, vmem_limit_bytes=vmem)


def _modrow(i, tm):
    return jnp.where(i >= NLAT // tm, BATCH, i // (SEQ // tm))


def _mod_kernel(c_ref, w_ref, b_ref, o_ref):
    c = c_ref[...]
    a = (c * jax.nn.sigmoid(c)).astype(bf16)
    o_ref[...] = jnp.dot(a, w_ref[...].astype(bf16), preferred_element_type=f32) + b_ref[...]


def _modulation(c_all, w_mod, b_mod):
    tn = 1024
    n = 6 * D_MODEL
    return pl.pallas_call(
        _mod_kernel,
        grid=(DEPTH, n // tn),
        in_specs=[pl.BlockSpec((8, D_MODEL), lambda l, j: (0, 0)),
                  pl.BlockSpec((None, D_MODEL, tn), lambda l, j: (l, 0, j)),
                  pl.BlockSpec((None, 1, tn), lambda l, j: (l, 0, j))],
        out_specs=pl.BlockSpec((None, 8, tn), lambda l, j: (l, 0, j)),
        out_shape=jax.ShapeDtypeStruct((DEPTH, 8, n), f32),
        compiler_params=_cp("arbitrary", "arbitrary"),
        name="modulation",
    )(c_all, w_mod, b_mod.reshape(DEPTH, 1, n))


def _normmod_kernel(x_ref, g_ref, sh_ref, sc_ref, o_ref):
    x = x_ref[...]
    y = x * lax.rsqrt(jnp.mean(x * x, axis=-1, keepdims=True) + EPS) * g_ref[...]
    o_ref[...] = (y * (1.0 + sc_ref[...]) + sh_ref[...]).astype(o_ref.dtype)


def _normmod(xa, g, modt, which, nrows, out_dtype=bf16):
    tm = 1024
    return pl.pallas_call(
        _normmod_kernel,
        grid=(nrows // tm,),
        in_specs=[pl.BlockSpec((tm, D_MODEL), lambda i: (i, 0)),
                  pl.BlockSpec((1, D_MODEL), lambda i: (0, 0)),
                  pl.BlockSpec((None, 1, D_MODEL), lambda i: (_modrow(i, tm), 0, which)),
                  pl.BlockSpec((None, 1, D_MODEL), lambda i: (_modrow(i, tm), 0, which + 1))],
        out_specs=pl.BlockSpec((tm, D_MODEL), lambda i: (i, 0)),
        out_shape=jax.ShapeDtypeStruct((nrows, D_MODEL), out_dtype),
        compiler_params=_cp("arbitrary"),
        name="normmod",
    )(xa, g.reshape(1, D_MODEL), modt, modt)


def _mm_kernel(a_ref, w_ref, o_ref):
    o_ref[...] = jnp.dot(a_ref[...], w_ref[...], preferred_element_type=f32).astype(o_ref.dtype)


def _in_proj(u, w):
    tm, tn = 1024, 1280
    return pl.pallas_call(
        _mm_kernel,
        grid=(Z_COLS // tn, R // tm),
        in_specs=[pl.BlockSpec((tm, D_MODEL), lambda j, i: (i, 0)),
                  pl.BlockSpec((D_MODEL, tn), lambda j, i: (0, j))],
        out_specs=pl.BlockSpec((tm, tn), lambda j, i: (i, j)),
        out_shape=jax.ShapeDtypeStruct((R, Z_COLS), f32),
        compiler_params=_cp("arbitrary", "arbitrary"),
        name="in_proj",
    )(u, w)


def _mm_resid_kernel(a_ref, w_ref, x_ref, g_ref, o_ref):
    y = jnp.dot(a_ref[...], w_ref[...], preferred_element_type=f32)
    o_ref[...] = x_ref[...] + g_ref[...] * y


def _mm_resid(a, w, xa, modt, which, nrows):
    k = a.shape[1]
    tm, tn = (1024, 1024) if k <= D_MODEL else (512, 1024)
    nj = D_MODEL // tn
    return pl.pallas_call(
        _mm_resid_kernel,
        grid=(nj, nrows // tm),
        in_specs=[pl.BlockSpec((tm, k), lambda j, i: (i, 0)),
                  pl.BlockSpec((k, tn), lambda j, i: (0, j)),
                  pl.BlockSpec((tm, tn), lambda j, i: (i, j)),
                  pl.BlockSpec((None, 1, tn), lambda j, i: (_modrow(i, tm), 0, which * nj + j))],
        out_specs=pl.BlockSpec((tm, tn), lambda j, i: (i, j)),
        out_shape=jax.ShapeDtypeStruct((nrows, D_MODEL), f32),
        compiler_params=_cp("arbitrary", "arbitrary"),
        name="mm_resid",
    )(a, w, xa, modt)


def _swiglu_kernel(a_ref, wg_ref, wu_ref, o_ref):
    a = a_ref[...]
    g = jnp.dot(a, wg_ref[...], preferred_element_type=f32)
    u = jnp.dot(a, wu_ref[...], preferred_element_type=f32)
    o_ref[...] = (g * jax.nn.sigmoid(g) * u).astype(o_ref.dtype)


def _ffn_up(f, wg, wu):
    tm, tn = 1024, 512
    n = wg.shape[1]
    return pl.pallas_call(
        _swiglu_kernel,
        grid=(n // tn, R // tm),
        in_specs=[pl.BlockSpec((tm, D_MODEL), lambda j, i: (i, 0)),
                  pl.BlockSpec((D_MODEL, tn), lambda j, i: (0, j)),
                  pl.BlockSpec((D_MODEL, tn), lambda j, i: (0, j))],
        out_specs=pl.BlockSpec((tm, tn), lambda j, i: (i, j)),
        out_shape=jax.ShapeDtypeStruct((R, n), bf16),
        compiler_params=_cp("arbitrary", "arbitrary"),
        name="ffn_up",
    )(f, wg, wu)


def _merge_kernel(a_ref, b_ref, c_ref, w_ref, g0_ref, g1_ref, g2_ref, o_ref):
    y = jax.nn.sigmoid(g0_ref[...]) * jnp.dot(a_ref[...], w_ref[0], preferred_element_type=f32)
    y += jax.nn.sigmoid(g1_ref[...]) * jnp.dot(b_ref[...], w_ref[1], preferred_element_type=f32)
    y += jax.nn.sigmoid(g2_ref[...]) * jnp.dot(c_ref[...], w_ref[2], preferred_element_type=f32)
    o_ref[...] = y.astype(o_ref.dtype)


def _merge(a, b, c, wb, z, nrows):
    tm, tn = 512, 1024
    nj = D_MODEL // tn
    act = pl.BlockSpec((tm, BRANCH_W), lambda j, i: (i, 0))

    def gate(br):
        return pl.BlockSpec((tm, tn), lambda j, i: (i, (Z_MERGE + br * D_MODEL) // tn + j))

    return pl.pallas_call(
        _merge_kernel,
        grid=(nj, nrows // tm),
        in_specs=[act, act, act,
                  pl.BlockSpec((N_BRANCH, BRANCH_W, tn), lambda j, i: (0, 0, j)),
                  gate(0), gate(1), gate(2)],
        out_specs=pl.BlockSpec((tm, tn), lambda j, i: (i, j)),
        out_shape=jax.ShapeDtypeStruct((nrows, D_MODEL), bf16),
        compiler_params=_cp("arbitrary", "arbitrary"),
        name="merge",
    )(a, b, c, wb, z, z, z)


def _short_conv_kernel(x_ref, xp_ref, xn_ref, w_ref, b_ref, o_ref, *, tiles_per_seq):
    i = pl.program_id(0)
    tm = x_ref.shape[0]
    x = x_ref[...]
    first = i % tiles_per_seq == 0
    last = i % tiles_per_seq == tiles_per_seq - 1
    prev = jnp.where(first, 0.0, xp_ref[7:8, :])
    nxt = jnp.where(last, 0.0, xn_ref[0:1, :])
    row = lax.broadcasted_iota(jnp.int32, x.shape, 0)
    up = jnp.where(row == 0, prev, pltpu.roll(x, 1, 0))
    dn = jnp.where(row == tm - 1, nxt, pltpu.roll(x, tm - 1, 0))
    o_ref[...] = w_ref[0:1, :] * up + w_ref[1:2, :] * x + w_ref[2:3, :] * dn + b_ref[...]


def _short_conv(z, w, b, row0, nrows, seq_len):
    tm, tn = 256, 3 * HY_WIDTH
    t0 = row0 // tm
    nc = 3 * HY_WIDTH
    last8 = R // 8 - 1
    return pl.pallas_call(
        functools.partial(_short_conv_kernel, tiles_per_seq=seq_len // tm),
        grid=(nrows // tm, nc // tn),
        in_specs=[pl.BlockSpec((tm, tn), lambda i, j: (i + t0, j)),
                  pl.BlockSpec((8, tn), lambda i, j: (jnp.maximum((i + t0) * (tm // 8) - 1, 0), j)),
                  pl.BlockSpec((8, tn), lambda i, j: (jnp.minimum((i + t0 + 1) * (tm // 8), last8), j)),
                  pl.BlockSpec((3, tn), lambda i, j: (0, j)),
                  pl.BlockSpec((1, tn), lambda i, j: (0, j))],
        out_specs=pl.BlockSpec((tm, tn), lambda i, j: (i, j)),
        out_shape=jax.ShapeDtypeStruct((nrows, nc), f32),
        compiler_params=_cp("arbitrary", "arbitrary"),
        name="hyena_short_conv",
    )(z, z, z, w, b.reshape(1, nc))


def _hyfilt_kernel(w1_ref, b1_ref, w2_ref, b2_ref, w3_ref, fr_ref, dl_ref, o_ref, *, seq_len):
    tm = o_ref.shape[0]
    r = pl.program_id(0) * tm + lax.broadcasted_iota(jnp.int32, (tm, 1), 0)
    p = jnp.where(r < seq_len, r, 2 * seq_len - r)
    t = p.astype(f32) / seq_len
    lane = lax.broadcasted_iota(jnp.int32, (tm, 128), 1)
    band = jnp.where(lane <= HY_BANDS, lane, lane - HY_BANDS).astype(f32)
    ang = ((2.0 * math.pi) * t) * band
    feats = jnp.where(lane == 0, t,
                      jnp.where(lane <= HY_BANDS, jnp.sin(ang),
                                jnp.where(lane <= 2 * HY_BANDS, jnp.cos(ang), 0.0)))
    h = jnp.dot(feats.astype(bf16), w1_ref[...].astype(bf16), preferred_element_type=f32) + b1_ref[...]
    h = jnp.sin(fr_ref[0:1, :] * h)
    h = jnp.dot(h.astype(bf16), w2_ref[...].astype(bf16), preferred_element_type=f32) + b2_ref[...]
    h = jnp.sin(fr_ref[1:2, :] * h)
    k = jnp.dot(h.astype(bf16), w3_ref[...].astype(bf16), preferred_element_type=f32)
    k = k * jnp.exp(-t * dl_ref[...])
    o_ref[...] = jnp.where(r == seq_len, 0.0, k)


def _hyena_filter(seq_len, w1, b1, w2, b2, w3, freq):
    nc = HY_ORDER * HY_WIDTH
    tm, tn = min(512, seq_len), nc
    deltas = jnp.abs(jnp.linspace(HY_MIN_DECAY, HY_MAX_DECAY, HY_WIDTH, dtype=f32))
    dl = jnp.tile(deltas, HY_ORDER).reshape(1, nc)
    w1p = jnp.pad(w1, ((0, 128 - HY_EMB), (0, 0)))
    per_dir = nc // tn
    full = lambda i, j: (0, 0)
    return pl.pallas_call(
        functools.partial(_hyfilt_kernel, seq_len=seq_len),
        grid=(2 * seq_len // tm, per_dir),
        in_specs=[pl.BlockSpec((128, HY_FFN), full),
                  pl.BlockSpec((1, HY_FFN), full),
                  pl.BlockSpec((HY_FFN, HY_FFN), full),
                  pl.BlockSpec((1, HY_FFN), full),
                  pl.BlockSpec((HY_FFN, tn), lambda i, j: (0, jnp.where(i >= seq_len // tm, per_dir, 0) + j)),
                  pl.BlockSpec((2, HY_FFN), full),
                  pl.BlockSpec((1, tn), lambda i, j: (0, j))],
        out_specs=pl.BlockSpec((tm, tn), lambda i, j: (i, j)),
        out_shape=jax.ShapeDtypeStruct((2 * seq_len, nc), f32),
        compiler_params=_cp("arbitrary", "arbitrary"),
        name="hyena_filter",
    )(w1p, b1.reshape(1, HY_FFN), w2, b2.reshape(1, HY_FFN), w3, freq, dl)


def _pass3(m):
    hi = m.astype(bf16)
    lo = (m - hi.astype(f32)).astype(bf16)
    return jnp.concatenate([hi, hi, lo], axis=-1)


def _rhs3(x):
    hi = x.astype(bf16)
    lo = (x - hi.astype(f32)).astype(bf16)
    return jnp.concatenate([hi, lo, hi], axis=0)


def _stack_complex(ar, ai):
    return jnp.concatenate([jnp.concatenate([ar, -ai], axis=-1),
                            jnp.concatenate([ai, ar], axis=-1)], axis=-2)


def _dft_tables(seq_len):
    n = 2 * seq_len
    n1 = n // HY_N2
    half = n1 // 2
    i1 = jnp.arange(n1, dtype=jnp.int32)
    ang1 = (2.0 * math.pi / n1) * ((i1[:, None] * i1[None, :]) % n1).astype(f32)
    c1, s1 = jnp.cos(ang1), jnp.sin(ang1)
    m2 = _pass3(_stack_complex(c1[:, :half], -s1[:, :half]))
    m2f = _pass3(jnp.concatenate([c1, -s1], axis=0))
    m8 = _pass3(_stack_complex(c1[:half, :] / n, s1[:half, :] / n))
    i2 = jnp.arange(HY_N2, dtype=jnp.int32)
    k = i1[:, None, None] + n1 * i2[None, :, None]
    ang = (2.0 * math.pi / n) * ((i2[None, None, :] * k) % n).astype(f32)
    c, s = jnp.cos(ang), jnp.sin(ang)
    g4 = _pass3(_stack_complex(c, -s))
    ct, st = jnp.swapaxes(c, 1, 2), jnp.swapaxes(s, 1, 2)
    g6 = _pass3(_stack_complex(ct, st))
    return dict(m2=m2, m2f=m2f, m8=m8, g4=g4, g6=g6)


def _store_halves(ref, rows, val):
    ref[0, rows, :] = val[:, :128]
    ref[1, rows, :] = val[:, 128:]


def _load_halves(ref, rows):
    return jnp.concatenate([ref[0, rows, :], ref[1, rows, :]], axis=1)


def _hyfft_kernel(xa_ref, xb_ref, m2_ref, g4_ref, kr_ref, ki_ref, br, bi, *, n1):
    s = pl.program_id(1)

    @pl.when(s == 0)
    def _():
        def body(n2, carry):
            rows = pl.ds(n2, n1, stride=HY_N2)
            rhs = _rhs3(jnp.concatenate([xa_ref[rows, :], xb_ref[rows, :]], axis=1))
            out = jnp.dot(m2_ref[...], rhs, preferred_element_type=f32)
            _store_halves(br, rows, out[:n1])
            _store_halves(bi, rows, out[n1:])
            return carry
        lax.fori_loop(0, HY_N2, body, 0, unroll=HY_UNROLL)

    for j in range(HY_SLABS):
        rows = pl.ds(pl.multiple_of((s * HY_SLABS + j) * HY_N2, HY_N2), HY_N2)
        y = jnp.concatenate([_load_halves(br, rows), _load_halves(bi, rows)], axis=0)
        z = jnp.dot(g4_ref[j], _rhs3(y), preferred_element_type=f32)
        kr_ref[j * HY_N2:(j + 1) * HY_N2, :] = z[:HY_N2]
        ki_ref[j * HY_N2:(j + 1) * HY_N2, :] = z[HY_N2:]


def _hyena_filter_fft(kern, tabs):
    n, nc = kern.shape
    n1 = n // HY_N2
    tn = 256
    sl = HY_SLABS * HY_N2
    spec_out = pl.BlockSpec((sl, tn), lambda t, s: (s, t))
    return pl.pallas_call(
        functools.partial(_hyfft_kernel, n1=n1),
        grid=(nc // tn, n1 // HY_SLABS),
        in_specs=[pl.BlockSpec((n, 128), lambda t, s: (0, 2 * t)),
                  pl.BlockSpec((n, 128), lambda t, s: (0, 2 * t + 1)),
                  pl.BlockSpec(tabs["m2f"].shape, lambda t, s: (0, 0)),
                  pl.BlockSpec((HY_SLABS,) + tabs["g4"].shape[1:], lambda t, s: (s, 0, 0))],
        out_specs=[spec_out, spec_out],
        out_shape=[jax.ShapeDtypeStruct((n, nc), f32)] * 2,
        scratch_shapes=[pltpu.VMEM((2, n, 128), f32), pltpu.VMEM((2, n, 128), f32)],
        compiler_params=_cp("arbitrary", "arbitrary"),
        name="hyena_filter_fft",
    )(kern, kern, tabs["m2f"], tabs["g4"])


def _hyconv_kernel(x_ref, m2_ref, g4_ref, g6_ref, m8_ref, kr_ref, ki_ref, o_ref, br, bi, *, n1):
    s = pl.program_id(1)
    ns = pl.num_programs(1)
    half = n1 // 2
    cw = x_ref.shape[2]

    @pl.when(s == 0)
    def _():
        def body(n2, carry):
            rows = pl.ds(n2, half, stride=HY_N2)
            xr = jnp.concatenate([x_ref[0, rows, :], x_ref[2, rows, :]], axis=1)
            xi = jnp.concatenate([x_ref[1, rows, :], x_ref[3, rows, :]], axis=1)
            rhs = _rhs3(jnp.concatenate([xr, xi], axis=0))
            out = jnp.dot(m2_ref[...], rhs, preferred_element_type=f32)
            brows = pl.ds(n2, n1, stride=HY_N2)
            _store_halves(br, brows, out[:n1])
            _store_halves(bi, brows, out[n1:])
            return carry
        lax.fori_loop(0, HY_N2, body, 0, unroll=HY_UNROLL)

    for j in range(HY_SLABS):
        srows = pl.ds(pl.multiple_of((s * HY_SLABS + j) * HY_N2, HY_N2), HY_N2)
        y = jnp.concatenate([_load_halves(br, srows), _load_halves(bi, srows)], axis=0)
        z = jnp.dot(g4_ref[j], _rhs3(y), preferred_element_type=f32)
        zr, zi = z[:HY_N2], z[HY_N2:]
        kr = jnp.concatenate([kr_ref[j * HY_N2:(j + 1) * HY_N2, :]] * 2, axis=1)
        ki = jnp.concatenate([ki_ref[j * HY_N2:(j + 1) * HY_N2, :]] * 2, axis=1)
        w = jnp.concatenate([zr * kr - zi * ki, zr * ki + zi * kr], axis=0)
        t = jnp.dot(g6_ref[j], _rhs3(w), preferred_element_type=f32)
        _store_halves(br, srows, t[:HY_N2])
        _store_halves(bi, srows, t[HY_N2:])

    @pl.when(s == ns - 1)
    def _():
        def body(n2, carry):
            rows = pl.ds(n2, n1, stride=HY_N2)
            rhs = _rhs3(jnp.concatenate([_load_halves(br, rows), _load_halves(bi, rows)], axis=0))
            out = jnp.dot(m8_ref[...], rhs, preferred_element_type=f32)
            orows = pl.ds(n2, half, stride=HY_N2)
            o_ref[0, orows, :] = out[:half, :cw]
            o_ref[2, orows, :] = out[:half, cw:]
            o_ref[1, orows, :] = out[half:, :cw]
            o_ref[3, orows, :] = out[half:, cw:]
            return carry
        lax.fori_loop(0, HY_N2, body, 0, unroll=HY_UNROLL)


def _hyena_conv(x, col0, khr, khi, order, tabs):
    seq_len = x.shape[1]
    n = 2 * seq_len
    n1 = n // HY_N2
    cw = 128
    sl = HY_SLABS * HY_N2
    nt = HY_WIDTH // cw
    kspec = pl.BlockSpec((sl, cw), lambda t, s: (s, order * nt + t))
    const = lambda a: pl.BlockSpec(a.shape, lambda t, s: (0,) * a.ndim)
    gspec = pl.BlockSpec((HY_SLABS,) + tabs["g4"].shape[1:], lambda t, s: (s, 0, 0))
    return pl.pallas_call(
        functools.partial(_hyconv_kernel, n1=n1),
        grid=(nt, n1 // HY_SLABS),
        in_specs=[pl.BlockSpec((BATCH, seq_len, cw), lambda t, s: (0, 0, col0 // cw + t)),
                  const(tabs["m2"]), gspec, gspec, const(tabs["m8"]), kspec, kspec],
        out_specs=pl.BlockSpec((BATCH, seq_len, cw), lambda t, s: (0, 0, t), pipeline_mode=pl.Buffered(1)),
        out_shape=jax.ShapeDtypeStruct((BATCH, seq_len, HY_WIDTH), f32),
        scratch_shapes=[pltpu.VMEM((2, n, cw), f32), pltpu.VMEM((2, n, cw), f32)],
        compiler_params=_cp("arbitrary", "arbitrary", vmem=60 * 1024 * 1024),
        name="hyena_long_conv",
    )(x, tabs["m2"], tabs["g4"], tabs["g6"], tabs["m8"], khr, khi)


def _hygate_kernel(c_ref, y_ref, g_ref, b_ref, o_ref):
    y = y_ref[...]
    o_ref[...] = (g_ref[...] * (c_ref[...] + b_ref[...] * y)).astype(o_ref.dtype)


def _hyena_gate(conv, y, ycol0, zc, gcol0, bias, out_dtype):
    tm, tn = 512, HY_WIDTH
    nrows = conv.shape[0]
    return pl.pallas_call(
        _hygate_kernel,
        grid=(nrows // tm, HY_WIDTH // tn),
        in_specs=[pl.BlockSpec((tm, tn), lambda i, j: (i, j)),
                  pl.BlockSpec((tm, tn), lambda i, j: (i, ycol0 // tn + j)),
                  pl.BlockSpec((tm, tn), lambda i, j: (i, gcol0 // tn + j)),
                  pl.BlockSpec((1, tn), lambda i, j: (0, j))],
        out_specs=pl.BlockSpec((tm, tn), lambda i, j: (i, j)),
        out_shape=jax.ShapeDtypeStruct((nrows, HY_WIDTH), out_dtype),
        compiler_params=_cp("arbitrary", "arbitrary"),
        name="hyena_gate",
    )(conv, y, zc, bias.reshape(1, HY_WIDTH))


def _short_dft_tables(seq_len):
    n = 2 * seq_len
    k = jnp.arange(n, dtype=jnp.int32)
    ang = (2.0 * math.pi / n) * ((k[:, None] * k[None, :]) % n).astype(f32)
    c, s = jnp.cos(ang), jnp.sin(ang)
    mx = _pass3(_stack_complex(c[:, :seq_len], -s[:, :seq_len]))
    mf = _pass3(jnp.concatenate([c, -s], axis=0))
    mi = _pass3(_stack_complex(c[:seq_len, :] / n, s[:seq_len, :] / n))
    return dict(mx=mx, mf=mf, mi=mi)


def _short_fft_kernel(x_ref, mf_ref, kr_ref, ki_ref):
    n = x_ref.shape[0]
    z = jnp.dot(mf_ref[...], _rhs3(x_ref[...]), preferred_element_type=f32)
    kr_ref[...] = z[:n]
    ki_ref[...] = z[n:]


def _short_filter_fft(kern, tabs):
    n, nc = kern.shape
    tn = 256
    spec = pl.BlockSpec((n, tn), lambda t: (0, t))
    return pl.pallas_call(
        _short_fft_kernel,
        grid=(nc // tn,),
        in_specs=[spec, pl.BlockSpec(tabs["mf"].shape, lambda t: (0, 0))],
        out_specs=[spec, spec],
        out_shape=[jax.ShapeDtypeStruct((n, nc), f32)] * 2,
        compiler_params=_cp("arbitrary"),
        name="hyena_short_filter_fft",
    )(kern, tabs["mf"])


def _short_conv_fft_kernel(x_ref, mx_ref, mi_ref, kr_ref, ki_ref, o_ref):
    seq_len = x_ref.shape[1]
    cw = x_ref.shape[2]
    n = 2 * seq_len
    xr = jnp.concatenate([x_ref[0], x_ref[2]], axis=1)
    xi = jnp.concatenate([x_ref[1], x_ref[3]], axis=1)
    z = jnp.dot(mx_ref[...], _rhs3(jnp.concatenate([xr, xi], axis=0)), preferred_element_type=f32)
    zr, zi = z[:n], z[n:]
    kr = jnp.concatenate([kr_ref[...]] * 2, axis=1)
    ki = jnp.concatenate([ki_ref[...]] * 2, axis=1)
    w = jnp.concatenate([zr * kr - zi * ki, zr * ki + zi * kr], axis=0)
    y = jnp.dot(mi_ref[...], _rhs3(w), preferred_element_type=f32)
    o_ref[0] = y[:seq_len, :cw]
    o_ref[2] = y[:seq_len, cw:]
    o_ref[1] = y[seq_len:, :cw]
    o_ref[3] = y[seq_len:, cw:]


def _short_long_conv(x, col0, khr, khi, order, tabs):
    seq_len = x.shape[1]
    n = 2 * seq_len
    cw = 128
    nt = HY_WIDTH // cw
    kspec = pl.BlockSpec((n, cw), lambda t: (0, order * nt + t))
    const = lambda a: pl.BlockSpec(a.shape, lambda t: (0,) * a.ndim)
    return pl.pallas_call(
        _short_conv_fft_kernel,
        grid=(nt,),
        in_specs=[pl.BlockSpec((BATCH, seq_len, cw), lambda t: (0, 0, col0 // cw + t)),
                  const(tabs["mx"]), const(tabs["mi"]), kspec, kspec],
        out_specs=pl.BlockSpec((BATCH, seq_len, cw), lambda t: (0, 0, t)),
        out_shape=jax.ShapeDtypeStruct((BATCH, seq_len, HY_WIDTH), f32),
        compiler_params=_cp("arbitrary"),
        name="hyena_short_long_conv",
    )(x, tabs["mx"], tabs["mi"], khr, khi)


def _hyena_branch(z, row0, seq_len, tabs, short_w, short_b, w1, b1, w2, b2, w3, freq, bias):
    two_stage = "g4" in tabs
    conv = _hyena_conv if two_stage else _short_long_conv
    nrows = BATCH * seq_len
    zc = _short_conv(z, short_w, short_b, row0, nrows, seq_len)
    kern = _hyena_filter(seq_len, w1, b1, w2, b2, w3, freq)
    khr, khi = (_hyena_filter_fft if two_stage else _short_filter_fft)(kern, tabs)
    zc3 = zc.reshape(BATCH, seq_len, 3 * HY_WIDTH)
    c1 = conv(zc3, 0, khr, khi, 0, tabs).reshape(nrows, HY_WIDTH)
    y1 = _hyena_gate(c1, zc, 0, zc, HY_WIDTH, bias[0], f32)
    c2 = conv(y1.reshape(BATCH, seq_len, HY_WIDTH), 0, khr, khi, 1, tabs).reshape(nrows, HY_WIDTH)
    return _hyena_gate(c2, y1, 0, zc, 2 * HY_WIDTH, bias[1], bf16)


def _rope_tables():
    half = ATT_HD // 2
    nf = half // 2
    inv = ROPE_BASE ** (-jnp.arange(nf, dtype=f32) / nf)
    pos = jnp.arange(SEQ)
    rows = (pos // GRID_W).astype(f32)[:, None] * inv[None, :]
    cols = (pos % GRID_W).astype(f32)[:, None] * inv[None, :]
    zero = jnp.zeros_like(rows)
    cos = jnp.concatenate([jnp.cos(rows)] * 2 + [jnp.cos(cols)] * 2, axis=-1)
    sin_up = jnp.concatenate([-jnp.sin(rows), zero, -jnp.sin(cols), zero], axis=-1)
    sin_dn = jnp.concatenate([zero, jnp.sin(rows), zero, jnp.sin(cols)], axis=-1)
    return cos, sin_up, sin_dn


def _rope_kernel(q_ref, k_ref, cos_ref, su_ref, sd_ref, qo_ref, ko_ref):
    cos, su, sd = cos_ref[...], su_ref[...], sd_ref[...]

    def rot(x):
        return x * cos + pltpu.roll(x, 96, 1) * su + pltpu.roll(x, 32, 1) * sd

    for h in range(ATT_HEADS):
        s = slice(h * ATT_HD, (h + 1) * ATT_HD)
        qo_ref[:, s] = rot(q_ref[:, s]).astype(qo_ref.dtype)
    for h in range(ATT_KV_HEADS):
        s = slice(h * ATT_HD, (h + 1) * ATT_HD)
        ko_ref[:, s] = rot(k_ref[:, s]).astype(ko_ref.dtype)


def _rope(z, tables):
    tm = 512
    nq = ATT_HEADS * ATT_HD
    nk = ATT_KV_HEADS * ATT_HD
    tab = pl.BlockSpec((tm, ATT_HD), lambda i: (i % (SEQ // tm), 0))
    return pl.pallas_call(
        _rope_kernel,
        grid=(NLAT // tm,),
        in_specs=[pl.BlockSpec((tm, nq), lambda i: (i, Z_Q // nq)),
                  pl.BlockSpec((tm, nk), lambda i: (i, Z_K // nk)),
                  tab, tab, tab],
        out_specs=[pl.BlockSpec((tm, nq), lambda i: (i, 0)),
                   pl.BlockSpec((tm, nk), lambda i: (i, 0))],
        out_shape=[jax.ShapeDtypeStruct((NLAT, nq), bf16), jax.ShapeDtypeStruct((NLAT, nk), bf16)],
        compiler_params=_cp("arbitrary"),
        name="rope",
    )(z, z, *tables)


_ATT_SCALE = ATT_HD ** -0.5
_NEG = float(np.finfo(np.float32).min)
_NT = (((1,), (1,)), ((), ()))


def _sink_column(sink_ref, h, rows):
    rg = lax.broadcasted_iota(jnp.int32, (rows * ATT_GROUP, 1), 0) // rows
    col = jnp.full((rows * ATT_GROUP, 1), sink_ref[h * ATT_GROUP + ATT_GROUP - 1], f32)
    for g in range(ATT_GROUP - 2, -1, -1):
        col = jnp.where(rg == g, sink_ref[h * ATT_GROUP + g], col)
    return col


def _attn_kernel(sink_ref, q_ref, kp_ref, kc_ref, kn_ref, vp_ref, vc_ref, vn_ref, kx_ref, vx_ref, o_ref):
    i = pl.program_id(1)
    nb = pl.num_programs(1)
    blk = ATT_BLOCK
    r = lax.broadcasted_iota(jnp.int32, (ATT_GROUP * blk, 3 * blk), 0) % blk
    c = lax.broadcasted_iota(jnp.int32, (ATT_GROUP * blk, 3 * blk), 1)
    lo = jnp.where(i > 0, 0, blk)
    hi = jnp.where(i < nb - 1, 3 * blk, 2 * blk)
    valid = (c >= r) & (c <= r + 2 * ATT_WINDOW) & (c >= lo) & (c < hi)
    for h in range(ATT_KV_HEADS):
        hs = slice(h * ATT_HD, (h + 1) * ATT_HD)
        k_win = jnp.concatenate([kp_ref[:, hs], kc_ref[:, hs], kn_ref[:, hs]], axis=0)
        v_win = jnp.concatenate([vp_ref[:, hs], vc_ref[:, hs], vn_ref[:, hs]], axis=0).astype(bf16)
        k_ctx = kx_ref[:, hs].astype(bf16)
        v_ctx = vx_ref[:, hs].astype(bf16)
        q = jnp.concatenate([q_ref[:, (h * ATT_GROUP + g) * ATT_HD:(h * ATT_GROUP + g + 1) * ATT_HD]
                             for g in range(ATT_GROUP)], axis=0)
        s_win = lax.dot_general(q, k_win, _NT, preferred_element_type=f32) * _ATT_SCALE
        s_win = jnp.where(valid, s_win, _NEG)
        s_ctx = lax.dot_general(q, k_ctx, _NT, preferred_element_type=f32) * _ATT_SCALE
        sink = _sink_column(sink_ref, h, blk)
        m = jnp.maximum(jnp.maximum(jnp.max(s_win, axis=-1, keepdims=True),
                                    jnp.max(s_ctx, axis=-1, keepdims=True)), sink)
        p_win = jnp.exp(s_win - m)
        p_ctx = jnp.exp(s_ctx - m)
        den = (jnp.sum(p_win, axis=-1, keepdims=True) + jnp.sum(p_ctx, axis=-1, keepdims=True)
               + jnp.exp(sink - m))
        o = (jnp.dot(p_win.astype(bf16), v_win, preferred_element_type=f32)
             + jnp.dot(p_ctx.astype(bf16), v_ctx, preferred_element_type=f32)) / den
        for g in range(ATT_GROUP):
            cs = slice((h * ATT_GROUP + g) * ATT_HD, (h * ATT_GROUP + g + 1) * ATT_HD)
            o_ref[:, cs] = o[g * blk:(g + 1) * blk].astype(o_ref.dtype)


def _window_attention(qr, kr, z, sink):
    blk = ATT_BLOCK
    nb = SEQ // blk
    nkv = ATT_KV_HEADS * ATT_HD
    cx = NLAT // CTX_LEN

    def krow(off):
        return lambda b, i: (b * nb + jnp.clip(i + off, 0, nb - 1), 0)

    def vrow(off):
        return lambda b, i: (b * nb + jnp.clip(i + off, 0, nb - 1), Z_V // nkv)

    return pl.pallas_call(
        _attn_kernel,
        grid=(BATCH, nb),
        in_specs=[pl.BlockSpec(memory_space=pltpu.SMEM),
                  pl.BlockSpec((blk, ATT_HEADS * ATT_HD), lambda b, i: (b * nb + i, 0)),
                  pl.BlockSpec((blk, nkv), krow(-1)),
                  pl.BlockSpec((blk, nkv), krow(0)),
                  pl.BlockSpec((blk, nkv), krow(1)),
                  pl.BlockSpec((blk, nkv), vrow(-1)),
                  pl.BlockSpec((blk, nkv), vrow(0)),
                  pl.BlockSpec((blk, nkv), vrow(1)),
                  pl.BlockSpec((CTX_LEN, nkv), lambda b, i: (cx + b, Z_K // nkv)),
                  pl.BlockSpec((CTX_LEN, nkv), lambda b, i: (cx + b, Z_V // nkv))],
        out_specs=pl.BlockSpec((blk, ATT_HEADS * ATT_HD), lambda b, i: (b * nb + i, 0)),
        out_shape=jax.ShapeDtypeStruct((NLAT, ATT_HEADS * ATT_HD), bf16),
        compiler_params=_cp("arbitrary", "arbitrary"),
        name="window_attention",
    )(sink, qr, kr, kr, kr, z, z, z, z, z)


def _ctx_attn_kernel(sink_ref, q_ref, k_ref, v_ref, o_ref):
    for h in range(ATT_KV_HEADS):
        hs = slice(h * ATT_HD, (h + 1) * ATT_HD)
        k = k_ref[:, hs].astype(bf16)
        v = v_ref[:, hs].astype(bf16)
        q = jnp.concatenate([q_ref[:, (h * ATT_GROUP + g) * ATT_HD:(h * ATT_GROUP + g + 1) * ATT_HD]
                             for g in range(ATT_GROUP)], axis=0).astype(bf16)
        s = lax.dot_general(q, k, _NT, preferred_element_type=f32) * _ATT_SCALE
        sink = _sink_column(sink_ref, h, CTX_LEN)
        m = jnp.maximum(jnp.max(s, axis=-1, keepdims=True), sink)
        p = jnp.exp(s - m)
        den = jnp.sum(p, axis=-1, keepdims=True) + jnp.exp(sink - m)
        o = jnp.dot(p.astype(bf16), v, preferred_element_type=f32) / den
        for g in range(ATT_GROUP):
            cs = slice((h * ATT_GROUP + g) * ATT_HD, (h * ATT_GROUP + g + 1) * ATT_HD)
            o_ref[:, cs] = o[g * CTX_LEN:(g + 1) * CTX_LEN].astype(o_ref.dtype)


def _context_attention(z, sink):
    nq = ATT_HEADS * ATT_HD
    nkv = ATT_KV_HEADS * ATT_HD
    cx = NLAT // CTX_LEN
    return pl.pallas_call(
        _ctx_attn_kernel,
        grid=(BATCH,),
        in_specs=[pl.BlockSpec(memory_space=pltpu.SMEM),
                  pl.BlockSpec((CTX_LEN, nq), lambda b: (cx + b, Z_Q // nq)),
                  pl.BlockSpec((CTX_LEN, nkv), lambda b: (cx + b, Z_K // nkv)),
                  pl.BlockSpec((CTX_LEN, nkv), lambda b: (cx + b, Z_V // nkv))],
        out_specs=pl.BlockSpec((CTX_LEN, nq), lambda b: (b, 0)),
        out_shape=jax.ShapeDtypeStruct((NCTX, nq), bf16),
        compiler_params=_cp("arbitrary"),
        name="context_attention",
    )(sink, z, z, z)


_ML_SCALE = ML_QK ** -0.5


def _split3(x):
    x1 = x.astype(bf16)
    r1 = x - x1.astype(f32)
    x2 = r1.astype(bf16)
    x3 = (r1 - x2.astype(f32)).astype(bf16)
    return x1, x2, x3


def _mlstm_kernel(q_ref, k_ref, v_ref, g_ref, gb_ref, o_ref, ct_ref, n_ref, m_ref):
    d = pl.program_id(1)
    c = pl.program_id(2)
    ch = ML_CHUNK

    @pl.when(c == 0)
    def _():
        ct_ref[...] = jnp.zeros_like(ct_ref)
        n_ref[...] = jnp.zeros_like(n_ref)
        m_ref[...] = jnp.zeros_like(m_ref)

    g = g_ref[...] + gb_ref[...]
    row = lax.broadcasted_iota(jnp.int32, (ch, ch), 0)
    col = lax.broadcasted_iota(jnp.int32, (ch, ch), 1)
    tri = (row - col) * (1 - 2 * d) >= 0
    tri_b = tri.astype(f32).astype(bf16)
    lf = jax.nn.log_sigmoid(g)
    l1, l2, l3 = _split3(lf)
    bcol = (jnp.dot(tri_b, l1, preferred_element_type=f32)
            + jnp.dot(tri_b, l2, preferred_element_type=f32)
            + jnp.dot(tri_b, l3, preferred_element_type=f32))
    bend = jnp.where(d == 0, bcol[ch - 1:ch, :], bcol[0:1, :])
    g_t = g.T
    b_t = bcol.T

    for h in range(ML_HEADS):
        qf = q_ref[:, h * ML_QK:(h + 1) * ML_QK] * _ML_SCALE
        kf = k_ref[:, h * ML_QK:(h + 1) * ML_QK]
        vf = v_ref[:, h * ML_V:(h + 1) * ML_V]
        q = qf.astype(bf16)
        k = kf.astype(bf16)
        li_c = g[:, h:h + 1]
        b_c = bcol[:, ML_HEADS + h:ML_HEADS + h + 1]
        li_r = g_t[h:h + 1, :]
        b_r = b_t[ML_HEADS + h:ML_HEADS + h + 1, :]
        m_prev = m_ref[h][:, 0:1]
        dmat = jnp.where(tri, b_c - b_r + li_r, -jnp.inf)
        inter = b_c + m_prev
        m_t = jnp.maximum(inter, jnp.max(dmat, axis=-1, keepdims=True))
        w_intra = jnp.exp(dmat - m_t)
        w_inter = jnp.exp(inter - m_t)
        s = lax.dot_general(q, k, _NT, preferred_element_type=f32) * w_intra
        qc = jnp.dot(q, ct_ref[h].astype(bf16), preferred_element_type=f32)
        num = jnp.dot(s.astype(bf16), vf.astype(bf16), preferred_element_type=f32) + w_inter * qc
        den = (jnp.sum(s, axis=-1, keepdims=True)
               + w_inter * jnp.sum(qf * n_ref[h], axis=-1, keepdims=True))
        o_ref[:, h * ML_V:(h + 1) * ML_V] = num / jnp.maximum(jnp.abs(den), jnp.exp(-m_t))

        b_e = bend[:, ML_HEADS + h:ML_HEADS + h + 1]
        g_c = b_e - b_c + li_c
        m_new = jnp.maximum(b_e + m_prev, jnp.max(g_c, axis=0, keepdims=True))
        w_s = jnp.exp(g_c - m_new)
        w_c = jnp.exp(b_e + m_prev - m_new)
        vw = (vf * w_s).astype(bf16)
        ct_ref[h] = w_c * ct_ref[h] + jnp.dot(kf.T.astype(bf16), vw, preferred_element_type=f32)
        n_ref[h] = w_c * n_ref[h] + jnp.sum(kf * w_s, axis=0, keepdims=True)
        m_ref[h] = jnp.broadcast_to(m_new, (1, 128))


def _mlstm(z, gate_b):
    ch = ML_CHUNK
    ncl = SEQ // ch
    nsteps = ncl + CTX_LEN // ch
    assert CTX_LEN == ch

    def rt(b, d, c):
        lat = b * ncl + jnp.where(d == 0, c - 1, ncl - c)
        return jnp.where(c == 0, NLAT // ch + b, lat)

    nq = ML_HEADS * ML_QK
    nv = ML_HEADS * ML_V
    return pl.pallas_call(
        _mlstm_kernel,
        grid=(BATCH, 2, nsteps),
        in_specs=[pl.BlockSpec((ch, nq), lambda b, d, c: (rt(b, d, c), Z_MQ // nq)),
                  pl.BlockSpec((ch, nq), lambda b, d, c: (rt(b, d, c), Z_MK // nq)),
                  pl.BlockSpec((ch, nv), lambda b, d, c: (rt(b, d, c), Z_MV // nv)),
                  pl.BlockSpec((ch, 128), lambda b, d, c: (rt(b, d, c), Z_GATE // 128 + d)),
                  pl.BlockSpec((None, 1, 128), lambda b, d, c: (d, 0, 0))],
        out_specs=pl.BlockSpec((None, ch, nv), lambda b, d, c: (d, rt(b, d, c), 0)),
        out_shape=jax.ShapeDtypeStruct((2, R, nv), f32),
        scratch_shapes=[pltpu.VMEM((ML_HEADS, ML_QK, ML_V), f32),
                        pltpu.VMEM((ML_HEADS, 1, ML_QK), f32),
                        pltpu.VMEM((ML_HEADS, 1, 128), f32)],
        compiler_params=_cp("arbitrary", "arbitrary", "arbitrary"),
        name="mlstm",
    )(z, z, z, z, gate_b)


def _mlstm_out_kernel(h_ref, zo_ref, g_ref, o_ref):
    for h in range(ML_HEADS):
        s = slice(h * ML_V, (h + 1) * ML_V)
        x = h_ref[0, :, s] + h_ref[1, :, s]
        xn = x * lax.rsqrt(jnp.mean(x * x, axis=-1, keepdims=True) + EPS) * g_ref[:, s]
        o_ref[:, s] = (xn * jax.nn.sigmoid(zo_ref[:, s])).astype(o_ref.dtype)


def _mlstm_out(hh, z, g, nrows):
    tm = 512
    nv = ML_HEADS * ML_V
    return pl.pallas_call(
        _mlstm_out_kernel,
        grid=(nrows // tm,),
        in_specs=[pl.BlockSpec((2, tm, nv), lambda i: (0, i, 0)),
                  pl.BlockSpec((tm, nv), lambda i: (i, Z_MO // nv)),
                  pl.BlockSpec((1, nv), lambda i: (0, 0))],
        out_specs=pl.BlockSpec((tm, nv), lambda i: (i, 0)),
        out_shape=jax.ShapeDtypeStruct((nrows, nv), bf16),
        compiler_params=_cp("arbitrary"),
        name="mlstm_out",
    )(hh, z, g.reshape(1, nv))


def _router_kernel(f_ref, w_ref, idx_ref, p_ref):
    logits = jnp.dot(f_ref[...].astype(bf16), w_ref[...], preferred_element_type=f32)
    lane = lax.broadcasted_iota(jnp.int32, logits.shape, 1).astype(f32)
    logits = jnp.where(lane < N_EXPERTS, logits, -jnp.inf)
    v1 = jnp.max(logits, axis=-1, keepdims=True)
    i1 = jnp.min(jnp.where(logits == v1, lane, 128.0), axis=-1, keepdims=True)
    rest = jnp.where(lane == i1, -jnp.inf, logits)
    v2 = jnp.max(rest, axis=-1, keepdims=True)
    i2 = jnp.min(jnp.where(rest == v2, lane, 128.0), axis=-1, keepdims=True)
    e = jnp.exp(v2 - v1)
    p1 = 1.0 / (1.0 + e)
    p2 = e / (1.0 + e)
    idx_ref[...] = jnp.where(lane == 0, i1, jnp.where(lane == 1, i2, 0.0)).astype(jnp.int32)
    p_ref[...] = jnp.where(lane == 0, p1, jnp.where(lane == 1, p2, 0.0))


def _router(f, w_router):
    tm = 512
    w = jnp.pad(w_router, ((0, 0), (0, 128 - N_EXPERTS))).astype(bf16)
    return pl.pallas_call(
        _router_kernel,
        grid=(NLAT // tm,),
        in_specs=[pl.BlockSpec((tm, D_MODEL), lambda i: (i, 0)),
                  pl.BlockSpec((D_MODEL, 128), lambda i: (0, 0))],
        out_specs=[pl.BlockSpec((tm, 128), lambda i: (i, 0)),
                   pl.BlockSpec((tm, 128), lambda i: (i, 0))],
        out_shape=[jax.ShapeDtypeStruct((NLAT, 128), jnp.int32),
                   jax.ShapeDtypeStruct((NLAT, 128), f32)],
        compiler_params=_cp("arbitrary"),
        name="router",
    )(f, w)


def _gather_kernel(tok_ref, src_ref, o_ref, buf, sem):
    base = pl.program_id(0) * MOE_TILE

    def copy(r):
        return pltpu.make_async_copy(src_ref.at[pl.ds(tok_ref[base + r], 1)], buf.at[pl.ds(r, 1)], sem)

    def start(r, carry):
        copy(r).start()
        return carry

    def wait(r, carry):
        copy(r).wait()
        return carry

    lax.fori_loop(0, MOE_TILE, start, 0, unroll=8)
    lax.fori_loop(0, MOE_TILE, wait, 0, unroll=8)
    o_ref[...] = buf[...].astype(o_ref.dtype)


def _gather_rows(src, rows, n_out, out_dtype):
    width = src.shape[1]
    return pl.pallas_call(
        _gather_kernel,
        grid_spec=pltpu.PrefetchScalarGridSpec(
            num_scalar_prefetch=1,
            grid=(n_out // MOE_TILE,),
            in_specs=[pl.BlockSpec(memory_space=pl.ANY)],
            out_specs=pl.BlockSpec((MOE_TILE, width), lambda i, tok: (i, 0)),
            scratch_shapes=[pltpu.VMEM((MOE_TILE, width), src.dtype), pltpu.SemaphoreType.DMA(())]),
        out_shape=jax.ShapeDtypeStruct((n_out, width), out_dtype),
        compiler_params=_cp("arbitrary"),
        name="gather_rows",
    )(rows, src)


def _moe_up_kernel(be_ref, nu_ref, a_ref, wg_ref, wu_ref, o_ref, wg_bf, wu_bf):
    i = pl.program_id(1)
    used = i < nu_ref[0]
    fresh = jnp.logical_or(i == 0, be_ref[i] != be_ref[jnp.maximum(i - 1, 0)])

    @pl.when(jnp.logical_and(used, fresh))
    def _():
        wg_bf[...] = wg_ref[...].astype(bf16)
        wu_bf[...] = wu_ref[...].astype(bf16)

    @pl.when(used)
    def _():
        a = a_ref[...]
        g = jnp.dot(a, wg_bf[...], preferred_element_type=f32)
        u = jnp.dot(a, wu_bf[...], preferred_element_type=f32)
        o_ref[...] = (g * jax.nn.sigmoid(g) * u).astype(o_ref.dtype)

    @pl.when(i >= nu_ref[0])
    def _():
        o_ref[...] = jnp.zeros_like(o_ref)


def _moe_up(xs, wg, wu, blk_expert, n_used):
    tn = 1024
    nblk = xs.shape[0] // MOE_TILE

    def row(j, i, be, nu):
        return (jnp.minimum(i, nu[0] - 1), 0)

    def wmap(j, i, be, nu):
        return (be[jnp.minimum(i, nu[0] - 1)], 0, j)

    return pl.pallas_call(
        _moe_up_kernel,
        grid_spec=pltpu.PrefetchScalarGridSpec(
            num_scalar_prefetch=2,
            grid=(FFN_EXPERT // tn, nblk),
            in_specs=[pl.BlockSpec((MOE_TILE, D_MODEL), row),
                      pl.BlockSpec((None, D_MODEL, tn), wmap),
                      pl.BlockSpec((None, D_MODEL, tn), wmap)],
            out_specs=pl.BlockSpec((MOE_TILE, tn), lambda j, i, be, nu: (i, j)),
            scratch_shapes=[pltpu.VMEM((D_MODEL, tn), bf16), pltpu.VMEM((D_MODEL, tn), bf16)]),
        out_shape=jax.ShapeDtypeStruct((xs.shape[0], FFN_EXPERT), bf16),
        compiler_params=_cp("arbitrary", "arbitrary"),
        name="moe_up",
    )(blk_expert, n_used, xs, wg, wu)


def _moe_down_kernel(be_ref, nu_ref, a_ref, w_ref, o_ref):
    i = pl.program_id(1)

    @pl.when(i < nu_ref[0])
    def _():
        o_ref[...] = jnp.dot(a_ref[...], w_ref[...], preferred_element_type=f32)

    @pl.when(i >= nu_ref[0])
    def _():
        o_ref[...] = jnp.zeros_like(o_ref)


def _moe_down(hs, wd, blk_expert, n_used):
    tn = 1024
    nblk = hs.shape[0] // MOE_TILE

    def row(j, i, be, nu):
        return (jnp.minimum(i, nu[0] - 1), 0)

    def wmap(j, i, be, nu):
        return (be[jnp.minimum(i, nu[0] - 1)], 0, j)

    return pl.pallas_call(
        _moe_down_kernel,
        grid_spec=pltpu.PrefetchScalarGridSpec(
            num_scalar_prefetch=2,
            grid=(D_MODEL // tn, nblk),
            in_specs=[pl.BlockSpec((MOE_TILE, FFN_EXPERT), row),
                      pl.BlockSpec((None, FFN_EXPERT, tn), wmap)],
            out_specs=pl.BlockSpec((MOE_TILE, tn), lambda j, i, be, nu: (i, j))),
        out_shape=jax.ShapeDtypeStruct((hs.shape[0], D_MODEL), f32),
        compiler_params=_cp("arbitrary", "arbitrary"),
        name="moe_down",
    )(blk_expert, n_used, hs, wd)


_COMBINE_TM = 256


def _combine_kernel(slot_ref, yb_ref, x_ref, p_ref, g2_ref, fg_ref, o_ref, buf, sem):
    i = pl.program_id(0)
    cur = i % 2

    def copy(step, b, r, k):
        s = slot_ref[(step * _COMBINE_TM + r) * TOP_K + k]
        return pltpu.make_async_copy(yb_ref.at[pl.ds(s, 1)], buf.at[b, k, pl.ds(r, 1)], sem.at[b])

    def fetch(step, b):
        def start(r, carry):
            copy(step, b, r, 0).start()
            copy(step, b, r, 1).start()
            return carry
        lax.fori_loop(0, _COMBINE_TM, start, 0, unroll=4)

    @pl.when(i == 0)
    def _():
        fetch(0, 0)

    @pl.when(i + 1 < pl.num_programs(0))
    def _():
        fetch(i + 1, 1 - cur)

    def wait(r, carry):
        copy(i, cur, r, 0).wait()
        copy(i, cur, r, 1).wait()
        return carry

    lax.fori_loop(0, _COMBINE_TM, wait, 0, unroll=8)
    p = p_ref[...]
    y = buf[cur, 0] * p[:, 0:1] + buf[cur, 1] * p[:, 1:2]
    x = x_ref[...] + g2_ref[...] * y
    o_ref[...] = x * lax.rsqrt(jnp.mean(x * x, axis=-1, keepdims=True) + EPS) * fg_ref[...]


def _combine_final(slot, yb, xa, probs, modt, final_g):
    tm = _COMBINE_TM
    return pl.pallas_call(
        _combine_kernel,
        grid_spec=pltpu.PrefetchScalarGridSpec(
            num_scalar_prefetch=1,
            grid=(NLAT // tm,),
            in_specs=[pl.BlockSpec(memory_space=pl.ANY),
                      pl.BlockSpec((tm, D_MODEL), lambda i, s: (i, 0)),
                      pl.BlockSpec((tm, 128), lambda i, s: (i, 0)),
                      pl.BlockSpec((None, 1, D_MODEL), lambda i, s: (i // (SEQ // tm), 0, 5)),
                      pl.BlockSpec((1, D_MODEL), lambda i, s: (0, 0))],
            out_specs=pl.BlockSpec((tm, D_MODEL), lambda i, s: (i, 0)),
            scratch_shapes=[pltpu.VMEM((2, TOP_K, tm, D_MODEL), f32),
                            pltpu.SemaphoreType.DMA((2,))]),
        out_shape=jax.ShapeDtypeStruct((NLAT, D_MODEL), f32),
        compiler_params=_cp("arbitrary"),
        name="moe_combine_final_norm",
    )(slot, yb, xa, probs, modt, final_g.reshape(1, D_MODEL))


def _moe_routing(top_i):
    a = NLAT * TOP_K
    e_flat = top_i.reshape(a)
    onehot = (e_flat[:, None] == jnp.arange(N_EXPERTS)[None, :]).astype(jnp.int32)
    csum = jnp.cumsum(onehot, axis=0)
    rank = jnp.sum(onehot * csum, axis=1) - 1
    counts = csum[-1]
    padded = (counts + MOE_TILE - 1) // MOE_TILE * MOE_TILE
    pad_end = jnp.cumsum(padded)
    pad_start = pad_end - padded
    slot = (pad_start[e_flat] + rank).astype(jnp.int32)
    n_rows = a + N_EXPERTS * MOE_TILE
    nblk = n_rows // MOE_TILE
    slot_tok = jnp.zeros((n_rows,), jnp.int32).at[slot].set(jnp.arange(a, dtype=jnp.int32) // TOP_K)
    blk_expert = jnp.minimum(jnp.searchsorted(pad_end, jnp.arange(nblk) * MOE_TILE, side='right'),
                             N_EXPERTS - 1).astype(jnp.int32)
    n_used = (pad_end[-1:] // MOE_TILE).astype(jnp.int32)
    return slot, slot_tok, blk_expert, n_used, n_rows


def _in_proj_weight(w, gate_b):
    o = np.cumsum((0,) + (3072, 1024, 256, 256, 512, 512, 1024, 1024, 16, 6144))
    hy, q, k, v, mq, mk, mv, mo, gt, mg = [w[:, o[i]:o[i + 1]] for i in range(10)]
    pad = jnp.zeros((D_MODEL, 128 - 2 * ML_HEADS), w.dtype)
    gates = [jnp.concatenate([gt[:, 8 * d:8 * d + 8], pad], axis=1) for d in range(2)]
    wz = jnp.concatenate([hy, q, mv, mo, mg, k, v, mq, mk] + gates
                         + [jnp.zeros((D_MODEL, Z_COLS - Z_GATE - 256), w.dtype)], axis=1).astype(bf16)
    gb = jnp.pad(gate_b.reshape(2, 1, 2 * ML_HEADS), ((0, 0), (0, 0), (0, 128 - 2 * ML_HEADS)))
    return wz, gb


def kernel(x, c, ctx, c_ctx, w_mod, b_mod, norm_mix_g, norm_ffn_g, w_in, hy_short_w, hy_short_b, hy_w1, hy_b1, hy_w2, hy_b2, hy_w3, hy_freq, hy_bias, att_sink, ml_gate_b, ml_norm_g, w_branch, w_out, ffn_wg, ffn_wu, ffn_wd, moe_router, moe_wg, moe_wu, moe_wd, final_g):
    xa = jnp.concatenate([x.reshape(NLAT, D_MODEL), ctx.reshape(NCTX, D_MODEL)], axis=0)
    c_all = jnp.concatenate([c, c_ctx[None], jnp.zeros((8 - BATCH - 1, D_MODEL), f32)], axis=0)
    mod = _modulation(c_all, w_mod, b_mod)
    rope_tabs = _rope_tables()
    dft_tabs = _dft_tables(SEQ)
    ctx_tabs = _short_dft_tables(CTX_LEN)
    out = None
    for layer in range(DEPTH):
        last = layer == DEPTH - 1
        modt = mod[layer].reshape(8, 1, 6 * D_MODEL)
        n_mix = NLAT if last else R

        u = _normmod(xa, norm_mix_g[layer], modt, 0, R)
        wz, gate_b = _in_proj_weight(w_in[layer], ml_gate_b[layer])
        z = _in_proj(u, wz)

        hy = (hy_short_w[layer], hy_short_b[layer], hy_w1[layer], hy_b1[layer], hy_w2[layer],
              hy_b2[layer], hy_w3[layer], hy_freq[layer], hy_bias[layer])
        a_rows = _hyena_branch(z, 0, SEQ, dft_tabs, *hy)

        qr, kr = _rope(z, rope_tabs)
        b_rows = _window_attention(qr, kr, z, att_sink[layer])

        hh = _mlstm(z, gate_b)
        c_rows = _mlstm_out(hh, z, ml_norm_g[layer], n_mix)

        if not last:
            a_rows = jnp.concatenate([a_rows, _hyena_branch(z, NLAT, CTX_LEN, ctx_tabs, *hy)], axis=0)
            b_rows = jnp.concatenate([b_rows, _context_attention(z, att_sink[layer])], axis=0)

        ymid = _merge(a_rows, b_rows, c_rows, w_branch[layer].astype(bf16), z, n_mix)
        xa = _mm_resid(ymid, w_out[layer].astype(bf16), xa, modt, 2, n_mix)

        f = _normmod(xa, norm_ffn_g[layer], modt, 3, n_mix, bf16 if layer % 2 == 0 else f32)
        if layer % 2 == 0:
            e = layer // 2
            hmid = _ffn_up(f, ffn_wg[e].astype(bf16), ffn_wu[e].astype(bf16))
            xa = _mm_resid(hmid, ffn_wd[e].astype(bf16), xa, modt, 5, R)
        else:
            e = layer // 2
            top_i, probs = _router(f, moe_router[e])
            slot, slot_tok, blk_expert, n_used, n_rows = _moe_routing(top_i[:, :TOP_K])
            xs = _gather_rows(f, slot_tok, n_rows, bf16)
            hs = _moe_up(xs, moe_wg[e], moe_wu[e], blk_expert, n_used)
            yb = _moe_down(hs, moe_wd[e].astype(bf16), blk_expert, n_used)
            assert last
            out = _combine_final(slot, yb, xa, probs, modt, final_g)
    return out.reshape(BATCH, SEQ, D_MODEL)
```

```python
import functools
import math

import numpy as np
import jax
import jax.numpy as jnp
from jax import lax
from jax.experimental import pallas as pl
from jax.experimental.pallas import tpu as pltpu

f32 = jnp.float32
bf16 = jnp.bfloat16

D_MODEL = 2048
BATCH = 4
SEQ = 4096
DEPTH = 2
GRID_W = 64
CTX_LEN = 256
EPS = 1e-6

HY_WIDTH = D_MODEL // 2
HY_ORDER = 2
HY_BANDS = 16
HY_EMB = 2 * HY_BANDS + 1
HY_FFN = 64
HY_MIN_DECAY = -3.0701134573253943
HY_MAX_DECAY = -15.35056728662697
HY_N2 = 128
HY_SLABS = 8
HY_UNROLL = 8

ATT_HD = 128
ATT_HEADS = 8
ATT_KV_HEADS = 2
ATT_GROUP = 4
ATT_WINDOW = 128
ATT_BLOCK = 128
ROPE_BASE = 10000.0

ML_HEADS = 4
ML_V = 256
ML_QK = 128
ML_CHUNK = 256

N_BRANCH = 3
BRANCH_W = D_MODEL // 2
FFN_DENSE = 5632
N_EXPERTS = 8
TOP_K = 2
FFN_EXPERT = 7168
MOE_TILE = 512

NCTX = BATCH * CTX_LEN
NLAT = BATCH * SEQ
R = NCTX + NLAT

Z_HY = 0
Z_Q = 3072
Z_MV = 4096
Z_MO = 5120
Z_MERGE = 6144
Z_K = 12288
Z_V = 12544
Z_MQ = 12800
Z_MK = 13312
Z_GATE = 13824
Z_COLS = 14080

VMEM_LIMIT = 56 * 1024 * 1024


def _cp(*sem, vmem=VMEM_LIMIT):
    return pltpu.CompilerParams(dimension_semantics=sem, vmem_limit_bytes=vmem)


def _modrow(i, tm):
    return jnp.where(i >= NLAT // tm, BATCH, i // (SEQ // tm))


def _mod_kernel(c_ref, w_ref, b_ref, o_ref):
    c = c_ref[...]
    a = (c * jax.nn.sigmoid(c)).astype(bf16)
    o_ref[...] = jnp.dot(a, w_ref[...].astype(bf16), preferred_element_type=f32) + b_ref[...]


def _modulation(c_all, w_mod, b_mod):
    tn = 1024
    n = 6 * D_MODEL
    return pl.pallas_call(
        _mod_kernel,
        grid=(DEPTH, n // tn),
        in_specs=[pl.BlockSpec((8, D_MODEL), lambda l, j: (0, 0)),
                  pl.BlockSpec((None, D_MODEL, tn), lambda l, j: (l, 0, j)),
                  pl.BlockSpec((None, 1, tn), lambda l, j: (l, 0, j))],
        out_specs=pl.BlockSpec((None, 8, tn), lambda l, j: (l, 0, j)),
        out_shape=jax.ShapeDtypeStruct((DEPTH, 8, n), f32),
        compiler_params=_cp("arbitrary", "arbitrary"),
        name="modulation",
    )(c_all, w_mod, b_mod.reshape(DEPTH, 1, n))


def _normmod_kernel(x_ref, g_ref, sh_ref, sc_ref, o_ref):
    x = x_ref[...]
    y = x * lax.rsqrt(jnp.mean(x * x, axis=-1, keepdims=True) + EPS) * g_ref[...]
    o_ref[...] = (y * (1.0 + sc_ref[...]) + sh_ref[...]).astype(o_ref.dtype)


def _normmod(xa, g, modt, which, nrows, out_dtype=bf16):
    tm = 1024
    return pl.pallas_call(
        _normmod_kernel,
        grid=(nrows // tm,),
        in_specs=[pl.BlockSpec((tm, D_MODEL), lambda i: (i, 0)),
                  pl.BlockSpec((1, D_MODEL), lambda i: (0, 0)),
                  pl.BlockSpec((None, 1, D_MODEL), lambda i: (_modrow(i, tm), 0, which)),
                  pl.BlockSpec((None, 1, D_MODEL), lambda i: (_modrow(i, tm), 0, which + 1))],
        out_specs=pl.BlockSpec((tm, D_MODEL), lambda i: (i, 0)),
        out_shape=jax.ShapeDtypeStruct((nrows, D_MODEL), out_dtype),
        compiler_params=_cp("arbitrary"),
        name="normmod",
    )(xa, g.reshape(1, D_MODEL), modt, modt)


def _mm_kernel(a_ref, w_ref, o_ref):
    o_ref[...] = jnp.dot(a_ref[...], w_ref[...], preferred_element_type=f32).astype(o_ref.dtype)


def _in_proj(u, w):
    tm, tn = 1024, 1280
    return pl.pallas_call(
        _mm_kernel,
        grid=(Z_COLS // tn, R // tm),
        in_specs=[pl.BlockSpec((tm, D_MODEL), lambda j, i: (i, 0)),
                  pl.BlockSpec((D_MODEL, tn), lambda j, i: (0, j))],
        out_specs=pl.BlockSpec((tm, tn), lambda j, i: (i, j)),
        out_shape=jax.ShapeDtypeStruct((R, Z_COLS), f32),
        compiler_params=_cp("arbitrary", "arbitrary"),
        name="in_proj",
    )(u, w)


def _mm_resid_kernel(a_ref, w_ref, x_ref, g_ref, o_ref):
    y = jnp.dot(a_ref[...], w_ref[...], preferred_element_type=f32)
    o_ref[...] = x_ref[...] + g_ref[...] * y


def _mm_resid(a, w, xa, modt, which, nrows):
    k = a.shape[1]
    tm, tn = (1024, 1024) if k <= D_MODEL else (512, 1024)
    nj = D_MODEL // tn
    return pl.pallas_call(
        _mm_resid_kernel,
        grid=(nj, nrows // tm),
        in_specs=[pl.BlockSpec((tm, k), lambda j, i: (i, 0)),
                  pl.BlockSpec((k, tn), lambda j, i: (0, j)),
                  pl.BlockSpec((tm, tn), lambda j, i: (i, j)),
                  pl.BlockSpec((None, 1, tn), lambda j, i: (_modrow(i, tm), 0, which * nj + j))],
        out_specs=pl.BlockSpec((tm, tn), lambda j, i: (i, j)),
        out_shape=jax.ShapeDtypeStruct((nrows, D_MODEL), f32),
        compiler_params=_cp("arbitrary", "arbitrary"),
        name="mm_resid",
    )(a, w, xa, modt)


def _swiglu_kernel(a_ref, wg_ref, wu_ref, o_ref):
    a = a_ref[...]
    g = jnp.dot(a, wg_ref[...], preferred_element_type=f32)
    u = jnp.dot(a, wu_ref[...], preferred_element_type=f32)
    o_ref[...] = (g * jax.nn.sigmoid(g) * u).astype(o_ref.dtype)


def _ffn_up(f, wg, wu):
    tm, tn = 1024, 512
    n = wg.shape[1]
    return pl.pallas_call(
        _swiglu_kernel,
        grid=(n // tn, R // tm),
        in_specs=[pl.BlockSpec((tm, D_MODEL), lambda j, i: (i, 0)),
                  pl.BlockSpec((D_MODEL, tn), lambda j, i: (0, j)),
                  pl.BlockSpec((D_MODEL, tn), lambda j, i: (0, j))],
        out_specs=pl.BlockSpec((tm, tn), lambda j, i: (i, j)),
        out_shape=jax.ShapeDtypeStruct((R, n), bf16),
        compiler_params=_cp("arbitrary", "arbitrary"),
        name="ffn_up",
    )(f, wg, wu)


def _merge_kernel(a_ref, b_ref, c_ref, w_ref, g0_ref, g1_ref, g2_ref, o_ref):
    y = jax.nn.sigmoid(g0_ref[...]) * jnp.dot(a_ref[...], w_ref[0], preferred_element_type=f32)
    y += jax.nn.sigmoid(g1_ref[...]) * jnp.dot(b_ref[...], w_ref[1], preferred_element_type=f32)
    y += jax.nn.sigmoid(g2_ref[...]) * jnp.dot(c_ref[...], w_ref[2], preferred_element_type=f32)
    o_ref[...] = y.astype(o_ref.dtype)


def _merge(a, b, c, wb, z, nrows):
    tm, tn = 512, 1024
    nj = D_MODEL // tn
    act = pl.BlockSpec((tm, BRANCH_W), lambda j, i: (i, 0))

    def gate(br):
        return pl.BlockSpec((tm, tn), lambda j, i: (i, (Z_MERGE + br * D_MODEL) // tn + j))

    return pl.pallas_call(
        _merge_kernel,
        grid=(nj, nrows // tm),
        in_specs=[act, act, act,
                  pl.BlockSpec((N_BRANCH, BRANCH_W, tn), lambda j, i: (0, 0, j)),
                  gate(0), gate(1), gate(2)],
        out_specs=pl.BlockSpec((tm, tn), lambda j, i: (i, j)),
        out_shape=jax.ShapeDtypeStruct((nrows, D_MODEL), bf16),
        compiler_params=_cp("arbitrary", "arbitrary"),
        name="merge",
    )(a, b, c, wb, z, z, z)


def _short_conv_kernel(x_ref, xp_ref, xn_ref, w_ref, b_ref, o_ref, *, tiles_per_seq):
    i = pl.program_id(0)
    tm = x_ref.shape[0]
    x = x_ref[...]
    first = i % tiles_per_seq == 0
    last = i % tiles_per_seq == tiles_per_seq - 1
    prev = jnp.where(first, 0.0, xp_ref[7:8, :])
    nxt = jnp.where(last, 0.0, xn_ref[0:1, :])
    row = lax.broadcasted_iota(jnp.int32, x.shape, 0)
    up = jnp.where(row == 0, prev, pltpu.roll(x, 1, 0))
    dn = jnp.where(row == tm - 1, nxt, pltpu.roll(x, tm - 1, 0))
    o_ref[...] = w_ref[0:1, :] * up + w_ref[1:2, :] * x + w_ref[2:3, :] * dn + b_ref[...]


def _short_conv(z, w, b, row0, nrows, seq_len):
    tm, tn = 256, 3 * HY_WIDTH
    t0 = row0 // tm
    nc = 3 * HY_WIDTH
    last8 = R // 8 - 1
    return pl.pallas_call(
        functools.partial(_short_conv_kernel, tiles_per_seq=seq_len // tm),
        grid=(nrows // tm, nc // tn),
        in_specs=[pl.BlockSpec((tm, tn), lambda i, j: (i + t0, j)),
                  pl.BlockSpec((8, tn), lambda i, j: (jnp.maximum((i + t0) * (tm // 8) - 1, 0), j)),
                  pl.BlockSpec((8, tn), lambda i, j: (jnp.minimum((i + t0 + 1) * (tm // 8), last8), j)),
                  pl.BlockSpec((3, tn), lambda i, j: (0, j)),
                  pl.BlockSpec((1, tn), lambda i, j: (0, j))],
        out_specs=pl.BlockSpec((tm, tn), lambda i, j: (i, j)),
        out_shape=jax.ShapeDtypeStruct((nrows, nc), f32),
        compiler_params=_cp("arbitrary", "arbitrary"),
        name="hyena_short_conv",
    )(z, z, z, w, b.reshape(1, nc))


def _hyfilt_kernel(w1_ref, b1_ref, w2_ref, b2_ref, w3_ref, fr_ref, dl_ref, o_ref, *, seq_len):
    tm = o_ref.shape[0]
    r = pl.program_id(0) * tm + lax.broadcasted_iota(jnp.int32, (tm, 1), 0)
    p = jnp.where(r < seq_len, r, 2 * seq_len - r)
    t = p.astype(f32) / seq_len
    lane = lax.broadcasted_iota(jnp.int32, (tm, 128), 1)
    band = jnp.where(lane <= HY_BANDS, lane, lane - HY_BANDS).astype(f32)
    ang = ((2.0 * math.pi) * t) * band
    feats = jnp.where(lane == 0, t,
                      jnp.where(lane <= HY_BANDS, jnp.sin(ang),
                                jnp.where(lane <= 2 * HY_BANDS, jnp.cos(ang), 0.0)))
    h = jnp.dot(feats.astype(bf16), w1_ref[...].astype(bf16), preferred_element_type=f32) + b1_ref[...]
    h = jnp.sin(fr_ref[0:1, :] * h)
    h = jnp.dot(h.astype(bf16), w2_ref[...].astype(bf16), preferred_element_type=f32) + b2_ref[...]
    h = jnp.sin(fr_ref[1:2, :] * h)
    k = jnp.dot(h.astype(bf16), w3_ref[...].astype(bf16), preferred_element_type=f32)
    k = k * jnp.exp(-t * dl_ref[...])
    o_ref[...] = jnp.where(r == seq_len, 0.0, k)


def _hyena_filter(seq_len, w1, b1, w2, b2, w3, freq):
    nc = HY_ORDER * HY_WIDTH
    tm, tn = min(512, seq_len), nc
    deltas = jnp.abs(jnp.linspace(HY_MIN_DECAY, HY_MAX_DECAY, HY_WIDTH, dtype=f32))
    dl = jnp.tile(deltas, HY_ORDER).reshape(1, nc)
    w1p = jnp.pad(w1, ((0, 128 - HY_EMB), (0, 0)))
    per_dir = nc // tn
    full = lambda i, j: (0, 0)
    return pl.pallas_call(
        functools.partial(_hyfilt_kernel, seq_len=seq_len),
        grid=(2 * seq_len // tm, per_dir),
        in_specs=[pl.BlockSpec((128, HY_FFN), full),
                  pl.BlockSpec((1, HY_FFN), full),
                  pl.BlockSpec((HY_FFN, HY_FFN), full),
                  pl.BlockSpec((1, HY_FFN), full),
                  pl.BlockSpec((HY_FFN, tn), lambda i, j: (0, jnp.where(i >= seq_len // tm, per_dir, 0) + j)),
                  pl.BlockSpec((2, HY_FFN), full),
                  pl.BlockSpec((1, tn), lambda i, j: (0, j))],
        out_specs=pl.BlockSpec((tm, tn), lambda i, j: (i, j)),
        out_shape=jax.ShapeDtypeStruct((2 * seq_len, nc), f32),
        compiler_params=_cp("arbitrary", "arbitrary"),
        name="hyena_filter",
    )(w1p, b1.reshape(1, HY_FFN), w2, b2.reshape(1, HY_FFN), w3, freq, dl)


def _pass3(m):
    return m.astype(bf16)


def _rhs3(x):
    return x.astype(bf16)


def _stack_complex(ar, ai):
    return jnp.concatenate([jnp.concatenate([ar, -ai], axis=-1),
                            jnp.concatenate([ai, ar], axis=-1)], axis=-2)


def _dft_tables(seq_len):
    n = 2 * seq_len
    n1 = n // HY_N2
    half = n1 // 2
    i1 = jnp.arange(n1, dtype=jnp.int32)
    ang1 = (2.0 * math.pi / n1) * ((i1[:, None] * i1[None, :]) % n1).astype(f32)
    c1, s1 = jnp.cos(ang1), jnp.sin(ang1)
    m2 = _pass3(_stack_complex(c1[:, :half], -s1[:, :half]))
    m2f = _pass3(jnp.concatenate([c1, -s1], axis=0))
    m8 = _pass3(_stack_complex(c1[:half, :] / n, s1[:half, :] / n))
    i2 = jnp.arange(HY_N2, dtype=jnp.int32)
    k = i1[:, None, None] + n1 * i2[None, :, None]
    ang = (2.0 * math.pi / n) * ((i2[None, None, :] * k) % n).astype(f32)
    c, s = jnp.cos(ang), jnp.sin(ang)
    g4 = _pass3(_stack_complex(c, -s))
    ct, st = jnp.swapaxes(c, 1, 2), jnp.swapaxes(s, 1, 2)
    g6 = _pass3(_stack_complex(ct, st))
    return dict(m2=m2, m2f=m2f, m8=m8, g4=g4, g6=g6)


def _store_halves(ref, rows, val):
    ref[0, rows, :] = val[:, :128]
    ref[1, rows, :] = val[:, 128:]


def _load_halves(ref, rows):
    return jnp.concatenate([ref[0, rows, :], ref[1, rows, :]], axis=1)


def _hyfft_kernel(xa_ref, xb_ref, m2_ref, g4_ref, kr_ref, ki_ref, br, bi, *, n1):
    s = pl.program_id(1)

    @pl.when(s == 0)
    def _():
        def body(n2, carry):
            rows = pl.ds(n2, n1, stride=HY_N2)
            rhs = _rhs3(jnp.concatenate([xa_ref[rows, :], xb_ref[rows, :]], axis=1))
            out = jnp.dot(m2_ref[...], rhs, preferred_element_type=f32)
            _store_halves(br, rows, out[:n1])
            _store_halves(bi, rows, out[n1:])
            return carry
        lax.fori_loop(0, HY_N2, body, 0, unroll=HY_UNROLL)

    for j in range(HY_SLABS):
        rows = pl.ds(pl.multiple_of((s * HY_SLABS + j) * HY_N2, HY_N2), HY_N2)
        y = jnp.concatenate([_load_halves(br, rows), _load_halves(bi, rows)], axis=0)
        z = jnp.dot(g4_ref[j], _rhs3(y), preferred_element_type=f32)
        kr_ref[j * HY_N2:(j + 1) * HY_N2, :] = z[:HY_N2]
        ki_ref[j * HY_N2:(j + 1) * HY_N2, :] = z[HY_N2:]


def _hyena_filter_fft(kern, tabs):
    n, nc = kern.shape
    n1 = n // HY_N2
    tn = 256
    sl = HY_SLABS * HY_N2
    spec_out = pl.BlockSpec((sl, tn), lambda t, s: (s, t))
    return pl.pallas_call(
        functools.partial(_hyfft_kernel, n1=n1),
        grid=(nc // tn, n1 // HY_SLABS),
        in_specs=[pl.BlockSpec((n, 128), lambda t, s: (0, 2 * t)),
                  pl.BlockSpec((n, 128), lambda t, s: (0, 2 * t + 1)),
                  pl.BlockSpec(tabs["m2f"].shape, lambda t, s: (0, 0)),
                  pl.BlockSpec((HY_SLABS,) + tabs["g4"].shape[1:], lambda t, s: (s, 0, 0))],
        out_specs=[spec_out, spec_out],
        out_shape=[jax.ShapeDtypeStruct((n, nc), f32)] * 2,
        scratch_shapes=[pltpu.VMEM((2, n, 128), f32), pltpu.VMEM((2, n, 128), f32)],
        compiler_params=_cp("arbitrary", "arbitrary"),
        name="hyena_filter_fft",
    )(kern, kern, tabs["m2f"], tabs["g4"])


def _hyconv_kernel(x_ref, m2_ref, g4_ref, g6_ref, m8_ref, kr_ref, ki_ref, o_ref, br, bi, *, n1):
    s = pl.program_id(1)
    ns = pl.num_programs(1)
    half = n1 // 2
    cw = x_ref.shape[2]

    @pl.when(s == 0)
    def _():
        def body(n2, carry):
            rows = pl.ds(n2, half, stride=HY_N2)
            xr = jnp.concatenate([x_ref[0, rows, :], x_ref[2, rows, :]], axis=1)
            xi = jnp.concatenate([x_ref[1, rows, :], x_ref[3, rows, :]], axis=1)
            rhs = _rhs3(jnp.concatenate([xr, xi], axis=0))
            out = jnp.dot(m2_ref[...], rhs, preferred_element_type=f32)
            brows = pl.ds(n2, n1, stride=HY_N2)
            _store_halves(br, brows, out[:n1])
            _store_halves(bi, brows, out[n1:])
            return carry
        lax.fori_loop(0, HY_N2, body, 0, unroll=HY_UNROLL)

    for j in range(HY_SLABS):
        srows = pl.ds(pl.multiple_of((s * HY_SLABS + j) * HY_N2, HY_N2), HY_N2)
        y = jnp.concatenate([_load_halves(br, srows), _load_halves(bi, srows)], axis=0)
        z = jnp.dot(g4_ref[j], _rhs3(y), preferred_element_type=f32)
        zr, zi = z[:HY_N2], z[HY_N2:]
        kr = jnp.concatenate([kr_ref[j * HY_N2:(j + 1) * HY_N2, :]] * 2, axis=1)
        ki = jnp.concatenate([ki_ref[j * HY_N2:(j + 1) * HY_N2, :]] * 2, axis=1)
        w = jnp.concatenate([zr * kr - zi * ki, zr * ki + zi * kr], axis=0)
        t = jnp.dot(g6_ref[j], _rhs3(w), preferred_element_type=f32)
        _store_halves(br, srows, t[:HY_N2])
        _store_halves(bi, srows, t[HY_N2:])

    @pl.when(s == ns - 1)
    def _():
        def body(n2, carry):
            rows = pl.ds(n2, n1, stride=HY_N2)
            rhs = _rhs3(jnp.concatenate([_load_halves(br, rows), _load_halves(bi, rows)], axis=0))
            out = jnp.dot(m8_ref[...], rhs, preferred_element_type=f32)
            orows = pl.ds(n2, half, stride=HY_N2)
            o_ref[0, orows, :] = out[:half, :cw]
            o_ref[2, orows, :] = out[:half, cw:]
            o_ref[1, orows, :] = out[half:, :cw]
            o_ref[3, orows, :] = out[half:, cw:]
            return carry
        lax.fori_loop(0, HY_N2, body, 0, unroll=HY_UNROLL)


def _hyena_conv(x, col0, khr, khi, order, tabs):
    seq_len = x.shape[1]
    n = 2 * seq_len
    n1 = n // HY_N2
    cw = 128
    sl = HY_SLABS * HY_N2
    nt = HY_WIDTH // cw
    kspec = pl.BlockSpec((sl, cw), lambda t, s: (s, order * nt + t))
    const = lambda a: pl.BlockSpec(a.shape, lambda t, s: (0,) * a.ndim)
    gspec = pl.BlockSpec((HY_SLABS,) + tabs["g4"].shape[1:], lambda t, s: (s, 0, 0))
    return pl.pallas_call(
        functools.partial(_hyconv_kernel, n1=n1),
        grid=(nt, n1 // HY_SLABS),
        in_specs=[pl.BlockSpec((BATCH, seq_len, cw), lambda t, s: (0, 0, col0 // cw + t)),
                  const(tabs["m2"]), gspec, gspec, const(tabs["m8"]), kspec, kspec],
        out_specs=pl.BlockSpec((BATCH, seq_len, cw), lambda t, s: (0, 0, t), pipeline_mode=pl.Buffered(1)),
        out_shape=jax.ShapeDtypeStruct((BATCH, seq_len, HY_WIDTH), f32),
        scratch_shapes=[pltpu.VMEM((2, n, cw), f32), pltpu.VMEM((2, n, cw), f32)],
        compiler_params=_cp("arbitrary", "arbitrary", vmem=60 * 1024 * 1024),
        name="hyena_long_conv",
    )(x, tabs["m2"], tabs["g4"], tabs["g6"], tabs["m8"], khr, khi)


def _hygate_kernel(c_ref, y_ref, g_ref, b_ref, o_ref):
    y = y_ref[...]
    o_ref[...] = (g_ref[...] * (c_ref[...] + b_ref[...] * y)).astype(o_ref.dtype)


def _hyena_gate(conv, y, ycol0, zc, gcol0, bias, out_dtype):
    tm, tn = 512, HY_WIDTH
    nrows = conv.shape[0]
    return pl.pallas_call(
        _hygate_kernel,
        grid=(nrows // tm, HY_WIDTH // tn),
        in_specs=[pl.BlockSpec((tm, tn), lambda i, j: (i, j)),
                  pl.BlockSpec((tm, tn), lambda i, j: (i, ycol0 // tn + j)),
                  pl.BlockSpec((tm, tn), lambda i, j: (i, gcol0 // tn + j)),
                  pl.BlockSpec((1, tn), lambda i, j: (0, j))],
        out_specs=pl.BlockSpec((tm, tn), lambda i, j: (i, j)),
        out_shape=jax.ShapeDtypeStruct((nrows, HY_WIDTH), out_dtype),
        compiler_params=_cp("arbitrary", "arbitrary"),
        name="hyena_gate",
    )(conv, y, zc, bias.reshape(1, HY_WIDTH))


def _short_dft_tables(seq_len):
    n = 2 * seq_len
    k = jnp.arange(n, dtype=jnp.int32)
    ang = (2.0 * math.pi / n) * ((k[:, None] * k[None, :]) % n).astype(f32)
    c, s = jnp.cos(ang), jnp.sin(ang)
    mx = _pass3(_stack_complex(c[:, :seq_len], -s[:, :seq_len]))
    mf = _pass3(jnp.concatenate([c, -s], axis=0))
    mi = _pass3(_stack_complex(c[:seq_len, :] / n, s[:seq_len, :] / n))
    return dict(mx=mx, mf=mf, mi=mi)


def _short_fft_kernel(x_ref, mf_ref, kr_ref, ki_ref):
    n = x_ref.shape[0]
    z = jnp.dot(mf_ref[...], _rhs3(x_ref[...]), preferred_element_type=f32)
    kr_ref[...] = z[:n]
    ki_ref[...] = z[n:]


def _short_filter_fft(kern, tabs):
    n, nc = kern.shape
    tn = 256
    spec = pl.BlockSpec((n, tn), lambda t: (0, t))
    return pl.pallas_call(
        _short_fft_kernel,
        grid=(nc // tn,),
        in_specs=[spec, pl.BlockSpec(tabs["mf"].shape, lambda t: (0, 0))],
        out_specs=[spec, spec],
        out_shape=[jax.ShapeDtypeStruct((n, nc), f32)] * 2,
        compiler_params=_cp("arbitrary"),
        name="hyena_short_filter_fft",
    )(kern, tabs["mf"])


def _short_conv_fft_kernel(x_ref, mx_ref, mi_ref, kr_ref, ki_ref, o_ref):
    seq_len = x_ref.shape[1]
    cw = x_ref.shape[2]
    n = 2 * seq_len
    xr = jnp.concatenate([x_ref[0], x_ref[2]], axis=1)
    xi = jnp.concatenate([x_ref[1], x_ref[3]], axis=1)
    z = jnp.dot(mx_ref[...], _rhs3(jnp.concatenate([xr, xi], axis=0)), preferred_element_type=f32)
    zr, zi = z[:n], z[n:]
    kr = jnp.concatenate([kr_ref[...]] * 2, axis=1)
    ki = jnp.concatenate([ki_ref[...]] * 2, axis=1)
    w = jnp.concatenate([zr * kr - zi * ki, zr * ki + zi * kr], axis=0)
    y = jnp.dot(mi_ref[...], _rhs3(w), preferred_element_type=f32)
    o_ref[0] = y[:seq_len, :cw]
    o_ref[2] = y[:seq_len, cw:]
    o_ref[1] = y[seq_len:, :cw]
    o_ref[3] = y[seq_len:, cw:]


def _short_long_conv(x, col0, khr, khi, order, tabs):
    seq_len = x.shape[1]
    n = 2 * seq_len
    cw = 128
    nt = HY_WIDTH // cw
    kspec = pl.BlockSpec((n, cw), lambda t: (0, order * nt + t))
    const = lambda a: pl.BlockSpec(a.shape, lambda t: (0,) * a.ndim)
    return pl.pallas_call(
        _short_conv_fft_kernel,
        grid=(nt,),
        in_specs=[pl.BlockSpec((BATCH, seq_len, cw), lambda t: (0, 0, col0 // cw + t)),
                  const(tabs["mx"]), const(tabs["mi"]), kspec, kspec],
        out_specs=pl.BlockSpec((BATCH, seq_len, cw), lambda t: (0, 0, t)),
        out_shape=jax.ShapeDtypeStruct((BATCH, seq_len, HY_WIDTH), f32),
        compiler_params=_cp("arbitrary"),
        name="hyena_short_long_conv",
    )(x, tabs["mx"], tabs["mi"], khr, khi)


def _hyena_branch(z, row0, seq_len, tabs, short_w, short_b, w1, b1, w2, b2, w3, freq, bias):
    two_stage = "g4" in tabs
    conv = _hyena_conv if two_stage else _short_long_conv
    nrows = BATCH * seq_len
    zc = _short_conv(z, short_w, short_b, row0, nrows, seq_len)
    kern = _hyena_filter(seq_len, w1, b1, w2, b2, w3, freq)
    khr, khi = (_hyena_filter_fft if two_stage else _short_filter_fft)(kern, tabs)
    zc3 = zc.reshape(BATCH, seq_len, 3 * HY_WIDTH)
    c1 = conv(zc3, 0, khr, khi, 0, tabs).reshape(nrows, HY_WIDTH)
    y1 = _hyena_gate(c1, zc, 0, zc, HY_WIDTH, bias[0], f32)
    c2 = conv(y1.reshape(BATCH, seq_len, HY_WIDTH), 0, khr, khi, 1, tabs).reshape(nrows, HY_WIDTH)
    return _hyena_gate(c2, y1, 0, zc, 2 * HY_WIDTH, bias[1], bf16)


def _rope_tables():
    half = ATT_HD // 2
    nf = half // 2
    inv = ROPE_BASE ** (-jnp.arange(nf, dtype=f32) / nf)
    pos = jnp.arange(SEQ)
    rows = (pos // GRID_W).astype(f32)[:, None] * inv[None, :]
    cols = (pos % GRID_W).astype(f32)[:, None] * inv[None, :]
    zero = jnp.zeros_like(rows)
    cos = jnp.concatenate([jnp.cos(rows)] * 2 + [jnp.cos(cols)] * 2, axis=-1)
    sin_up = jnp.concatenate([-jnp.sin(rows), zero, -jnp.sin(cols), zero], axis=-1)
    sin_dn = jnp.concatenate([zero, jnp.sin(rows), zero, jnp.sin(cols)], axis=-1)
    return cos, sin_up, sin_dn


def _rope_kernel(q_ref, k_ref, cos_ref, su_ref, sd_ref, qo_ref, ko_ref):
    cos, su, sd = cos_ref[...], su_ref[...], sd_ref[...]

    def rot(x):
        return x * cos + pltpu.roll(x, 96, 1) * su + pltpu.roll(x, 32, 1) * sd

    for h in range(ATT_HEADS):
        s = slice(h * ATT_HD, (h + 1) * ATT_HD)
        qo_ref[:, s] = rot(q_ref[:, s]).astype(qo_ref.dtype)
    for h in range(ATT_KV_HEADS):
        s = slice(h * ATT_HD, (h + 1) * ATT_HD)
        ko_ref[:, s] = rot(k_ref[:, s]).astype(ko_ref.dtype)


def _rope(z, tables):
    tm = 512
    nq = ATT_HEADS * ATT_HD
    nk = ATT_KV_HEADS * ATT_HD
    tab = pl.BlockSpec((tm, ATT_HD), lambda i: (i % (SEQ // tm), 0))
    return pl.pallas_call(
        _rope_kernel,
        grid=(NLAT // tm,),
        in_specs=[pl.BlockSpec((tm, nq), lambda i: (i, Z_Q // nq)),
                  pl.BlockSpec((tm, nk), lambda i: (i, Z_K // nk)),
                  tab, tab, tab],
        out_specs=[pl.BlockSpec((tm, nq), lambda i: (i, 0)),
                   pl.BlockSpec((tm, nk), lambda i: (i, 0))],
        out_shape=[jax.ShapeDtypeStruct((NLAT, nq), bf16), jax.ShapeDtypeStruct((NLAT, nk), bf16)],
        compiler_params=_cp("arbitrary"),
        name="rope",
    )(z, z, *tables)


_ATT_SCALE = ATT_HD ** -0.5
_NEG = float(np.finfo(np.float32).min)
_NT = (((1,), (1,)), ((), ()))


def _sink_column(sink_ref, h, rows):
    rg = lax.broadcasted_iota(jnp.int32, (rows * ATT_GROUP, 1), 0) // rows
    col = jnp.full((rows * ATT_GROUP, 1), sink_ref[h * ATT_GROUP + ATT_GROUP - 1], f32)
    for g in range(ATT_GROUP - 2, -1, -1):
        col = jnp.where(rg == g, sink_ref[h * ATT_GROUP + g], col)
    return col


def _attn_kernel(sink_ref, q_ref, kp_ref, kc_ref, kn_ref, vp_ref, vc_ref, vn_ref, kx_ref, vx_ref, o_ref):
    i = pl.program_id(1)
    nb = pl.num_programs(1)
    blk = ATT_BLOCK
    r = lax.broadcasted_iota(jnp.int32, (ATT_GROUP * blk, 3 * blk), 0) % blk
    c = lax.broadcasted_iota(jnp.int32, (ATT_GROUP * blk, 3 * blk), 1)
    lo = jnp.where(i > 0, 0, blk)
    hi = jnp.where(i < nb - 1, 3 * blk, 2 * blk)
    valid = (c >= r) & (c <= r + 2 * ATT_WINDOW) & (c >= lo) & (c < hi)
    for h in range(ATT_KV_HEADS):
        hs = slice(h * ATT_HD, (h + 1) * ATT_HD)
        k_win = jnp.concatenate([kp_ref[:, hs], kc_ref[:, hs], kn_ref[:, hs]], axis=0)
        v_win = jnp.concatenate([vp_ref[:, hs], vc_ref[:, hs], vn_ref[:, hs]], axis=0).astype(bf16)
        k_ctx = kx_ref[:, hs].astype(bf16)
        v_ctx = vx_ref[:, hs].astype(bf16)
        q = jnp.concatenate([q_ref[:, (h * ATT_GROUP + g) * ATT_HD:(h * ATT_GROUP + g + 1) * ATT_HD]
                             for g in range(ATT_GROUP)], axis=0)
        s_win = lax.dot_general(q, k_win, _NT, preferred_element_type=f32) * _ATT_SCALE
        s_win = jnp.where(valid, s_win, _NEG)
        s_ctx = lax.dot_general(q, k_ctx, _NT, preferred_element_type=f32) * _ATT_SCALE
        sink = _sink_column(sink_ref, h, blk)
        m = jnp.maximum(jnp.maximum(jnp.max(s_win, axis=-1, keepdims=True),
                                    jnp.max(s_ctx, axis=-1, keepdims=True)), sink)
        p_win = jnp.exp(s_win - m)
        p_ctx = jnp.exp(s_ctx - m)
        den = (jnp.sum(p_win, axis=-1, keepdims=True) + jnp.sum(p_ctx, axis=-1, keepdims=True)
               + jnp.exp(sink - m))
        o = (jnp.dot(p_win.astype(bf16), v_win, preferred_element_type=f32)
             + jnp.dot(p_ctx.astype(bf16), v_ctx, preferred_element_type=f32)) / den
        for g in range(ATT_GROUP):
            cs = slice((h * ATT_GROUP + g) * ATT_HD, (h * ATT_GROUP + g + 1) * ATT_HD)
            o_ref[:, cs] = o[g * blk:(g + 1) * blk].astype(o_ref.dtype)


def _window_attention(qr, kr, z, sink):
    blk = ATT_BLOCK
    nb = SEQ // blk
    nkv = ATT_KV_HEADS * ATT_HD
    cx = NLAT // CTX_LEN

    def krow(off):
        return lambda b, i: (b * nb + jnp.clip(i + off, 0, nb - 1), 0)

    def vrow(off):
        return lambda b, i: (b * nb + jnp.clip(i + off, 0, nb - 1), Z_V // nkv)

    return pl.pallas_call(
        _attn_kernel,
        grid=(BATCH, nb),
        in_specs=[pl.BlockSpec(memory_space=pltpu.SMEM),
                  pl.BlockSpec((blk, ATT_HEADS * ATT_HD), lambda b, i: (b * nb + i, 0)),
                  pl.BlockSpec((blk, nkv), krow(-1)),
                  pl.BlockSpec((blk, nkv), krow(0)),
                  pl.BlockSpec((blk, nkv), krow(1)),
                  pl.BlockSpec((blk, nkv), vrow(-1)),
                  pl.BlockSpec((blk, nkv), vrow(0)),
                  pl.BlockSpec((blk, nkv), vrow(1)),
                  pl.BlockSpec((CTX_LEN, nkv), lambda b, i: (cx + b, Z_K // nkv)),
                  pl.BlockSpec((CTX_LEN, nkv), lambda b, i: (cx + b, Z_V // nkv))],
        out_specs=pl.BlockSpec((blk, ATT_HEADS * ATT_HD), lambda b, i: (b * nb + i, 0)),
        out_shape=jax.ShapeDtypeStruct((NLAT, ATT_HEADS * ATT_HD), bf16),
        compiler_params=_cp("arbitrary", "arbitrary"),
        name="window_attention",
    )(sink, qr, kr, kr, kr, z, z, z, z, z)


def _ctx_attn_kernel(sink_ref, q_ref, k_ref, v_ref, o_ref):
    for h in range(ATT_KV_HEADS):
        hs = slice(h * ATT_HD, (h + 1) * ATT_HD)
        k = k_ref[:, hs].astype(bf16)
        v = v_ref[:, hs].astype(bf16)
        q = jnp.concatenate([q_ref[:, (h * ATT_GROUP + g) * ATT_HD:(h * ATT_GROUP + g + 1) * ATT_HD]
                             for g in range(ATT_GROUP)], axis=0).astype(bf16)
        s = lax.dot_general(q, k, _NT, preferred_element_type=f32) * _ATT_SCALE
        sink = _sink_column(sink_ref, h, CTX_LEN)
        m = jnp.maximum(jnp.max(s, axis=-1, keepdims=True), sink)
        p = jnp.exp(s - m)
        den = jnp.sum(p, axis=-1, keepdims=True) + jnp.exp(sink - m)
        o = jnp.dot(p.astype(bf16), v, preferred_element_type=f32) / den
        for g in range(ATT_GROUP):
            cs = slice((h * ATT_GROUP + g) * ATT_HD, (h * ATT_GROUP + g + 1) * ATT_HD)
            o_ref[:, cs] = o[g * CTX_LEN:(g + 1) * CTX_LEN].astype(o_ref.dtype)


def _context_attention(z, sink):
    nq = ATT_HEADS * ATT_HD
    nkv = ATT_KV_HEADS * ATT_HD
    cx = NLAT // CTX_LEN
    return pl.pallas_call(
        _ctx_attn_kernel,
        grid=(BATCH,),
        in_specs=[pl.BlockSpec(memory_space=pltpu.SMEM),
                  pl.BlockSpec((CTX_LEN, nq), lambda b: (cx + b, Z_Q // nq)),
                  pl.BlockSpec((CTX_LEN, nkv), lambda b: (cx + b, Z_K // nkv)),
                  pl.BlockSpec((CTX_LEN, nkv), lambda b: (cx + b, Z_V // nkv))],
        out_specs=pl.BlockSpec((CTX_LEN, nq), lambda b: (b, 0)),
        out_shape=jax.ShapeDtypeStruct((NCTX, nq), bf16),
        compiler_params=_cp("arbitrary"),
        name="context_attention",
    )(sink, z, z, z)


_ML_SCALE = ML_QK ** -0.5


def _split3(x):
    x1 = x.astype(bf16)
    r1 = x - x1.astype(f32)
    x2 = r1.astype(bf16)
    x3 = (r1 - x2.astype(f32)).astype(bf16)
    return x1, x2, x3


def _mlstm_kernel(q_ref, k_ref, v_ref, g_ref, gb_ref, o_ref, ct_ref, n_ref, m_ref):
    d = pl.program_id(1)
    c = pl.program_id(2)
    ch = ML_CHUNK

    @pl.when(c == 0)
    def _():
        ct_ref[...] = jnp.zeros_like(ct_ref)
        n_ref[...] = jnp.zeros_like(n_ref)
        m_ref[...] = jnp.zeros_like(m_ref)

    g = g_ref[...] + gb_ref[...]
    row = lax.broadcasted_iota(jnp.int32, (ch, ch), 0)
    col = lax.broadcasted_iota(jnp.int32, (ch, ch), 1)
    tri = (row - col) * (1 - 2 * d) >= 0
    tri_b = tri.astype(f32).astype(bf16)
    lf = jax.nn.log_sigmoid(g)
    l1, l2, l3 = _split3(lf)
    bcol = (jnp.dot(tri_b, l1, preferred_element_type=f32)
            + jnp.dot(tri_b, l2, preferred_element_type=f32)
            + jnp.dot(tri_b, l3, preferred_element_type=f32))
    bend = jnp.where(d == 0, bcol[ch - 1:ch, :], bcol[0:1, :])
    g_t = g.T
    b_t = bcol.T

    for h in range(ML_HEADS):
        qf = q_ref[:, h * ML_QK:(h + 1) * ML_QK] * _ML_SCALE
        kf = k_ref[:, h * ML_QK:(h + 1) * ML_QK]
        vf = v_ref[:, h * ML_V:(h + 1) * ML_V]
        q = qf.astype(bf16)
        k = kf.astype(bf16)
        li_c = g[:, h:h + 1]
        b_c = bcol[:, ML_HEADS + h:ML_HEADS + h + 1]
        li_r = g_t[h:h + 1, :]
        b_r = b_t[ML_HEADS + h:ML_HEADS + h + 1, :]
        m_prev = m_ref[h][:, 0:1]
        dmat = jnp.where(tri, b_c - b_r + li_r, -jnp.inf)
        inter = b_c + m_prev
        m_t = jnp.maximum(inter, jnp.max(dmat, axis=-1, keepdims=True))
        w_intra = jnp.exp(dmat - m_t)
        w_inter = jnp.exp(inter - m_t)
        s = lax.dot_general(q, k, _NT, preferred_element_type=f32) * w_intra
        qc = jnp.dot(q, ct_ref[h].astype(bf16), preferred_element_type=f32)
        num = jnp.dot(s.astype(bf16), vf.astype(bf16), preferred_element_type=f32) + w_inter * qc
        den = (jnp.sum(s, axis=-1, keepdims=True)
               + w_inter * jnp.sum(qf * n_ref[h], axis=-1, keepdims=True))
        o_ref[:, h * ML_V:(h + 1) * ML_V] = num / jnp.maximum(jnp.abs(den), jnp.exp(-m_t))

        b_e = bend[:, ML_HEADS + h:ML_HEADS + h + 1]
        g_c = b_e - b_c + li_c
        m_new = jnp.maximum(b_e + m_prev, jnp.max(g_c, axis=0, keepdims=True))
        w_s = jnp.exp(g_c - m_new)
        w_c = jnp.exp(b_e + m_prev - m_new)
        vw = (vf * w_s).astype(bf16)
        ct_ref[h] = w_c * ct_ref[h] + jnp.dot(kf.T.astype(bf16), vw, preferred_element_type=f32)
        n_ref[h] = w_c * n_ref[h] + jnp.sum(kf * w_s, axis=0, keepdims=True)
        m_ref[h] = jnp.broadcast_to(m_new, (1, 128))


def _mlstm(z, gate_b):
    ch = ML_CHUNK
    ncl = SEQ // ch
    nsteps = ncl + CTX_LEN // ch
    assert CTX_LEN == ch

    def rt(b, d, c):
        lat = b * ncl + jnp.where(d == 0, c - 1, ncl - c)
        return jnp.where(c == 0, NLAT // ch + b, lat)

    nq = ML_HEADS * ML_QK
    nv = ML_HEADS * ML_V
    return pl.pallas_call(
        _mlstm_kernel,
        grid=(BATCH, 2, nsteps),
        in_specs=[pl.BlockSpec((ch, nq), lambda b, d, c: (rt(b, d, c), Z_MQ // nq)),
                  pl.BlockSpec((ch, nq), lambda b, d, c: (rt(b, d, c), Z_MK // nq)),
                  pl.BlockSpec((ch, nv), lambda b, d, c: (rt(b, d, c), Z_MV // nv)),
                  pl.BlockSpec((ch, 128), lambda b, d, c: (rt(b, d, c), Z_GATE // 128 + d)),
                  pl.BlockSpec((None, 1, 128), lambda b, d, c: (d, 0, 0))],
        out_specs=pl.BlockSpec((None, ch, nv), lambda b, d, c: (d, rt(b, d, c), 0)),
        out_shape=jax.ShapeDtypeStruct((2, R, nv), f32),
        scratch_shapes=[pltpu.VMEM((ML_HEADS, ML_QK, ML_V), f32),
                        pltpu.VMEM((ML_HEADS, 1, ML_QK), f32),
                        pltpu.VMEM((ML_HEADS, 1, 128), f32)],
        compiler_params=_cp("arbitrary", "arbitrary", "arbitrary"),
        name="mlstm",
    )(z, z, z, z, gate_b)


def _mlstm_out_kernel(h_ref, zo_ref, g_ref, o_ref):
    for h in range(ML_HEADS):
        s = slice(h * ML_V, (h + 1) * ML_V)
        x = h_ref[0, :, s] + h_ref[1, :, s]
        xn = x * lax.rsqrt(jnp.mean(x * x, axis=-1, keepdims=True) + EPS) * g_ref[:, s]
        o_ref[:, s] = (xn * jax.nn.sigmoid(zo_ref[:, s])).astype(o_ref.dtype)


def _mlstm_out(hh, z, g, nrows):
    tm = 512
    nv = ML_HEADS * ML_V
    return pl.pallas_call(
        _mlstm_out_kernel,
        grid=(nrows // tm,),
        in_specs=[pl.BlockSpec((2, tm, nv), lambda i: (0, i, 0)),
                  pl.BlockSpec((tm, nv), lambda i: (i, Z_MO // nv)),
                  pl.BlockSpec((1, nv), lambda i: (0, 0))],
        out_specs=pl.BlockSpec((tm, nv), lambda i: (i, 0)),
        out_shape=jax.ShapeDtypeStruct((nrows, nv), bf16),
        compiler_params=_cp("arbitrary"),
        name="mlstm_out",
    )(hh, z, g.reshape(1, nv))


def _router_kernel(f_ref, w_ref, idx_ref, p_ref):
    logits = jnp.dot(f_ref[...].astype(bf16), w_ref[...], preferred_element_type=f32)
    lane = lax.broadcasted_iota(jnp.int32, logits.shape, 1).astype(f32)
    logits = jnp.where(lane < N_EXPERTS, logits, -jnp.inf)
    v1 = jnp.max(logits, axis=-1, keepdims=True)
    i1 = jnp.min(jnp.where(logits == v1, lane, 128.0), axis=-1, keepdims=True)
    rest = jnp.where(lane == i1, -jnp.inf, logits)
    v2 = jnp.max(rest, axis=-1, keepdims=True)
    i2 = jnp.min(jnp.where(rest == v2, lane, 128.0), axis=-1, keepdims=True)
    e = jnp.exp(v2 - v1)
    p1 = 1.0 / (1.0 + e)
    p2 = e / (1.0 + e)
    idx_ref[...] = jnp.where(lane == 0, i1, jnp.where(lane == 1, i2, 0.0)).astype(jnp.int32)
    p_ref[...] = jnp.where(lane == 0, p1, jnp.where(lane == 1, p2, 0.0))


def _router(f, w_router):
    tm = 512
    w = jnp.pad(w_router, ((0, 0), (0, 128 - N_EXPERTS))).astype(bf16)
    return pl.pallas_call(
        _router_kernel,
        grid=(NLAT // tm,),
        in_specs=[pl.BlockSpec((tm, D_MODEL), lambda i: (i, 0)),
                  pl.BlockSpec((D_MODEL, 128), lambda i: (0, 0))],
        out_specs=[pl.BlockSpec((tm, 128), lambda i: (i, 0)),
                   pl.BlockSpec((tm, 128), lambda i: (i, 0))],
        out_shape=[jax.ShapeDtypeStruct((NLAT, 128), jnp.int32),
                   jax.ShapeDtypeStruct((NLAT, 128), f32)],
        compiler_params=_cp("arbitrary"),
        name="router",
    )(f, w)


def _gather_kernel(tok_ref, src_ref, o_ref, buf, sem):
    base = pl.program_id(0) * MOE_TILE

    def copy(r):
        return pltpu.make_async_copy(src_ref.at[pl.ds(tok_ref[base + r], 1)], buf.at[pl.ds(r, 1)], sem)

    def start(r, carry):
        copy(r).start()
        return carry

    def wait(r, carry):
        copy(r).wait()
        return carry

    lax.fori_loop(0, MOE_TILE, start, 0, unroll=8)
    lax.fori_loop(0, MOE_TILE, wait, 0, unroll=8)
    o_ref[...] = buf[...].astype(o_ref.dtype)


def _gather_rows(src, rows, n_out, out_dtype):
    width = src.shape[1]
    return pl.pallas_call(
        _gather_kernel,
        grid_spec=pltpu.PrefetchScalarGridSpec(
            num_scalar_prefetch=1,
            grid=(n_out // MOE_TILE,),
            in_specs=[pl.BlockSpec(memory_space=pl.ANY)],
            out_specs=pl.BlockSpec((MOE_TILE, width), lambda i, tok: (i, 0)),
            scratch_shapes=[pltpu.VMEM((MOE_TILE, width), src.dtype), pltpu.SemaphoreType.DMA(())]),
        out_shape=jax.ShapeDtypeStruct((n_out, width), out_dtype),
        compiler_params=_cp("arbitrary"),
        name="gather_rows",
    )(rows, src)


def _moe_up_kernel(be_ref, nu_ref, a_ref, wg_ref, wu_ref, o_ref, wg_bf, wu_bf):
    i = pl.program_id(1)
    used = i < nu_ref[0]
    fresh = jnp.logical_or(i == 0, be_ref[i] != be_ref[jnp.maximum(i - 1, 0)])

    @pl.when(jnp.logical_and(used, fresh))
    def _():
        wg_bf[...] = wg_ref[...].astype(bf16)
        wu_bf[...] = wu_ref[...].astype(bf16)

    @pl.when(used)
    def _():
        a = a_ref[...]
        g = jnp.dot(a, wg_bf[...], preferred_element_type=f32)
        u = jnp.dot(a, wu_bf[...], preferred_element_type=f32)
        o_ref[...] = (g * jax.nn.sigmoid(g) * u).astype(o_ref.dtype)

    @pl.when(i >= nu_ref[0])
    def _():
        o_ref[...] = jnp.zeros_like(o_ref)


def _moe_up(xs, wg, wu, blk_expert, n_used):
    tn = 1024
    nblk = xs.shape[0] // MOE_TILE

    def row(j, i, be, nu):
        return (jnp.minimum(i, nu[0] - 1), 0)

    def wmap(j, i, be, nu):
        return (be[jnp.minimum(i, nu[0] - 1)], 0, j)

    return pl.pallas_call(
        _moe_up_kernel,
        grid_spec=pltpu.PrefetchScalarGridSpec(
            num_scalar_prefetch=2,
            grid=(FFN_EXPERT // tn, nblk),
            in_specs=[pl.BlockSpec((MOE_TILE, D_MODEL), row),
                      pl.BlockSpec((None, D_MODEL, tn), wmap),
                      pl.BlockSpec((None, D_MODEL, tn), wmap)],
            out_specs=pl.BlockSpec((MOE_TILE, tn), lambda j, i, be, nu: (i, j)),
            scratch_shapes=[pltpu.VMEM((D_MODEL, tn), bf16), pltpu.VMEM((D_MODEL, tn), bf16)]),
        out_shape=jax.ShapeDtypeStruct((xs.shape[0], FFN_EXPERT), bf16),
        compiler_params=_cp("arbitrary", "arbitrary"),
        name="moe_up",
    )(blk_expert, n_used, xs, wg, wu)


def _moe_down_kernel(be_ref, nu_ref, a_ref, w_ref, o_ref):
    i = pl.program_id(1)

    @pl.when(i < nu_ref[0])
    def _():
        o_ref[...] = jnp.dot(a_ref[...], w_ref[...], preferred_element_type=f32)

    @pl.when(i >= nu_ref[0])
    def _():
        o_ref[...] = jnp.zeros_like(o_ref)


def _moe_down(hs, wd, blk_expert, n_used):
    tn = 1024
    nblk = hs.shape[0] // MOE_TILE

    def row(j, i, be, nu):
        return (jnp.minimum(i, nu[0] - 1), 0)

    def wmap(j, i, be, nu):
        return (be[jnp.minimum(i, nu[0] - 1)], 0, j)

    return pl.pallas_call(
        _moe_down_kernel,
        grid_spec=pltpu.PrefetchScalarGridSpec(
            num_scalar_prefetch=2,
            grid=(D_MODEL // tn, nblk),
            in_specs=[pl.BlockSpec((MOE_TILE, FFN_EXPERT), row),
                      pl.BlockSpec((None, FFN_EXPERT, tn), wmap)],
            out_specs=pl.BlockSpec((MOE_TILE, tn), lambda j, i, be, nu: (i, j))),
        out_shape=jax.ShapeDtypeStruct((hs.shape[0], D_MODEL), f32),
        compiler_params=_cp("arbitrary", "arbitrary"),
        name="moe_down",
    )(blk_expert, n_used, hs, wd)


_COMBINE_TM = 256


def _combine_kernel(slot_ref, yb_ref, x_ref, p_ref, g2_ref, fg_ref, o_ref, buf, sem):
    i = pl.program_id(0)
    cur = i % 2

    def copy(step, b, r, k):
        s = slot_ref[(step * _COMBINE_TM + r) * TOP_K + k]
        return pltpu.make_async_copy(yb_ref.at[pl.ds(s, 1)], buf.at[b, k, pl.ds(r, 1)], sem.at[b])

    def fetch(step, b):
        def start(r, carry):
            copy(step, b, r, 0).start()
            copy(step, b, r, 1).start()
            return carry
        lax.fori_loop(0, _COMBINE_TM, start, 0, unroll=4)

    @pl.when(i == 0)
    def _():
        fetch(0, 0)

    @pl.when(i + 1 < pl.num_programs(0))
    def _():
        fetch(i + 1, 1 - cur)

    def wait(r, carry):
        copy(i, cur, r, 0).wait()
        copy(i, cur, r, 1).wait()
        return carry

    lax.fori_loop(0, _COMBINE_TM, wait, 0, unroll=8)
    p = p_ref[...]
    y = buf[cur, 0] * p[:, 0:1] + buf[cur, 1] * p[:, 1:2]
    x = x_ref[...] + g2_ref[...] * y
    o_ref[...] = x * lax.rsqrt(jnp.mean(x * x, axis=-1, keepdims=True) + EPS) * fg_ref[...]


def _combine_final(slot, yb, xa, probs, modt, final_g):
    tm = _COMBINE_TM
    return pl.pallas_call(
        _combine_kernel,
        grid_spec=pltpu.PrefetchScalarGridSpec(
            num_scalar_prefetch=1,
            grid=(NLAT // tm,),
            in_specs=[pl.BlockSpec(memory_space=pl.ANY),
                      pl.BlockSpec((tm, D_MODEL), lambda i, s: (i, 0)),
                      pl.BlockSpec((tm, 128), lambda i, s: (i, 0)),
                      pl.BlockSpec((None, 1, D_MODEL), lambda i, s: (i // (SEQ // tm), 0, 5)),
                      pl.BlockSpec((1, D_MODEL), lambda i, s: (0, 0))],
            out_specs=pl.BlockSpec((tm, D_MODEL), lambda i, s: (i, 0)),
            scratch_shapes=[pltpu.VMEM((2, TOP_K, tm, D_MODEL), f32),
                            pltpu.SemaphoreType.DMA((2,))]),
        out_shape=jax.ShapeDtypeStruct((NLAT, D_MODEL), f32),
        compiler_params=_cp("arbitrary"),
        name="moe_combine_final_norm",
    )(slot, yb, xa, probs, modt, final_g.reshape(1, D_MODEL))


def _moe_routing(top_i):
    a = NLAT * TOP_K
    e_flat = top_i.reshape(a)
    onehot = (e_flat[:, None] == jnp.arange(N_EXPERTS)[None, :]).astype(jnp.int32)
    csum = jnp.cumsum(onehot, axis=0)
    rank = jnp.sum(onehot * csum, axis=1) - 1
    counts = csum[-1]
    padded = (counts + MOE_TILE - 1) // MOE_TILE * MOE_TILE
    pad_end = jnp.cumsum(padded)
    pad_start = pad_end - padded
    slot = (pad_start[e_flat] + rank).astype(jnp.int32)
    n_rows = a + N_EXPERTS * MOE_TILE
    nblk = n_rows // MOE_TILE
    slot_tok = jnp.zeros((n_rows,), jnp.int32).at[slot].set(jnp.arange(a, dtype=jnp.int32) // TOP_K)
    blk_expert = jnp.minimum(jnp.searchsorted(pad_end, jnp.arange(nblk) * MOE_TILE, side='right'),
                             N_EXPERTS - 1).astype(jnp.int32)
    n_used = (pad_end[-1:] // MOE_TILE).astype(jnp.int32)
    return slot, slot_tok, blk_expert, n_used, n_rows


def _in_proj_weight(w, gate_b):
    o = np.cumsum((0,) + (3072, 1024, 256, 256, 512, 512, 1024, 1024, 16, 6144))
    hy, q, k, v, mq, mk, mv, mo, gt, mg = [w[:, o[i]:o[i + 1]] for i in range(10)]
    pad = jnp.zeros((D_MODEL, 128 - 2 * ML_HEADS), w.dtype)
    gates = [jnp.concatenate([gt[:, 8 * d:8 * d + 8], pad], axis=1) for d in range(2)]
    wz = jnp.concatenate([hy, q, mv, mo, mg, k, v, mq, mk] + gates
                         + [jnp.zeros((D_MODEL, Z_COLS - Z_GATE - 256), w.dtype)], axis=1).astype(bf16)
    gb = jnp.pad(gate_b.reshape(2, 1, 2 * ML_HEADS), ((0, 0), (0, 0), (0, 128 - 2 * ML_HEADS)))
    return wz, gb


def kernel(x, c, ctx, c_ctx, w_mod, b_mod, norm_mix_g, norm_ffn_g, w_in, hy_short_w, hy_short_b, hy_w1, hy_b1, hy_w2, hy_b2, hy_w3, hy_freq, hy_bias, att_sink, ml_gate_b, ml_norm_g, w_branch, w_out, ffn_wg, ffn_wu, ffn_wd, moe_router, moe_wg, moe_wu, moe_wd, final_g):
    xa = jnp.concatenate([x.reshape(NLAT, D_MODEL), ctx.reshape(NCTX, D_MODEL)], axis=0)
    c_all = jnp.concatenate([c, c_ctx[None], jnp.zeros((8 - BATCH - 1, D_MODEL), f32)], axis=0)
    mod = _modulation(c_all, w_mod, b_mod)
    rope_tabs = _rope_tables()
    dft_tabs = _dft_tables(SEQ)
    ctx_tabs = _short_dft_tables(CTX_LEN)
    out = None
    for layer in range(DEPTH):
        last = layer == DEPTH - 1
        modt = mod[layer].reshape(8, 1, 6 * D_MODEL)
        n_mix = NLAT if last else R

        u = _normmod(xa, norm_mix_g[layer], modt, 0, R)
        wz, gate_b = _in_proj_weight(w_in[layer], ml_gate_b[layer])
        z = _in_proj(u, wz)

        hy = (hy_short_w[layer], hy_short_b[layer], hy_w1[layer], hy_b1[layer], hy_w2[layer],
              hy_b2[layer], hy_w3[layer], hy_freq[layer], hy_bias[layer])
        a_rows = _hyena_branch(z, 0, SEQ, dft_tabs, *hy)

        qr, kr = _rope(z, rope_tabs)
        b_rows = _window_attention(qr, kr, z, att_sink[layer])

        hh = _mlstm(z, gate_b)
        c_rows = _mlstm_out(hh, z, ml_norm_g[layer], n_mix)

        if not last:
            a_rows = jnp.concatenate([a_rows, _hyena_branch(z, NLAT, CTX_LEN, ctx_tabs, *hy)], axis=0)
            b_rows = jnp.concatenate([b_rows, _context_attention(z, att_sink[layer])], axis=0)

        ymid = _merge(a_rows, b_rows, c_rows, w_branch[layer].astype(bf16), z, n_mix)
        xa = _mm_resid(ymid, w_out[layer].astype(bf16), xa, modt, 2, n_mix)

        f = _normmod(xa, norm_ffn_g[layer], modt, 3, n_mix, bf16 if layer % 2 == 0 else f32)
        if layer % 2 == 0:
            e = layer // 2
            hmid = _ffn_up(f, ffn_wg[e].astype(bf16), ffn_wu[e].astype(bf16))
            xa = _mm_resid(hmid, ffn_wd[e].astype(bf16), xa, modt, 5, R)
        else:
            e = layer // 2
            top_i, probs = _router(f, moe_router[e])
            slot, slot_tok, blk_expert, n_used, n_rows = _moe_routing(top_i[:, :TOP_K])
            xs = _gather_rows(f, slot_tok, n_rows, bf16)
            hs = _moe_up(xs, moe_wg[e], moe_wu[e], blk_expert, n_used)
            yb = _moe_down(hs, moe_wd[e].astype(bf16), blk_expert, n_used)
            assert last
            out = _combine_final(slot, yb, xa, probs, modt, final_g)
    return out.reshape(BATCH, SEQ, D_MODEL)
```
